```python
import math
import jax
import jax.numpy as jnp
from jax import lax
import numpy as np

D_MODEL = 1024
BATCH = 8
SEQ = 2048
DEPTH = 2

GRID_W = 64
CTX_LEN = 256
HEAD_DIM = 64
MIX_WIDTH = 1024
N_MOD = 9
FFN_HIDDEN = 2816
NORM_EPS = 1e-6
FOURIER_GROUPS = 4
FOURIER_WIDTH = FOURIER_GROUPS * HEAD_DIM
GLA_HEADS = 6
GLA_DK = 64
GLA_DV = 128
GLA_GATE_RANK = 16
GLA_TAU = 16.0
GLA_CHUNK = 64
CONV_GROUPS = 4
CONV_WIDTH = CONV_GROUPS * HEAD_DIM
CONV_TAPS = 3
DIFF_HEADS = 6
DIFF_DV = 2 * HEAD_DIM
Q_BLOCK = 128
ROPE_THETA = 10000.0
ROPE_AXIS_DIM = HEAD_DIM // 2
N_EVEN = (DEPTH + 1) // 2
N_ODD = DEPTH // 2
EVEN_SPLITS = (FOURIER_WIDTH, GLA_HEADS * GLA_DK, GLA_HEADS * GLA_DK, GLA_HEADS * GLA_DV, GLA_HEADS * GLA_DV, GLA_GATE_RANK, GLA_GATE_RANK)
EVEN_IN = sum(EVEN_SPLITS)
ODD_SPLITS = (CONV_WIDTH, CONV_WIDTH, CONV_WIDTH, DIFF_HEADS * 2 * HEAD_DIM, DIFF_HEADS * 2 * HEAD_DIM, DIFF_HEADS * DIFF_DV)
ODD_IN = sum(ODD_SPLITS)

kernel_name = 'hybrid_fnet_gla_shortconv_diffattn_prefix_dit'


def _split(z, sizes):
    idx = [int(i) for i in np.cumsum(sizes)[:-1]]
    return jnp.split(z, idx, axis=-1)


def _rmsnorm(x, g):
    xf = x.astype(jnp.float32)
    y = xf * lax.rsqrt(jnp.mean(xf * xf, axis=-1, keepdims=True) + NORM_EPS)
    return (y * g.astype(jnp.float32)).astype(x.dtype)


def _modulation(cond, w, b):
    m = jax.nn.silu(cond) @ w + b
    return jnp.split(m[..., None, :], N_MOD, axis=-1)


def _prenorm(h, gain, shift, scale):
    return _rmsnorm(h, gain) * (1.0 + scale) + shift


def _swiglu(h, w_in, w_out):
    g, u = jnp.split(h @ w_in, 2, axis=-1)
    return (jax.nn.silu(g) * u) @ w_out


def _axial_rope_tables(rows):
    row = jnp.repeat(jnp.arange(rows), GRID_W).astype(jnp.float32)
    col = jnp.tile(jnp.arange(GRID_W), rows).astype(jnp.float32)
    n = ROPE_AXIS_DIM // 2
    inv = ROPE_THETA ** (-jnp.arange(n, dtype=jnp.float32) / n)
    ang = jnp.concatenate([row[:, None] * inv, col[:, None] * inv], axis=-1)
    return jnp.cos(ang), jnp.sin(ang)


def _axial_rope(x, cos, sin):
    n = ROPE_AXIS_DIM // 2
    def rot(part, cs, sn):
        p1, p2 = jnp.split(part, 2, axis=-1)
        return jnp.concatenate([p1 * cs - p2 * sn, p1 * sn + p2 * cs], axis=-1)
    bc = lambda t: t[:, None, None, :]
    xr = rot(x[..., :ROPE_AXIS_DIM], bc(cos[:, :n]), bc(sin[:, :n]))
    xc = rot(x[..., ROPE_AXIS_DIM:], bc(cos[:, n:]), bc(sin[:, n:]))
    return jnp.concatenate([xr, xc], axis=-1).astype(x.dtype)


def _fourier_mix(z):
    bn, t, _ = z.shape
    zf = z.astype(jnp.float32).reshape(bn, t, FOURIER_GROUPS, HEAD_DIM)
    y = jnp.fft.fft2(zf, axes=(1, 3), norm='ortho').real
    return y.reshape(bn, t, FOURIER_WIDTH).astype(z.dtype)


def _gla_chunked(q, k, v, logg, s0):
    bn, h, t, dk = q.shape
    dv = v.shape[-1]
    n = t // GLA_CHUNK
    def chunks(a):
        return jnp.moveaxis(a.reshape(bn, h, n, GLA_CHUNK, a.shape[-1]), 2, 0)
    b = jnp.cumsum(chunks(logg), axis=-2)
    mask = jnp.tril(jnp.ones((GLA_CHUNK, GLA_CHUNK), dtype=bool))
    def step(s, inp):
        qc, kc, vc, bc = inp
        bl = bc[..., -1:, :]
        qd = qc * jnp.exp(bc)
        kd = kc * jnp.exp(-bc)
        att = jnp.where(mask, jnp.einsum('bhid,bhjd->bhij', qd, kd), 0.0)
        o = jnp.einsum('bhid,bhdv->bhiv', qd, s) + jnp.einsum('bhij,bhjv->bhiv', att, vc)
        s = s * jnp.exp(bl)[..., 0, :, None] + jnp.einsum('bhjd,bhjv->bhdv', kc * jnp.exp(bl - bc), vc)
        return s, o
    s, o = lax.scan(step, s0, (chunks(q), chunks(k), chunks(v), b))
    return jnp.moveaxis(o, 0, 2).reshape(bn, h, t, dv), s


def _even_mixer(z, s0_f, s0_b, gate_w, gate_b, gla_g):
    bn, t, _ = z.shape
    z_f, zq, zk, zv, zr, zgf, zgb = _split(z, EVEN_SPLITS)
    y_f = _fourier_mix(z_f)
    def heads(a, d):
        return a.astype(jnp.float32).reshape(bn, t, GLA_HEADS, d).transpose(0, 2, 1, 3)
    q = heads(zq, GLA_DK) * (GLA_DK ** -0.5)
    k = heads(zk, GLA_DK)
    v = heads(zv, GLA_DV)
    gw = gate_w.astype(jnp.float32)
    gb = gate_b.astype(jnp.float32)
    logg_f = heads(jax.nn.log_sigmoid(zgf.astype(jnp.float32) @ gw[0] + gb[0]), GLA_DK) / GLA_TAU
    logg_b = heads(jax.nn.log_sigmoid(zgb.astype(jnp.float32) @ gw[1] + gb[1]), GLA_DK) / GLA_TAU
    flip = lambda a: jnp.flip(a, axis=2)
    o_f, s_f = _gla_chunked(q, k, v, logg_f, s0_f)
    o_b, s_b = _gla_chunked(flip(q), flip(k), flip(v), flip(logg_b), s0_b)
    o = (o_f + flip(o_b)).transpose(0, 2, 1, 3)
    o = _rmsnorm(o, gla_g) * jax.nn.silu(zr.astype(jnp.float32).reshape(bn, t, GLA_HEADS, GLA_DV))
    y_g = o.reshape(bn, t, GLA_HEADS * GLA_DV).astype(z.dtype)
    return jnp.concatenate([y_f, y_g], axis=-1), s_f, s_b


def _short_conv(zb, zc, zx, w, b):
    u = zc * zx
    t = u.shape[1]
    up = jnp.pad(u, ((0, 0), (1, 1), (0, 0)))
    y = up[:, 0:t] * w[0] + up[:, 1:t + 1] * w[1] + up[:, 2:t + 2] * w[2] + b
    return zb * y


def _odd_parts(z):
    bn, t, _ = z.shape
    zb, zc, zx, q, k, v = _split(z, ODD_SPLITS)
    q = q.reshape(bn, t, DIFF_HEADS, 2, HEAD_DIM)
    k = k.reshape(bn, t, DIFF_HEADS, 2, HEAD_DIM)
    v = v.reshape(bn, t, DIFF_HEADS, DIFF_DV)
    return zb, zc, zx, q, k, v


def _diff_attn(q, k, v, lam):
    s = jnp.einsum('bqhsd,bkhsd->bhsqk', q, k, preferred_element_type=jnp.float32) * (HEAD_DIM ** -0.5)
    p = jax.nn.softmax(s, axis=-1)
    a = p[:, :, 0] - lam * p[:, :, 1]
    return jnp.einsum('bhqk,bkhv->bqhv', a, v.astype(jnp.float32))


def _diff_attn_blocked(q, k, v, lam):
    bn, t = q.shape[:2]
    nb = t // Q_BLOCK
    qb = jnp.moveaxis(q.reshape(bn, nb, Q_BLOCK, DIFF_HEADS, 2, HEAD_DIM), 1, 0)
    o = lax.map(lambda blk: _diff_attn(blk, k, v, lam), qb)
    return jnp.moveaxis(o, 0, 1).reshape(bn, t, DIFF_HEADS, DIFF_DV)


def _subln(o, g, lam_init, dtype):
    bn, t = o.shape[:2]
    return (_rmsnorm(o, g) * (1.0 - lam_init)).reshape(bn, t, DIFF_HEADS * DIFF_DV).astype(dtype)


def _odd_mixer(zl, zc, cos, sin, conv_w, conv_b, lam, lam_init, dnorm, with_ctx_out):
    lat = _odd_parts(zl)
    cx = _odd_parts(zc)
    ql = _axial_rope(lat[3], cos, sin)
    kl = _axial_rope(lat[4], cos, sin)
    k_all = jnp.concatenate([kl, cx[4]], axis=1)
    v_all = jnp.concatenate([lat[5], cx[5]], axis=1)
    att_l = _diff_attn_blocked(ql, k_all, v_all, lam)
    y_l = jnp.concatenate([_short_conv(lat[0], lat[1], lat[2], conv_w, conv_b),
                           _subln(att_l, dnorm, lam_init, zl.dtype)], axis=-1)
    if not with_ctx_out:
        return y_l, None
    att_c = _diff_attn(cx[3], cx[4], cx[5], lam)
    y_c = jnp.concatenate([_short_conv(cx[0], cx[1], cx[2], conv_w, conv_b),
                           _subln(att_c, dnorm, lam_init, zc.dtype)], axis=-1)
    return y_l, y_c


def setup_inputs(seed: int = 0) -> dict:
    key = jax.random.key(seed)
    ks = jax.random.split(key, 28)
    nrm = lambda k, shape, s: jax.random.normal(k, shape, jnp.float32) * s
    gain = lambda k, shape: 1.0 + 0.05 * jax.random.normal(k, shape, jnp.float32)
    D = D_MODEL
    return {
        'x': nrm(ks[0], (BATCH, SEQ, D), 1.0),
        'c': nrm(ks[1], (BATCH, D), 1.0),
        'ctx': nrm(ks[2], (BATCH, CTX_LEN, D), 1.0),
        'c_ctx': nrm(ks[3], (D,), 1.0),
        'ada_w': nrm(ks[4], (DEPTH, D, N_MOD * D), 0.5 * D ** -0.5),
        'ada_b': nrm(ks[5], (DEPTH, N_MOD * D), 0.01),
        'norm_ffn1': gain(ks[6], (DEPTH, D)),
        'norm_mix': gain(ks[7], (DEPTH, D)),
        'norm_ffn2': gain(ks[8], (DEPTH, D)),
        'ffn1_w_in': nrm(ks[9], (DEPTH, D, 2 * FFN_HIDDEN), D ** -0.5),
        'ffn1_w_out': nrm(ks[10], (DEPTH, FFN_HIDDEN, D), FFN_HIDDEN ** -0.5),
        'ffn2_w_in': nrm(ks[11], (DEPTH, D, 2 * FFN_HIDDEN), D ** -0.5),
        'ffn2_w_out': nrm(ks[12], (DEPTH, FFN_HIDDEN, D), FFN_HIDDEN ** -0.5),
        'mix_w_out': nrm(ks[13], (DEPTH, MIX_WIDTH, D), MIX_WIDTH ** -0.5),
        'even_w_in': nrm(ks[14], (N_EVEN, D, EVEN_IN), D ** -0.5),
        'gla_gate_w': nrm(ks[15], (N_EVEN, 2, GLA_GATE_RANK, GLA_HEADS * GLA_DK), GLA_GATE_RANK ** -0.5),
        'gla_gate_b': nrm(ks[16], (N_EVEN, 2, GLA_HEADS * GLA_DK), 0.1),
        'gla_norm': gain(ks[17], (N_EVEN, GLA_DV)),
        'odd_w_in': nrm(ks[18], (N_ODD, D, ODD_IN), D ** -0.5),
        'conv_w': nrm(ks[19], (N_ODD, CONV_TAPS, CONV_WIDTH), CONV_TAPS ** -0.5),
        'conv_b': nrm(ks[20], (N_ODD, CONV_WIDTH), 0.01),
        'lambda_q1': nrm(ks[21], (N_ODD, HEAD_DIM), 0.1),
        'lambda_k1': nrm(ks[22], (N_ODD, HEAD_DIM), 0.1),
        'lambda_q2': nrm(ks[23], (N_ODD, HEAD_DIM), 0.1),
        'lambda_k2': nrm(ks[24], (N_ODD, HEAD_DIM), 0.1),
        'diff_norm': gain(ks[25], (N_ODD, DIFF_DV)),
        'final_norm': gain(ks[26], (D,)),
    }


def reference(x, c, ctx, c_ctx, ada_w, ada_b, norm_ffn1, norm_mix, norm_ffn2, ffn1_w_in, ffn1_w_out,
              ffn2_w_in, ffn2_w_out, mix_w_out, even_w_in, gla_gate_w, gla_gate_b, gla_norm, odd_w_in,
              conv_w, conv_b, lambda_q1, lambda_k1, lambda_q2, lambda_k2, diff_norm, final_norm):
    bn = x.shape[0]
    rows = x.shape[1] // GRID_W
    cos, sin = _axial_rope_tables(rows)
    h, hc = x, ctx
    for layer in range(DEPTH):
        last = layer == DEPTH - 1
        sh1, sc1, g1, shm, scm, gm, sh2, sc2, g2 = _modulation(c, ada_w[layer], ada_b[layer])
        csh1, csc1, cg1, cshm, cscm, cgm, csh2, csc2, cg2 = _modulation(c_ctx, ada_w[layer], ada_b[layer])
        h = h + 0.5 * g1 * _swiglu(_prenorm(h, norm_ffn1[layer], sh1, sc1), ffn1_w_in[layer], ffn1_w_out[layer])
        hc = hc + 0.5 * cg1 * _swiglu(_prenorm(hc, norm_ffn1[layer], csh1, csc1), ffn1_w_in[layer], ffn1_w_out[layer])
        if layer % 2 == 0:
            i = layer // 2
            w_in = even_w_in[i]
            zl = _prenorm(h, norm_mix[layer], shm, scm) @ w_in
            zc = _prenorm(hc, norm_mix[layer], cshm, cscm) @ w_in
            s0 = jnp.zeros((bn, GLA_HEADS, GLA_DK, GLA_DV), jnp.float32)
            mix_c, s_f, s_b = _even_mixer(zc, s0, s0, gla_gate_w[i], gla_gate_b[i], gla_norm[i])
            mix_l, _, _ = _even_mixer(zl, s_f, s_b, gla_gate_w[i], gla_gate_b[i], gla_norm[i])
        else:
            i = layer // 2
            w_in = odd_w_in[i]
            zl = _prenorm(h, norm_mix[layer], shm, scm) @ w_in
            zc = _prenorm(hc, norm_mix[layer], cshm, cscm) @ w_in
            lam_init = 0.8 - 0.6 * math.exp(-0.3 * layer)
            lam = (jnp.exp(jnp.sum(lambda_q1[i].astype(jnp.float32) * lambda_k1[i].astype(jnp.float32)))
                   - jnp.exp(jnp.sum(lambda_q2[i].astype(jnp.float32) * lambda_k2[i].astype(jnp.float32)))
                   + lam_init)
            mix_l, mix_c = _odd_mixer(zl, zc, cos, sin, conv_w[i], conv_b[i], lam, lam_init, diff_norm[i],
                                      not last)
        h = h + gm * (mix_l @ mix_w_out[layer])
        h = h + 0.5 * g2 * _swiglu(_prenorm(h, norm_ffn2[layer], sh2, sc2), ffn2_w_in[layer], ffn2_w_out[layer])
        if not last:
            hc = hc + cgm * (mix_c @ mix_w_out[layer])
            hc = hc + 0.5 * cg2 * _swiglu(_prenorm(hc, norm_ffn2[layer], csh2, csc2), ffn2_w_in[layer], ffn2_w_out[layer])
    return _rmsnorm(h, final_norm)
```

```python
import functools
import math

import numpy as np
import jax
import jax.numpy as jnp
from jax import lax
from jax.experimental import pallas as pl
from jax.experimental.pallas import tpu as pltpu

D_MODEL = 1024
DEPTH = 2
GRID_W = 64
HEAD_DIM = 64
N_MOD = 9
FFN_HIDDEN = 2816
NORM_EPS = 1e-6
FOURIER_GROUPS = 4
FOURIER_WIDTH = FOURIER_GROUPS * HEAD_DIM
GLA_HEADS = 6
GLA_DK = 64
GLA_DV = 128
GLA_GATE_RANK = 16
GLA_TAU = 16.0
GLA_CHUNK = 64
CONV_WIDTH = 4 * HEAD_DIM
DIFF_HEADS = 6
DIFF_DV = 2 * HEAD_DIM
ROPE_THETA = 10000.0
ROPE_AXIS_DIM = HEAD_DIM // 2

EVEN_MAIN = FOURIER_WIDTH + 2 * GLA_HEADS * GLA_DK + 2 * GLA_HEADS * GLA_DV
ODD_IN = 3 * CONV_WIDTH + 3 * DIFF_HEADS * DIFF_DV

MOD_ROWS = 16
VMEM_LIMIT = 48 * 1024 * 1024

F32 = jnp.float32
BF16 = jnp.bfloat16
HIGHEST = lax.Precision.HIGHEST


def _params(*sem):
    return pltpu.CompilerParams(dimension_semantics=sem, vmem_limit_bytes=VMEM_LIMIT)


def _sigmoid(x):
    return 1.0 / (1.0 + jnp.exp(-x))


def _dot(a, b):
    return jnp.dot(a, b, preferred_element_type=F32)


def _dot_nt(a, b):
    return lax.dot_general(a, b, (((1,), (1,)), ((), ())), preferred_element_type=F32)


def _dot_tn(a, b):
    return lax.dot_general(a, b, (((0,), (0,)), ((), ())), preferred_element_type=F32)


def _rms(x):
    return x * lax.rsqrt(jnp.mean(x * x, axis=-1, keepdims=True) + NORM_EPS)


def _mod_kernel(cond_ref, w_ref, b_ref, o_ref):
    c = cond_ref[...]
    s = c * _sigmoid(c)
    o_ref[...] = jnp.dot(s, w_ref[...], precision=HIGHEST, preferred_element_type=F32) + b_ref[...]


def _modulation(cond, ada_w, ada_b):
    n = N_MOD * D_MODEL
    tn = n // 8
    out = pl.pallas_call(
        _mod_kernel,
        out_shape=jax.ShapeDtypeStruct((DEPTH, MOD_ROWS, n), F32),
        grid=(DEPTH, n // tn),
        in_specs=[
            pl.BlockSpec((MOD_ROWS, D_MODEL), lambda l, j: (0, 0)),
            pl.BlockSpec((None, D_MODEL, tn), lambda l, j: (l, 0, j)),
            pl.BlockSpec((None, 1, tn), lambda l, j: (l, 0, j)),
        ],
        out_specs=pl.BlockSpec((None, MOD_ROWS, tn), lambda l, j: (l, 0, j)),
        compiler_params=_params("parallel", "parallel"),
        name="modulation",
    )(cond, ada_w, ada_b.reshape(DEPTH, 1, n))
    return out.reshape(DEPTH, MOD_ROWS, N_MOD, 1, D_MODEL)


def _mod_spec(layer, k, row_of_block):
    return pl.BlockSpec((None, None, None, 1, D_MODEL),
                        lambda i, *_: (layer, row_of_block(i), k, 0, 0))


def _gain_spec(layer):
    return pl.BlockSpec((None, 1, D_MODEL), lambda i, *_: (layer, 0, 0))


def _ffn_kernel(*refs, n_f, with_mix, with_final):
    it = iter(refs)
    h_ref = next(it)
    if with_mix:
        ma_ref, mb_ref, wa_ref, wb_ref, gm_ref = (next(it) for _ in range(5))
    sh_ref, sc_ref, g_ref, gain_ref, wg_ref, wu_ref, wo_ref = (next(it) for _ in range(7))
    if with_final:
        fn_ref = next(it)
    o_ref, hres_scr, xn_scr, acc_scr = (next(it) for _ in range(4))
    j = pl.program_id(1)

    @pl.when(j == 0)
    def _():
        x = h_ref[...]
        if with_mix:
            x = x + gm_ref[...] * (_dot(ma_ref[...], wa_ref[...]) + _dot(mb_ref[...], wb_ref[...]))
        hres_scr[...] = x
        y = _rms(x) * gain_ref[...] * (1.0 + sc_ref[...]) + sh_ref[...]
        xn_scr[...] = y.astype(BF16)
        acc_scr[...] = jnp.zeros_like(acc_scr)

    xn = xn_scr[...]
    g = _dot(xn, wg_ref[...])
    u = _dot(xn, wu_ref[...])
    a = (g * _sigmoid(g) * u).astype(BF16)
    acc_scr[...] += _dot(a, wo_ref[...])

    @pl.when(j == n_f - 1)
    def _():
        out = hres_scr[...] + (0.5 * g_ref[...]) * acc_scr[...]
        if with_final:
            out = _rms(out) * fn_ref[...]
        o_ref[...] = out


def _ffn(h, mods, layer, mod_base, row_of_block, gain, w_in, w_out, *, tm, mix=None, final_gain=None):
    m = h.shape[0]
    tf = 256
    n_f = FFN_HIDDEN // tf
    row = lambda i, j: (i, 0)
    args = [h]
    specs = [pl.BlockSpec((tm, D_MODEL), row)]
    if mix is not None:
        ma, mb, wa, wb = mix
        ka, kb = ma.shape[1], mb.shape[1]
        args += [ma, mb, wa, wb, mods]
        specs += [
            pl.BlockSpec((tm, ka), row),
            pl.BlockSpec((tm, kb), row),
            pl.BlockSpec((None, ka, D_MODEL), lambda i, j: (layer, 0, 0)),
            pl.BlockSpec((None, kb, D_MODEL), lambda i, j: (layer, 0, 0)),
            _mod_spec(layer, 5, row_of_block),
        ]
    args += [mods, mods, mods, gain, w_in, w_in, w_out]
    specs += [
        _mod_spec(layer, mod_base, row_of_block),
        _mod_spec(layer, mod_base + 1, row_of_block),
        _mod_spec(layer, mod_base + 2, row_of_block),
        _gain_spec(layer),
        pl.BlockSpec((None, D_MODEL, tf), lambda i, j: (layer, 0, j)),
        pl.BlockSpec((None, D_MODEL, tf), lambda i, j: (layer, 0, j + n_f)),
        pl.BlockSpec((None, tf, D_MODEL), lambda i, j: (layer, j, 0)),
    ]
    if final_gain is not None:
        args.append(final_gain)
        specs.append(pl.BlockSpec((1, D_MODEL), lambda i, j: (0, 0)))
    return pl.pallas_call(
        functools.partial(_ffn_kernel, n_f=n_f, with_mix=mix is not None, with_final=final_gain is not None),
        out_shape=jax.ShapeDtypeStruct((m, D_MODEL), F32),
        grid=(m // tm, n_f),
        in_specs=specs,
        out_specs=pl.BlockSpec((tm, D_MODEL), row),
        scratch_shapes=[pltpu.VMEM((tm, D_MODEL), F32), pltpu.VMEM((tm, D_MODEL), BF16),
                        pltpu.VMEM((tm, D_MODEL), F32)],
        compiler_params=_params("parallel", "arbitrary"),
        name="ffn",
    )(*args)


def _proj_kernel(*refs, with_gate):
    it = iter(refs)
    h_ref, sh_ref, sc_ref, gain_ref, w_ref = (next(it) for _ in range(5))
    if with_gate:
        wg_ref = next(it)
    o_ref = next(it)
    if with_gate:
        og_ref = next(it)
    xn_scr = next(it)
    j = pl.program_id(1)

    @pl.when(j == 0)
    def _():
        y = _rms(h_ref[...]) * gain_ref[...] * (1.0 + sc_ref[...]) + sh_ref[...]
        xn_scr[...] = y.astype(BF16)
        if with_gate:
            og_ref[...] = _dot(xn_scr[...], wg_ref[...])

    o_ref[...] = _dot(xn_scr[...], w_ref[...])


def _proj_in(h, mods, layer, row_of_block, gain, w, *, n_main, col0, tm, tn, w_gate=None):
    m = h.shape[0]
    args = [h, mods, mods, gain, w]
    specs = [
        pl.BlockSpec((tm, D_MODEL), lambda i, j: (i, 0)),
        _mod_spec(layer, 3, row_of_block),
        _mod_spec(layer, 4, row_of_block),
        _gain_spec(layer),
        pl.BlockSpec((D_MODEL, tn), lambda i, j: (0, j + col0)),
    ]
    out_shape = [jax.ShapeDtypeStruct((m, n_main), F32)]
    out_specs = [pl.BlockSpec((tm, tn), lambda i, j: (i, j))]
    if w_gate is not None:
        ng = w_gate.shape[1]
        args.append(w_gate)
        specs.append(pl.BlockSpec((D_MODEL, ng), lambda i, j: (0, 0)))
        out_shape.append(jax.ShapeDtypeStruct((m, ng), F32))
        out_specs.append(pl.BlockSpec((tm, ng), lambda i, j: (i, 0)))
    outs = pl.pallas_call(
        functools.partial(_proj_kernel, with_gate=w_gate is not None),
        out_shape=out_shape,
        grid=(m // tm, n_main // tn),
        in_specs=specs,
        out_specs=out_specs,
        scratch_shapes=[pltpu.VMEM((tm, D_MODEL), BF16)],
        compiler_params=_params("parallel", "arbitrary"),
        name="proj_in",
    )(*args)
    return outs if w_gate is not None else outs[0]


def _dft_tables(t):
    k = (np.arange(t, dtype=np.int64)[:, None] * np.arange(t, dtype=np.int64)[None, :]) % t
    ang = 2.0 * np.pi * k.astype(np.float64) / t
    pos = np.concatenate([np.cos(ang), -np.sin(ang)], axis=1).astype(np.float32)
    kc = (np.arange(HEAD_DIM)[:, None] * np.arange(HEAD_DIM)[None, :]) % HEAD_DIM
    angc = 2.0 * np.pi * kc.astype(np.float64) / HEAD_DIM
    eye = np.eye(FOURIER_GROUPS)
    chan = np.concatenate([np.kron(eye, np.cos(angc)), np.kron(eye, np.sin(angc))], axis=1).astype(np.float32)
    return pos, chan


def _chan_dft_kernel(z_ref, c_ref, o_ref, *, t):
    ab = _dot(z_ref[...].astype(BF16), c_ref[...].astype(BF16))
    o_ref[0:t, :] = ab[:, :FOURIER_WIDTH].astype(BF16)
    o_ref[t:2 * t, :] = ab[:, FOURIER_WIDTH:].astype(BF16)


def _pos_dft_kernel(p_ref, ab_ref, o_ref, p_scr, *, scale):
    @pl.when(pl.program_id(1) == 0)
    def _():
        p_scr[...] = p_ref[...].astype(BF16)

    o_ref[...] = (_dot(p_scr[...], ab_ref[...]) * scale).astype(BF16)


def _fourier(z, t, bn):
    pos, chan = _dft_tables(t)
    ab = pl.pallas_call(
        functools.partial(_chan_dft_kernel, t=t),
        out_shape=jax.ShapeDtypeStruct((bn, 2 * t, FOURIER_WIDTH), BF16),
        grid=(bn,),
        in_specs=[pl.BlockSpec((None, t, FOURIER_WIDTH), lambda b: (b, 0, 0)),
                  pl.BlockSpec((FOURIER_WIDTH, 2 * FOURIER_WIDTH), lambda b: (0, 0))],
        out_specs=pl.BlockSpec((None, 2 * t, FOURIER_WIDTH), lambda b: (b, 0, 0)),
        compiler_params=_params("parallel"),
        name="chan_dft",
    )(z, jnp.asarray(chan))
    tq = min(t, 512)
    return pl.pallas_call(
        functools.partial(_pos_dft_kernel, scale=1.0 / math.sqrt(t * HEAD_DIM)),
        out_shape=jax.ShapeDtypeStruct((bn, t, FOURIER_WIDTH), BF16),
        grid=(t // tq, bn),
        in_specs=[pl.BlockSpec((tq, 2 * t), lambda i, b: (i, 0)),
                  pl.BlockSpec((None, 2 * t, FOURIER_WIDTH), lambda i, b: (b, 0, 0))],
        out_specs=pl.BlockSpec((None, tq, FOURIER_WIDTH), lambda i, b: (b, i, 0)),
        scratch_shapes=[pltpu.VMEM((tq, 2 * t), BF16)],
        compiler_params=_params("parallel", "arbitrary"),
        name="pos_dft",
    )(jnp.asarray(pos), ab)


def _gla_segment(q_ref, k_ref, v_ref, r_ref, g_ref, y_ref, gw_ref, gb_ref, gn_ref,
                 b_scr, o_scr, s_scr, n_rows):
    c_len = GLA_CHUNK
    n = n_rows // c_len
    pair_k = 2 * GLA_DK
    pair_v = 2 * GLA_DV

    r_in = lax.broadcasted_iota(jnp.int32, (n_rows, pair_k), 0) & (c_len - 1)
    for d in range(2):
        zg = g_ref[:, d * GLA_GATE_RANK:(d + 1) * GLA_GATE_RANK]
        logit = jnp.dot(zg, gw_ref[d], precision=HIGHEST, preferred_element_type=F32) + gb_ref[d]
        x = (jnp.minimum(logit, 0.0) - jnp.log(1.0 + jnp.exp(-jnp.abs(logit)))) * (1.0 / GLA_TAU)
        step = 1
        while step < c_len:
            if d == 0:
                x = x + jnp.where(r_in >= step, pltpu.roll(x, step, 0), 0.0)
            else:
                x = x + jnp.where(r_in < c_len - step, pltpu.roll(x, n_rows - step, 0), 0.0)
            step *= 2
        b_scr[d, 0:n_rows, :] = x

    lane = lax.broadcasted_iota(jnp.int32, (c_len, pair_k), 1)
    col = lax.broadcasted_iota(jnp.int32, (c_len, pair_v), 1)
    ti = lax.broadcasted_iota(jnp.int32, (c_len, c_len), 0)
    tj = lax.broadcasted_iota(jnp.int32, (c_len, c_len), 1)
    blk = (lax.broadcasted_iota(jnp.int32, (pair_v, pair_k), 0) // GLA_DV
           == lax.broadcasted_iota(jnp.int32, (pair_v, pair_k), 1) // GLA_DK)

    def body(i, carry):
        for d in range(2):
            c = i if d == 0 else n - 1 - i
            r0 = pl.multiple_of(c * c_len, c_len)
            b = b_scr[d, pl.ds(r0, c_len), :]
            bl = b[c_len - 1:c_len, :] if d == 0 else b[0:1, :]
            q = q_ref[pl.ds(r0, c_len), :]
            k = k_ref[pl.ds(r0, c_len), :]
            v = v_ref[pl.ds(r0, c_len), :]
            qd = q * (GLA_DK ** -0.5) * jnp.exp(b)
            kd = (k * jnp.exp(-b)).astype(BF16)
            kdec = (k * jnp.exp(bl - b)).astype(BF16)
            s_prev = s_scr[d]
            o = _dot_nt(qd.astype(BF16), s_prev.astype(BF16))
            keep = (tj <= ti) if d == 0 else (tj >= ti)
            for hh in range(2):
                qh = jnp.where((lane // GLA_DK) == hh, qd, 0.0).astype(BF16)
                att = jnp.where(keep, _dot_nt(qh, kd), 0.0).astype(BF16)
                vh = jnp.where((col // GLA_DV) == hh, v, 0.0).astype(BF16)
                o = o + _dot(att, vh)
            o_scr[d, pl.ds(r0, c_len), :] = o
            upd = _dot_tn(v.astype(BF16), kdec)
            s_scr[d] = jnp.where(blk, s_prev * jnp.exp(bl) + upd, 0.0)
        return carry

    lax.fori_loop(0, n, body, 0)

    gn = gn_ref[...]
    for hh in range(2):
        sl = slice(hh * GLA_DV, (hh + 1) * GLA_DV)
        o = o_scr[0, 0:n_rows, sl] + o_scr[1, 0:n_rows, sl]
        rr = r_ref[:, sl]
        y_ref[:, sl] = (_rms(o) * gn * (rr * _sigmoid(rr))).astype(y_ref.dtype)


def _gla_kernel(qc_ref, kc_ref, vc_ref, rc_ref, gc_ref, ql_ref, kl_ref, vl_ref, rl_ref, gl_ref,
                gw_ref, gb_ref, gn_ref, yc_ref, yl_ref, b_scr, o_scr, s_scr, *, t_ctx, t_lat):
    s_scr[...] = jnp.zeros_like(s_scr)
    _gla_segment(qc_ref, kc_ref, vc_ref, rc_ref, gc_ref, yc_ref, gw_ref, gb_ref, gn_ref,
                 b_scr, o_scr, s_scr, t_ctx)
    _gla_segment(ql_ref, kl_ref, vl_ref, rl_ref, gl_ref, yl_ref, gw_ref, gb_ref, gn_ref,
                 b_scr, o_scr, s_scr, t_lat)


def _gla(zc, gc, zl, gl, gate_w, gate_b, gla_g, bn, t_ctx, t_lat):
    pk, pv = 2 * GLA_DK, 2 * GLA_DV
    q0 = FOURIER_WIDTH // pk
    k0 = (FOURIER_WIDTH + GLA_HEADS * GLA_DK) // pk
    v0 = (FOURIER_WIDTH + 2 * GLA_HEADS * GLA_DK) // pv
    r0 = (FOURIER_WIDTH + 2 * GLA_HEADS * GLA_DK + GLA_HEADS * GLA_DV) // pv

    def seg_specs(t):
        return [pl.BlockSpec((None, t, pk), lambda b, p: (b, 0, q0 + p)),
                pl.BlockSpec((None, t, pk), lambda b, p: (b, 0, k0 + p)),
                pl.BlockSpec((None, t, pv), lambda b, p: (b, 0, v0 + p)),
                pl.BlockSpec((None, t, pv), lambda b, p: (b, 0, r0 + p)),
                pl.BlockSpec((None, t, 2 * GLA_GATE_RANK), lambda b, p: (b, 0, 0))]

    wdt = GLA_HEADS * GLA_DV
    return pl.pallas_call(
        functools.partial(_gla_kernel, t_ctx=t_ctx, t_lat=t_lat),
        out_shape=[jax.ShapeDtypeStruct((bn, t_ctx, wdt), BF16), jax.ShapeDtypeStruct((bn, t_lat, wdt), BF16)],
        grid=(bn, GLA_HEADS // 2),
        in_specs=seg_specs(t_ctx) + seg_specs(t_lat) + [
            pl.BlockSpec((2, GLA_GATE_RANK, pk), lambda b, p: (0, 0, p)),
            pl.BlockSpec((2, 1, pk), lambda b, p: (0, 0, p)),
            pl.BlockSpec((1, GLA_DV), lambda b, p: (0, 0))],
        out_specs=[pl.BlockSpec((None, t_ctx, pv), lambda b, p: (b, 0, p)),
                   pl.BlockSpec((None, t_lat, pv), lambda b, p: (b, 0, p))],
        scratch_shapes=[pltpu.VMEM((2, t_lat, pk), F32), pltpu.VMEM((2, t_lat, pv), F32),
                        pltpu.VMEM((2, pv, pk), F32)],
        compiler_params=_params("parallel", "parallel"),
        name="gla",
    )(zc, zc, zc, zc, gc, zl, zl, zl, zl, gl, gate_w, gate_b.reshape(2, 1, GLA_HEADS * GLA_DK),
      gla_g.reshape(1, GLA_DV))


def _conv_kernel(zb_ref, zc_ref, zx_ref, w_ref, b_ref, o_ref, *, t):
    u = zc_ref[...] * zx_ref[...]
    row = lax.broadcasted_iota(jnp.int32, u.shape, 0)
    prev = jnp.where(row >= 1, pltpu.roll(u, 1, 0), 0.0)
    nxt = jnp.where(row < t - 1, pltpu.roll(u, t - 1, 0), 0.0)
    y = prev * w_ref[0:1, :] + u * w_ref[1:2, :] + nxt * w_ref[2:3, :] + b_ref[...]
    o_ref[...] = (zb_ref[...] * y).astype(o_ref.dtype)


def _short_conv(z, conv_w, conv_b, bn, t):
    cw = CONV_WIDTH
    return pl.pallas_call(
        functools.partial(_conv_kernel, t=t),
        out_shape=jax.ShapeDtypeStruct((bn, t, cw), BF16),
        grid=(bn,),
        in_specs=[pl.BlockSpec((None, t, cw), lambda b: (b, 0, 0)),
                  pl.BlockSpec((None, t, cw), lambda b: (b, 0, 1)),
                  pl.BlockSpec((None, t, cw), lambda b: (b, 0, 2)),
                  pl.BlockSpec((3, cw), lambda b: (0, 0)),
                  pl.BlockSpec((1, cw), lambda b: (0, 0))],
        out_specs=pl.BlockSpec((None, t, cw), lambda b: (b, 0, 0)),
        compiler_params=_params("parallel"),
        name="short_conv",
    )(z, z, z, conv_w, conv_b.reshape(1, cw))


def _rope_tables(t):
    rows = t // GRID_W
    row = jnp.repeat(jnp.arange(rows), GRID_W).astype(F32)
    col = jnp.tile(jnp.arange(GRID_W), rows).astype(F32)
    n = ROPE_AXIS_DIM // 2
    inv = ROPE_THETA ** (-jnp.arange(n, dtype=F32) / n)
    ar, ac = row[:, None] * inv, col[:, None] * inv
    ang = jnp.concatenate([ar, ar, ac, ac], axis=-1)
    sign = jnp.tile(jnp.concatenate([-jnp.ones((n,), F32), jnp.ones((n,), F32)]), 2)
    cos = jnp.cos(ang)
    sin = jnp.sin(ang) * sign
    return jnp.tile(cos, (1, 2)), jnp.tile(sin, (1, 2))


def _rope(x, cos, sin):
    lane = lax.broadcasted_iota(jnp.int32, x.shape, 1)
    n = ROPE_AXIS_DIM // 2
    w = x.shape[1]
    partner = jnp.where((lane & (2 * n - 1)) < n, pltpu.roll(x, w - n, 1), pltpu.roll(x, n, 1))
    return x * cos + partner * sin


def _attn_kernel(lam_ref, q_ref, kl_ref, vl_ref, kc_ref, vc_ref, cq_ref, sq_ref, ck_ref, sk_ref, dn_ref,
                 o_ref, k_scr, v_scr, *, t_lat, lam_init):
    @pl.when(pl.program_id(2) == 0)
    def _():
        k_scr[0:t_lat, :] = _rope(kl_ref[...], ck_ref[...], sk_ref[...]).astype(BF16)
        k_scr[t_lat:, :] = kc_ref[...].astype(BF16)
        v_scr[0:t_lat, :] = vl_ref[...].astype(BF16)
        v_scr[t_lat:, :] = vc_ref[...].astype(BF16)

    lv = lam_ref[...]
    lam = (jnp.exp(jnp.sum(lv[0:1] * lv[1:2], axis=-1, keepdims=True))
           - jnp.exp(jnp.sum(lv[2:3] * lv[3:4], axis=-1, keepdims=True)) + lam_init)
    q = _rope(q_ref[...], cq_ref[...], sq_ref[...]) * (HEAD_DIM ** -0.5)
    lane = lax.broadcasted_iota(jnp.int32, q.shape, 1)
    k = k_scr[...]
    v = v_scr[...]
    outs = []
    for half in range(2):
        qh = jnp.where((lane // HEAD_DIM) == half, q, 0.0).astype(BF16)
        s = _dot_nt(qh, k)
        p = jnp.exp(s - jnp.max(s, axis=-1, keepdims=True))
        l = jnp.sum(p, axis=-1, keepdims=True)
        outs.append(_dot(p.astype(BF16), v) / l)
    o = outs[0] - lam * outs[1]
    o_ref[...] = (_rms(o) * dn_ref[...] * (1.0 - lam_init)).astype(o_ref.dtype)


def _diff_attn(zl, zc, lam_vecs, dnorm, lam_init, bn, t_lat, t_ctx, c_k0):
    hw = 2 * HEAD_DIM
    q0 = 3 * CONV_WIDTH // hw
    k0 = q0 + DIFF_HEADS
    v0 = k0 + DIFF_HEADS
    tq = 256
    t_all = t_lat + t_ctx
    cos, sin = _rope_tables(t_lat)
    return pl.pallas_call(
        functools.partial(_attn_kernel, t_lat=t_lat, lam_init=lam_init),
        out_shape=jax.ShapeDtypeStruct((bn, t_lat, DIFF_HEADS * DIFF_DV), BF16),
        grid=(bn, DIFF_HEADS, t_lat // tq),
        in_specs=[
            pl.BlockSpec((4, HEAD_DIM), lambda b, h, i: (0, 0)),
            pl.BlockSpec((None, tq, hw), lambda b, h, i: (b, i, q0 + h)),
            pl.BlockSpec((None, t_lat, hw), lambda b, h, i: (b, 0, k0 + h)),
            pl.BlockSpec((None, t_lat, hw), lambda b, h, i: (b, 0, v0 + h)),
            pl.BlockSpec((None, t_ctx, hw), lambda b, h, i: (b, 0, c_k0 + h)),
            pl.BlockSpec((None, t_ctx, hw), lambda b, h, i: (b, 0, c_k0 + DIFF_HEADS + h)),
            pl.BlockSpec((tq, hw), lambda b, h, i: (i, 0)),
            pl.BlockSpec((tq, hw), lambda b, h, i: (i, 0)),
            pl.BlockSpec((t_lat, hw), lambda b, h, i: (0, 0)),
            pl.BlockSpec((t_lat, hw), lambda b, h, i: (0, 0)),
            pl.BlockSpec((1, DIFF_DV), lambda b, h, i: (0, 0)),
        ],
        out_specs=pl.BlockSpec((None, tq, hw), lambda b, h, i: (b, i, h)),
        scratch_shapes=[pltpu.VMEM((t_all, hw), BF16), pltpu.VMEM((t_all, hw), BF16)],
        compiler_params=_params("parallel", "parallel", "arbitrary"),
        name="diff_attn",
    )(lam_vecs, zl, zl, zl, zc, zc, cos, sin, cos, sin, dnorm.reshape(1, DIFF_DV))


def kernel(x, c, ctx, c_ctx, ada_w, ada_b, norm_ffn1, norm_mix, norm_ffn2, ffn1_w_in, ffn1_w_out, ffn2_w_in,
           ffn2_w_out, mix_w_out, even_w_in, gla_gate_w, gla_gate_b, gla_norm, odd_w_in, conv_w, conv_b,
           lambda_q1, lambda_k1, lambda_q2, lambda_k2, diff_norm, final_norm):
    assert DEPTH == 2
    bn, t_lat, d = x.shape
    t_ctx = ctx.shape[1]
    assert bn < MOD_ROWS
    ctx_row = bn

    cond = jnp.concatenate([c, c_ctx[None, :], jnp.zeros((MOD_ROWS - bn - 1, d), F32)], axis=0)
    mods = _modulation(cond, ada_w, ada_b)

    tm = 1024
    lat_row = lambda i: i // (t_lat // tm)
    ctx_rows = lambda i: ctx_row
    g1 = norm_ffn1.reshape(DEPTH, 1, d)
    gm = norm_mix.reshape(DEPTH, 1, d)
    g2 = norm_ffn2.reshape(DEPTH, 1, d)
    w1i, w1o = ffn1_w_in.astype(BF16), ffn1_w_out.astype(BF16)
    w2i, w2o = ffn2_w_in.astype(BF16), ffn2_w_out.astype(BF16)
    wmix_a = mix_w_out[:, :FOURIER_WIDTH, :].astype(BF16)
    wmix_b = mix_w_out[:, FOURIER_WIDTH:, :].astype(BF16)

    h = x.reshape(bn * t_lat, d)
    hc = ctx.reshape(bn * t_ctx, d)

    h = _ffn(h, mods, 0, 0, lat_row, g1, w1i, w1o, tm=tm)
    hc = _ffn(hc, mods, 0, 0, ctx_rows, g1, w1i, w1o, tm=tm)
    w_even = even_w_in[0].astype(BF16)
    w_gate = w_even[:, EVEN_MAIN:]
    zl, gl = _proj_in(h, mods, 0, lat_row, gm, w_even, n_main=EVEN_MAIN, col0=0, tm=tm, tn=512, w_gate=w_gate)
    zc, gc = _proj_in(hc, mods, 0, ctx_rows, gm, w_even, n_main=EVEN_MAIN, col0=0, tm=tm, tn=512, w_gate=w_gate)
    zl = zl.reshape(bn, t_lat, EVEN_MAIN)
    zc = zc.reshape(bn, t_ctx, EVEN_MAIN)
    gl = gl.reshape(bn, t_lat, 2 * GLA_GATE_RANK)
    gc = gc.reshape(bn, t_ctx, 2 * GLA_GATE_RANK)
    yf_l = _fourier(zl, t_lat, bn)
    yf_c = _fourier(zc, t_ctx, bn)
    yg_c, yg_l = _gla(zc, gc, zl, gl, gla_gate_w[0], gla_gate_b[0], gla_norm[0], bn, t_ctx, t_lat)
    mix_l = (yf_l.reshape(bn * t_lat, -1), yg_l.reshape(bn * t_lat, -1), wmix_a, wmix_b)
    mix_c = (yf_c.reshape(bn * t_ctx, -1), yg_c.reshape(bn * t_ctx, -1), wmix_a, wmix_b)
    h = _ffn(h, mods, 0, 6, lat_row, g2, w2i, w2o, tm=tm, mix=mix_l)
    hc = _ffn(hc, mods, 0, 6, ctx_rows, g2, w2i, w2o, tm=tm, mix=mix_c)

    h = _ffn(h, mods, 1, 0, lat_row, g1, w1i, w1o, tm=tm)
    hc = _ffn(hc, mods, 1, 0, ctx_rows, g1, w1i, w1o, tm=tm)
    w_odd = odd_w_in[0].astype(BF16)
    kv_w = 2 * DIFF_HEADS * DIFF_DV
    zl = _proj_in(h, mods, 1, lat_row, gm, w_odd, n_main=ODD_IN, col0=0, tm=tm, tn=512)
    zc = _proj_in(hc, mods, 1, ctx_rows, gm, w_odd, n_main=kv_w, col0=(ODD_IN - kv_w) // 512, tm=tm, tn=512)
    zl = zl.reshape(bn, t_lat, ODD_IN)
    zc = zc.reshape(bn, t_ctx, kv_w)
    lam_init = 0.8 - 0.6 * math.exp(-0.3 * 1)
    lam_vecs = jnp.stack([lambda_q1[0], lambda_k1[0], lambda_q2[0], lambda_k2[0]]).astype(F32)
    y_conv = _short_conv(zl, conv_w[0], conv_b[0], bn, t_lat)
    y_att = _diff_attn(zl, zc, lam_vecs, diff_norm[0], lam_init, bn, t_lat, t_ctx, 0)
    mix_l = (y_conv.reshape(bn * t_lat, -1), y_att.reshape(bn * t_lat, -1), wmix_a, wmix_b)
    h = _ffn(h, mods, 1, 6, lat_row, g2, w2i, w2o, tm=tm, mix=mix_l, final_gain=final_norm.reshape(1, d))
    return h.reshape(bn, t_lat, d)
```

```python
import functools
import math

import numpy as np
import jax
import jax.numpy as jnp
from jax import lax
from jax.experimental import pallas as pl
from jax.experimental.pallas import tpu as pltpu

D_MODEL = 1024
DEPTH = 2
GRID_W = 64
HEAD_DIM = 64
N_MOD = 9
FFN_HIDDEN = 2816
NORM_EPS = 1e-6
FOURIER_GROUPS = 4
FOURIER_WIDTH = FOURIER_GROUPS * HEAD_DIM
GLA_HEADS = 6
GLA_DK = 64
GLA_DV = 128
GLA_GATE_RANK = 16
GLA_TAU = 16.0
GLA_CHUNK = 64
CONV_WIDTH = 4 * HEAD_DIM
DIFF_HEADS = 6
DIFF_DV = 2 * HEAD_DIM
ROPE_THETA = 10000.0
ROPE_AXIS_DIM = HEAD_DIM // 2

EVEN_MAIN = FOURIER_WIDTH + 2 * GLA_HEADS * GLA_DK + 2 * GLA_HEADS * GLA_DV
ODD_IN = 3 * CONV_WIDTH + 3 * DIFF_HEADS * DIFF_DV

MOD_ROWS = 16
VMEM_LIMIT = 48 * 1024 * 1024

F32 = jnp.float32
BF16 = jnp.bfloat16
HIGHEST = lax.Precision.HIGHEST


def _params(*sem):
    return pltpu.CompilerParams(dimension_semantics=sem, vmem_limit_bytes=VMEM_LIMIT)


def _sigmoid(x):
    return 1.0 / (1.0 + jnp.exp(-x))


def _dot(a, b):
    return jnp.dot(a, b, preferred_element_type=F32)


def _dot_nt(a, b):
    return lax.dot_general(a, b, (((1,), (1,)), ((), ())), preferred_element_type=F32)


def _dot_tn(a, b):
    return lax.dot_general(a, b, (((0,), (0,)), ((), ())), preferred_element_type=F32)


def _rms(x):
    return x * lax.rsqrt(jnp.mean(x * x, axis=-1, keepdims=True) + NORM_EPS)


def _mod_kernel(cond_ref, w_ref, b_ref, o_ref):
    c = cond_ref[...]
    s = c * _sigmoid(c)
    o_ref[...] = jnp.dot(s, w_ref[...], precision=HIGHEST, preferred_element_type=F32) + b_ref[...]


def _modulation(cond, ada_w, ada_b):
    n = N_MOD * D_MODEL
    tn = n // 8
    out = pl.pallas_call(
        _mod_kernel,
        out_shape=jax.ShapeDtypeStruct((DEPTH, MOD_ROWS, n), F32),
        grid=(DEPTH, n // tn),
        in_specs=[
            pl.BlockSpec((MOD_ROWS, D_MODEL), lambda l, j: (0, 0)),
            pl.BlockSpec((None, D_MODEL, tn), lambda l, j: (l, 0, j)),
            pl.BlockSpec((None, 1, tn), lambda l, j: (l, 0, j)),
        ],
        out_specs=pl.BlockSpec((None, MOD_ROWS, tn), lambda l, j: (l, 0, j)),
        compiler_params=_params("parallel", "parallel"),
        name="modulation",
    )(cond, ada_w, ada_b.reshape(DEPTH, 1, n))
    return out.reshape(DEPTH, MOD_ROWS, N_MOD, 1, D_MODEL)


def _mod_spec(layer, k, row_of_block):
    return pl.BlockSpec((None, None, None, 1, D_MODEL),
                        lambda i, *_: (layer, row_of_block(i), k, 0, 0))


def _gain_spec(layer):
    return pl.BlockSpec((None, 1, D_MODEL), lambda i, *_: (layer, 0, 0))


def _ffn_kernel(*refs, n_f, with_mix, with_final):
    it = iter(refs)
    h_ref = next(it)
    if with_mix:
        ma_ref, mb_ref, wa_ref, wb_ref, gm_ref = (next(it) for _ in range(5))
    sh_ref, sc_ref, g_ref, gain_ref, wg_ref, wu_ref, wo_ref = (next(it) for _ in range(7))
    if with_final:
        fn_ref = next(it)
    o_ref, hres_scr, xn_scr, acc_scr = (next(it) for _ in range(4))
    j = pl.program_id(1)

    @pl.when(j == 0)
    def _():
        x = h_ref[...]
        if with_mix:
            x = x + gm_ref[...] * (_dot(ma_ref[...], wa_ref[...]) + _dot(mb_ref[...], wb_ref[...]))
        hres_scr[...] = x
        y = _rms(x) * gain_ref[...] * (1.0 + sc_ref[...]) + sh_ref[...]
        xn_scr[...] = y.astype(BF16)
        acc_scr[...] = jnp.zeros_like(acc_scr)

    xn = xn_scr[...]
    g = _dot(xn, wg_ref[...])
    u = _dot(xn, wu_ref[...])
    a = (g * _sigmoid(g) * u).astype(BF16)
    acc_scr[...] += _dot(a, wo_ref[...])

    @pl.when(j == n_f - 1)
    def _():
        out = hres_scr[...] + (0.5 * g_ref[...]) * acc_scr[...]
        if with_final:
            out = _rms(out) * fn_ref[...]
        o_ref[...] = out


def _ffn(h, mods, layer, mod_base, row_of_block, gain, w_in, w_out, *, tm, mix=None, final_gain=None):
    m = h.shape[0]
    tf = 256
    n_f = FFN_HIDDEN // tf
    row = lambda i, j: (i, 0)
    args = [h]
    specs = [pl.BlockSpec((tm, D_MODEL), row)]
    if mix is not None:
        ma, mb, wa, wb = mix
        ka, kb = ma.shape[1], mb.shape[1]
        args += [ma, mb, wa, wb, mods]
        specs += [
            pl.BlockSpec((tm, ka), row),
            pl.BlockSpec((tm, kb), row),
            pl.BlockSpec((None, ka, D_MODEL), lambda i, j: (layer, 0, 0)),
            pl.BlockSpec((None, kb, D_MODEL), lambda i, j: (layer, 0, 0)),
            _mod_spec(layer, 5, row_of_block),
        ]
    args += [mods, mods, mods, gain, w_in, w_in, w_out]
    specs += [
        _mod_spec(layer, mod_base, row_of_block),
        _mod_spec(layer, mod_base + 1, row_of_block),
        _mod_spec(layer, mod_base + 2, row_of_block),
        _gain_spec(layer),
        pl.BlockSpec((None, D_MODEL, tf), lambda i, j: (layer, 0, j)),
        pl.BlockSpec((None, D_MODEL, tf), lambda i, j: (layer, 0, j + n_f)),
        pl.BlockSpec((None, tf, D_MODEL), lambda i, j: (layer, j, 0)),
    ]
    if final_gain is not None:
        args.append(final_gain)
        specs.append(pl.BlockSpec((1, D_MODEL), lambda i, j: (0, 0)))
    return pl.pallas_call(
        functools.partial(_ffn_kernel, n_f=n_f, with_mix=mix is not None, with_final=final_gain is not None),
        out_shape=jax.ShapeDtypeStruct((m, D_MODEL), F32),
        grid=(m // tm, n_f),
        in_specs=specs,
        out_specs=pl.BlockSpec((tm, D_MODEL), row),
        scratch_shapes=[pltpu.VMEM((tm, D_MODEL), F32), pltpu.VMEM((tm, D_MODEL), BF16),
                        pltpu.VMEM((tm, D_MODEL), F32)],
        compiler_params=_params("parallel", "arbitrary"),
        name="ffn",
    )(*args)


def _proj_kernel(*refs, with_gate):
    it = iter(refs)
    h_ref, sh_ref, sc_ref, gain_ref, w_ref = (next(it) for _ in range(5))
    if with_gate:
        wg_ref = next(it)
    o_ref = next(it)
    if with_gate:
        og_ref = next(it)
    xn_scr = next(it)
    j = pl.program_id(1)

    @pl.when(j == 0)
    def _():
        y = _rms(h_ref[...]) * gain_ref[...] * (1.0 + sc_ref[...]) + sh_ref[...]
        xn_scr[...] = y.astype(BF16)
        if with_gate:
            og_ref[...] = _dot(xn_scr[...], wg_ref[...])

    o_ref[...] = _dot(xn_scr[...], w_ref[...])


def _proj_in(h, mods, layer, row_of_block, gain, w, *, n_main, col0, tm, tn, w_gate=None):
    m = h.shape[0]
    args = [h, mods, mods, gain, w]
    specs = [
        pl.BlockSpec((tm, D_MODEL), lambda i, j: (i, 0)),
        _mod_spec(layer, 3, row_of_block),
        _mod_spec(layer, 4, row_of_block),
        _gain_spec(layer),
        pl.BlockSpec((D_MODEL, tn), lambda i, j: (0, j + col0)),
    ]
    out_shape = [jax.ShapeDtypeStruct((m, n_main), F32)]
    out_specs = [pl.BlockSpec((tm, tn), lambda i, j: (i, j))]
    if w_gate is not None:
        ng = w_gate.shape[1]
        args.append(w_gate)
        specs.append(pl.BlockSpec((D_MODEL, ng), lambda i, j: (0, 0)))
        out_shape.append(jax.ShapeDtypeStruct((m, ng), F32))
        out_specs.append(pl.BlockSpec((tm, ng), lambda i, j: (i, 0)))
    outs = pl.pallas_call(
        functools.partial(_proj_kernel, with_gate=w_gate is not None),
        out_shape=out_shape,
        grid=(m // tm, n_main // tn),
        in_specs=specs,
        out_specs=out_specs,
        scratch_shapes=[pltpu.VMEM((tm, D_MODEL), BF16)],
        compiler_params=_params("parallel", "arbitrary"),
        name="proj_in",
    )(*args)
    return outs if w_gate is not None else outs[0]


def _dft_tables(t):
    k = (np.arange(t, dtype=np.int64)[:, None] * np.arange(t, dtype=np.int64)[None, :]) % t
    ang = 2.0 * np.pi * k.astype(np.float64) / t
    pos = np.concatenate([np.cos(ang), -np.sin(ang)], axis=1).astype(np.float32)
    kc = (np.arange(HEAD_DIM)[:, None] * np.arange(HEAD_DIM)[None, :]) % HEAD_DIM
    angc = 2.0 * np.pi * kc.astype(np.float64) / HEAD_DIM
    eye = np.eye(FOURIER_GROUPS)
    chan = np.concatenate([np.kron(eye, np.cos(angc)), np.kron(eye, np.sin(angc))], axis=1).astype(np.float32)
    return pos, chan


def _chan_dft_kernel(z_ref, c_ref, o_ref, *, t):
    ab = _dot(z_ref[...].astype(BF16), c_ref[...].astype(BF16))
    o_ref[0:t, :] = ab[:, :FOURIER_WIDTH].astype(BF16)
    o_ref[t:2 * t, :] = ab[:, FOURIER_WIDTH:].astype(BF16)


def _pos_dft_kernel(p_ref, ab_ref, o_ref, p_scr, *, scale):
    @pl.when(pl.program_id(1) == 0)
    def _():
        p_scr[...] = p_ref[...].astype(BF16)

    o_ref[...] = (_dot(p_scr[...], ab_ref[...]) * scale).astype(BF16)


def _fourier(z, t, bn):
    pos, chan = _dft_tables(t)
    ab = pl.pallas_call(
        functools.partial(_chan_dft_kernel, t=t),
        out_shape=jax.ShapeDtypeStruct((bn, 2 * t, FOURIER_WIDTH), BF16),
        grid=(bn,),
        in_specs=[pl.BlockSpec((None, t, FOURIER_WIDTH), lambda b: (b, 0, 0)),
                  pl.BlockSpec((FOURIER_WIDTH, 2 * FOURIER_WIDTH), lambda b: (0, 0))],
        out_specs=pl.BlockSpec((None, 2 * t, FOURIER_WIDTH), lambda b: (b, 0, 0)),
        compiler_params=_params("parallel"),
        name="chan_dft",
    )(z, jnp.asarray(chan))
    tq = min(t, 512)
    return pl.pallas_call(
        functools.partial(_pos_dft_kernel, scale=1.0 / math.sqrt(t * HEAD_DIM)),
        out_shape=jax.ShapeDtypeStruct((bn, t, FOURIER_WIDTH), BF16),
        grid=(t // tq, bn),
        in_specs=[pl.BlockSpec((tq, 2 * t), lambda i, b: (i, 0)),
                  pl.BlockSpec((None, 2 * t, FOURIER_WIDTH), lambda i, b: (b, 0, 0))],
        out_specs=pl.BlockSpec((None, tq, FOURIER_WIDTH), lambda i, b: (b, i, 0)),
        scratch_shapes=[pltpu.VMEM((tq, 2 * t), BF16)],
        compiler_params=_params("parallel", "arbitrary"),
        name="pos_dft",
    )(jnp.asarray(pos), ab)


def _gla_segment(q_ref, k_ref, v_ref, r_ref, g_ref, y_ref, gw_ref, gb_ref, gn_ref,
                 b_scr, o_scr, s_scr, n_rows):
    c_len = GLA_CHUNK
    n = n_rows // c_len
    pair_k = 2 * GLA_DK
    pair_v = 2 * GLA_DV

    r_in = lax.broadcasted_iota(jnp.int32, (n_rows, pair_k), 0) & (c_len - 1)
    for d in range(2):
        zg = g_ref[:, d * GLA_GATE_RANK:(d + 1) * GLA_GATE_RANK]
        logit = jnp.dot(zg, gw_ref[d], precision=HIGHEST, preferred_element_type=F32) + gb_ref[d]
        x = (jnp.minimum(logit, 0.0) - jnp.log(1.0 + jnp.exp(-jnp.abs(logit)))) * (1.0 / GLA_TAU)
        step = 1
        while step < c_len:
            if d == 0:
                x = x + jnp.where(r_in >= step, pltpu.roll(x, step, 0), 0.0)
            else:
                x = x + jnp.where(r_in < c_len - step, pltpu.roll(x, n_rows - step, 0), 0.0)
            step *= 2
        b_scr[d, 0:n_rows, :] = x

    lane = lax.broadcasted_iota(jnp.int32, (c_len, pair_k), 1)
    col = lax.broadcasted_iota(jnp.int32, (c_len, pair_v), 1)
    ti = lax.broadcasted_iota(jnp.int32, (c_len, c_len), 0)
    tj = lax.broadcasted_iota(jnp.int32, (c_len, c_len), 1)
    blk = (lax.broadcasted_iota(jnp.int32, (pair_v, pair_k), 0) // GLA_DV
           == lax.broadcasted_iota(jnp.int32, (pair_v, pair_k), 1) // GLA_DK)

    def body(i, carry):
        for d in range(2):
            c = i if d == 0 else n - 1 - i
            r0 = pl.multiple_of(c * c_len, c_len)
            b = b_scr[d, pl.ds(r0, c_len), :]
            bl = b[c_len - 1:c_len, :] if d == 0 else b[0:1, :]
            q = q_ref[pl.ds(r0, c_len), :]
            k = k_ref[pl.ds(r0, c_len), :]
            v = v_ref[pl.ds(r0, c_len), :]
            qd = q * (GLA_DK ** -0.5) * jnp.exp(b)
            kd = (k * jnp.exp(-b)).astype(BF16)
            kdec = (k * jnp.exp(bl - b)).astype(BF16)
            s_prev = s_scr[d]
            o = _dot_nt(qd.astype(BF16), s_prev.astype(BF16))
            keep = (tj <= ti) if d == 0 else (tj >= ti)
            for hh in range(2):
                qh = jnp.where((lane // GLA_DK) == hh, qd, 0.0).astype(BF16)
                att = jnp.where(keep, _dot_nt(qh, kd), 0.0).astype(BF16)
                vh = jnp.where((col // GLA_DV) == hh, v, 0.0).astype(BF16)
                o = o + _dot(att, vh)
            o_scr[d, pl.ds(r0, c_len), :] = o
            upd = _dot_tn(v.astype(BF16), kdec)
            s_scr[d] = jnp.where(blk, s_prev * jnp.exp(bl) + upd, 0.0)
        return carry

    lax.fori_loop(0, n, body, 0)

    gn = gn_ref[...]
    for hh in range(2):
        sl = slice(hh * GLA_DV, (hh + 1) * GLA_DV)
        o = o_scr[0, 0:n_rows, sl] + o_scr[1, 0:n_rows, sl]
        rr = r_ref[:, sl]
        y_ref[:, sl] = (_rms(o) * gn * (rr * _sigmoid(rr))).astype(y_ref.dtype)


def _gla_kernel(qc_ref, kc_ref, vc_ref, rc_ref, gc_ref, ql_ref, kl_ref, vl_ref, rl_ref, gl_ref,
                gw_ref, gb_ref, gn_ref, yc_ref, yl_ref, b_scr, o_scr, s_scr, *, t_ctx, t_lat):
    s_scr[...] = jnp.zeros_like(s_scr)
    _gla_segment(qc_ref, kc_ref, vc_ref, rc_ref, gc_ref, yc_ref, gw_ref, gb_ref, gn_ref,
                 b_scr, o_scr, s_scr, t_ctx)
    _gla_segment(ql_ref, kl_ref, vl_ref, rl_ref, gl_ref, yl_ref, gw_ref, gb_ref, gn_ref,
                 b_scr, o_scr, s_scr, t_lat)


def _gla(zc, gc, zl, gl, gate_w, gate_b, gla_g, bn, t_ctx, t_lat):
    pk, pv = 2 * GLA_DK, 2 * GLA_DV
    q0 = FOURIER_WIDTH // pk
    k0 = (FOURIER_WIDTH + GLA_HEADS * GLA_DK) // pk
    v0 = (FOURIER_WIDTH + 2 * GLA_HEADS * GLA_DK) // pv
    r0 = (FOURIER_WIDTH + 2 * GLA_HEADS * GLA_DK + GLA_HEADS * GLA_DV) // pv

    def seg_specs(t):
        return [pl.BlockSpec((None, t, pk), lambda b, p: (b, 0, q0 + p)),
                pl.BlockSpec((None, t, pk), lambda b, p: (b, 0, k0 + p)),
                pl.BlockSpec((None, t, pv), lambda b, p: (b, 0, v0 + p)),
                pl.BlockSpec((None, t, pv), lambda b, p: (b, 0, r0 + p)),
                pl.BlockSpec((None, t, 2 * GLA_GATE_RANK), lambda b, p: (b, 0, 0))]

    wdt = GLA_HEADS * GLA_DV
    return pl.pallas_call(
        functools.partial(_gla_kernel, t_ctx=t_ctx, t_lat=t_lat),
        out_shape=[jax.ShapeDtypeStruct((bn, t_ctx, wdt), BF16), jax.ShapeDtypeStruct((bn, t_lat, wdt), BF16)],
        grid=(bn, GLA_HEADS // 2),
        in_specs=seg_specs(t_ctx) + seg_specs(t_lat) + [
            pl.BlockSpec((2, GLA_GATE_RANK, pk), lambda b, p: (0, 0, p)),
            pl.BlockSpec((2, 1, pk), lambda b, p: (0, 0, p)),
            pl.BlockSpec((1, GLA_DV), lambda b, p: (0, 0))],
        out_specs=[pl.BlockSpec((None, t_ctx, pv), lambda b, p: (b, 0, p)),
                   pl.BlockSpec((None, t_lat, pv), lambda b, p: (b, 0, p))],
        scratch_shapes=[pltpu.VMEM((2, t_lat, pk), F32), pltpu.VMEM((2, t_lat, pv), F32),
                        pltpu.VMEM((2, pv, pk), F32)],
        compiler_params=_params("parallel", "parallel"),
        name="gla",
    )(zc, zc, zc, zc, gc, zl, zl, zl, zl, gl, gate_w, gate_b.reshape(2, 1, GLA_HEADS * GLA_DK),
      gla_g.reshape(1, GLA_DV))


def _conv_kernel(zb_ref, zc_ref, zx_ref, w_ref, b_ref, o_ref, *, t):
    u = zc_ref[...] * zx_ref[...]
    row = lax.broadcasted_iota(jnp.int32, u.shape, 0)
    prev = jnp.where(row >= 1, pltpu.roll(u, 1, 0), 0.0)
    nxt = jnp.where(row < t - 1, pltpu.roll(u, t - 1, 0), 0.0)
    y = prev * w_ref[0:1, :] + u * w_ref[1:2, :] + nxt * w_ref[2:3, :] + b_ref[...]
    o_ref[...] = (zb_ref[...] * y).astype(o_ref.dtype)


def _short_conv(z, conv_w, conv_b, bn, t):
    cw = CONV_WIDTH
    return pl.pallas_call(
        functools.partial(_conv_kernel, t=t),
        out_shape=jax.ShapeDtypeStruct((bn, t, cw), BF16),
        grid=(bn,),
        in_specs=[pl.BlockSpec((None, t, cw), lambda b: (b, 0, 0)),
                  pl.BlockSpec((None, t, cw), lambda b: (b, 0, 1)),
                  pl.BlockSpec((None, t, cw), lambda b: (b, 0, 2)),
                  pl.BlockSpec((3, cw), lambda b: (0, 0)),
                  pl.BlockSpec((1, cw), lambda b: (0, 0))],
        out_specs=pl.BlockSpec((None, t, cw), lambda b: (b, 0, 0)),
        compiler_params=_params("parallel"),
        name="short_conv",
    )(z, z, z, conv_w, conv_b.reshape(1, cw))


def _rope_tables(t):
    rows = t // GRID_W
    row = jnp.repeat(jnp.arange(rows), GRID_W).astype(F32)
    col = jnp.tile(jnp.arange(GRID_W), rows).astype(F32)
    n = ROPE_AXIS_DIM // 2
    inv = ROPE_THETA ** (-jnp.arange(n, dtype=F32) / n)
    ar, ac = row[:, None] * inv, col[:, None] * inv
    ang = jnp.concatenate([ar, ar, ac, ac], axis=-1)
    sign = jnp.tile(jnp.concatenate([-jnp.ones((n,), F32), jnp.ones((n,), F32)]), 2)
    cos = jnp.cos(ang)
    sin = jnp.sin(ang) * sign
    return jnp.tile(cos, (1, 2)), jnp.tile(sin, (1, 2))


def _rope(x, cos, sin):
    lane = lax.broadcasted_iota(jnp.int32, x.shape, 1)
    n = ROPE_AXIS_DIM // 2
    w = x.shape[1]
    partner = jnp.where((lane & (2 * n - 1)) < n, pltpu.roll(x, w - n, 1), pltpu.roll(x, n, 1))
    return x * cos + partner * sin


ATTN_ONES_ROWS = 16


def _attn_kernel(lam_ref, q_ref, kl_ref, vl_ref, kc_ref, vc_ref, cos_ref, sin_ref, dn_ref,
                 o_ref, k_scr, vt_scr, s_scr, m_scr, *, t_lat, lam_init, tq):
    hw = 2 * HEAD_DIM
    t_all = k_scr.shape[0]

    k_scr[0:t_lat, :] = _rope(kl_ref[...], cos_ref[...], sin_ref[...]).astype(BF16)
    k_scr[t_lat:, :] = kc_ref[...].astype(BF16)
    vt_scr[0:hw, 0:t_lat] = vl_ref[...].T.astype(BF16)
    vt_scr[0:hw, t_lat:] = vc_ref[...].T.astype(BF16)
    vt_scr[hw:, :] = jnp.ones((ATTN_ONES_ROWS, t_all), BF16)

    lv = lam_ref[...]
    lam = (jnp.exp(jnp.sum(lv[0:1] * lv[1:2], axis=-1, keepdims=True))
           - jnp.exp(jnp.sum(lv[2:3] * lv[3:4], axis=-1, keepdims=True)) + lam_init)
    lane = lax.broadcasted_iota(jnp.int32, (tq, hw), 1)

    n_blocks = t_lat // tq

    def scores(i):
        rows = pl.ds(pl.multiple_of(i * tq, tq), tq)
        q = _rope(q_ref[rows, :], cos_ref[rows, :], sin_ref[rows, :]) * (HEAD_DIM ** -0.5 * math.log2(math.e))
        slot = i % 2
        for half in range(2):
            qb = jnp.where((lane // HEAD_DIM) == half, q, 0.0).astype(BF16)
            s = _dot_nt(k_scr[...], qb)
            s_scr[slot, half] = s
            m_scr[slot, half] = jnp.max(s, axis=0, keepdims=True)

    def outputs(i):
        rows = pl.ds(pl.multiple_of(i * tq, tq), tq)
        slot = i % 2
        outs = []
        for half in range(2):
            p = jnp.exp2(s_scr[slot, half] - m_scr[slot, half]).astype(BF16)
            acc = _dot(vt_scr[...], p)
            outs.append(acc[0:hw, :] / acc[hw:hw + 1, :])
        o = (outs[0] - lam * outs[1]).T
        o_ref[rows, :] = (_rms(o) * dn_ref[...] * (1.0 - lam_init)).astype(o_ref.dtype)

    scores(0)

    def body(i, carry):
        scores(i + 1)
        outputs(i)
        return carry

    lax.fori_loop(0, n_blocks - 1, body, 0)
    outputs(n_blocks - 1)


def _diff_attn(zl, zc, lam_vecs, dnorm, lam_init, bn, t_lat, t_ctx, c_k0):
    hw = 2 * HEAD_DIM
    q0 = 3 * CONV_WIDTH // hw
    k0 = q0 + DIFF_HEADS
    v0 = k0 + DIFF_HEADS
    tq = 256
    t_all = t_lat + t_ctx
    cos, sin = _rope_tables(t_lat)
    return pl.pallas_call(
        functools.partial(_attn_kernel, t_lat=t_lat, lam_init=lam_init, tq=tq),
        out_shape=jax.ShapeDtypeStruct((bn, t_lat, DIFF_HEADS * DIFF_DV), BF16),
        grid=(bn, DIFF_HEADS),
        in_specs=[
            pl.BlockSpec((4, HEAD_DIM), lambda b, h: (0, 0)),
            pl.BlockSpec((None, t_lat, hw), lambda b, h: (b, 0, q0 + h)),
            pl.BlockSpec((None, t_lat, hw), lambda b, h: (b, 0, k0 + h)),
            pl.BlockSpec((None, t_lat, hw), lambda b, h: (b, 0, v0 + h)),
            pl.BlockSpec((None, t_ctx, hw), lambda b, h: (b, 0, c_k0 + h)),
            pl.BlockSpec((None, t_ctx, hw), lambda b, h: (b, 0, c_k0 + DIFF_HEADS + h)),
            pl.BlockSpec((t_lat, hw), lambda b, h: (0, 0)),
            pl.BlockSpec((t_lat, hw), lambda b, h: (0, 0)),
            pl.BlockSpec((1, DIFF_DV), lambda b, h: (0, 0)),
        ],
        out_specs=pl.BlockSpec((None, t_lat, hw), lambda b, h: (b, 0, h)),
        scratch_shapes=[pltpu.VMEM((t_all, hw), BF16), pltpu.VMEM((hw + ATTN_ONES_ROWS, t_all), BF16),
                        pltpu.VMEM((2, 2, t_all, tq), F32), pltpu.VMEM((2, 2, 1, tq), F32)],
        compiler_params=_params("parallel", "parallel"),
        name="diff_attn",
    )(lam_vecs, zl, zl, zl, zc, zc, cos, sin, dnorm.reshape(1, DIFF_DV))


def kernel(x, c, ctx, c_ctx, ada_w, ada_b, norm_ffn1, norm_mix, norm_ffn2, ffn1_w_in, ffn1_w_out, ffn2_w_in,
           ffn2_w_out, mix_w_out, even_w_in, gla_gate_w, gla_gate_b, gla_norm, odd_w_in, conv_w, conv_b,
           lambda_q1, lambda_k1, lambda_q2, lambda_k2, diff_norm, final_norm):
    assert DEPTH == 2
    bn, t_lat, d = x.shape
    t_ctx = ctx.shape[1]
    assert bn < MOD_ROWS
    ctx_row = bn

    cond = jnp.concatenate([c, c_ctx[None, :], jnp.zeros((MOD_ROWS - bn - 1, d), F32)], axis=0)
    mods = _modulation(cond, ada_w, ada_b)

    tm = 1024
    lat_row = lambda i: i // (t_lat // tm)
    ctx_rows = lambda i: ctx_row
    g1 = norm_ffn1.reshape(DEPTH, 1, d)
    gm = norm_mix.reshape(DEPTH, 1, d)
    g2 = norm_ffn2.reshape(DEPTH, 1, d)
    w1i, w1o = ffn1_w_in.astype(BF16), ffn1_w_out.astype(BF16)
    w2i, w2o = ffn2_w_in.astype(BF16), ffn2_w_out.astype(BF16)
    wmix_a = mix_w_out[:, :FOURIER_WIDTH, :].astype(BF16)
    wmix_b = mix_w_out[:, FOURIER_WIDTH:, :].astype(BF16)

    h = x.reshape(bn * t_lat, d)
    hc = ctx.reshape(bn * t_ctx, d)

    h = _ffn(h, mods, 0, 0, lat_row, g1, w1i, w1o, tm=tm)
    hc = _ffn(hc, mods, 0, 0, ctx_rows, g1, w1i, w1o, tm=tm)
    w_even = even_w_in[0].astype(BF16)
    w_gate = w_even[:, EVEN_MAIN:]
    zl, gl = _proj_in(h, mods, 0, lat_row, gm, w_even, n_main=EVEN_MAIN, col0=0, tm=tm, tn=512, w_gate=w_gate)
    zc, gc = _proj_in(hc, mods, 0, ctx_rows, gm, w_even, n_main=EVEN_MAIN, col0=0, tm=tm, tn=512, w_gate=w_gate)
    zl = zl.reshape(bn, t_lat, EVEN_MAIN)
    zc = zc.reshape(bn, t_ctx, EVEN_MAIN)
    gl = gl.reshape(bn, t_lat, 2 * GLA_GATE_RANK)
    gc = gc.reshape(bn, t_ctx, 2 * GLA_GATE_RANK)
    yf_l = _fourier(zl, t_lat, bn)
    yf_c = _fourier(zc, t_ctx, bn)
    yg_c, yg_l = _gla(zc, gc, zl, gl, gla_gate_w[0], gla_gate_b[0], gla_norm[0], bn, t_ctx, t_lat)
    mix_l = (yf_l.reshape(bn * t_lat, -1), yg_l.reshape(bn * t_lat, -1), wmix_a, wmix_b)
    mix_c = (yf_c.reshape(bn * t_ctx, -1), yg_c.reshape(bn * t_ctx, -1), wmix_a, wmix_b)
    h = _ffn(h, mods, 0, 6, lat_row, g2, w2i, w2o, tm=tm, mix=mix_l)
    hc = _ffn(hc, mods, 0, 6, ctx_rows, g2, w2i, w2o, tm=tm, mix=mix_c)

    h = _ffn(h, mods, 1, 0, lat_row, g1, w1i, w1o, tm=tm)
    hc = _ffn(hc, mods, 1, 0, ctx_rows, g1, w1i, w1o, tm=tm)
    w_odd = odd_w_in[0].astype(BF16)
    kv_w = 2 * DIFF_HEADS * DIFF_DV
    zl = _proj_in(h, mods, 1, lat_row, gm, w_odd, n_main=ODD_IN, col0=0, tm=tm, tn=512)
    zc = _proj_in(hc, mods, 1, ctx_rows, gm, w_odd, n_main=kv_w, col0=(ODD_IN - kv_w) // 512, tm=tm, tn=512)
    zl = zl.reshape(bn, t_lat, ODD_IN)
    zc = zc.reshape(bn, t_ctx, kv_w)
    lam_init = 0.8 - 0.6 * math.exp(-0.3 * 1)
    lam_vecs = jnp.stack([lambda_q1[0], lambda_k1[0], lambda_q2[0], lambda_k2[0]]).astype(F32)
    y_conv = _short_conv(zl, conv_w[0], conv_b[0], bn, t_lat)
    y_att = _diff_attn(zl, zc, lam_vecs, diff_norm[0], lam_init, bn, t_lat, t_ctx, 0)
    mix_l = (y_conv.reshape(bn * t_lat, -1), y_att.reshape(bn * t_lat, -1), wmix_a, wmix_b)
    h = _ffn(h, mods, 1, 6, lat_row, g2, w2i, w2o, tm=tm, mix=mix_l, final_gain=final_norm.reshape(1, d))
    return h.reshape(bn, t_lat, d)
```

```python
import functools
import math

import numpy as np
import jax
import jax.numpy as jnp
from jax import lax
from jax.experimental import pallas as pl
from jax.experimental.pallas import tpu as pltpu

D_MODEL = 1024
DEPTH = 2
GRID_W = 64
HEAD_DIM = 64
N_MOD = 9
FFN_HIDDEN = 2816
NORM_EPS = 1e-6
FOURIER_GROUPS = 4
FOURIER_WIDTH = FOURIER_GROUPS * HEAD_DIM
GLA_HEADS = 6
GLA_DK = 64
GLA_DV = 128
GLA_GATE_RANK = 16
GLA_TAU = 16.0
GLA_CHUNK = 64
GLA_GROUP = 4
FFN_TILE = 256
PROJ_TILE = 512
CONV_WIDTH = 4 * HEAD_DIM
DIFF_HEADS = 6
DIFF_DV = 2 * HEAD_DIM
ROPE_THETA = 10000.0
ROPE_AXIS_DIM = HEAD_DIM // 2

EVEN_MAIN = FOURIER_WIDTH + 2 * GLA_HEADS * GLA_DK + 2 * GLA_HEADS * GLA_DV
ODD_IN = 3 * CONV_WIDTH + 3 * DIFF_HEADS * DIFF_DV

MOD_ROWS = 16
VMEM_LIMIT = 48 * 1024 * 1024

F32 = jnp.float32
BF16 = jnp.bfloat16
HIGHEST = lax.Precision.HIGHEST


def _params(*sem):
    return pltpu.CompilerParams(dimension_semantics=sem, vmem_limit_bytes=VMEM_LIMIT)


def _sigmoid(x):
    return 1.0 / (1.0 + jnp.exp(-x))


def _dot(a, b):
    return jnp.dot(a, b, preferred_element_type=F32)


def _dot_nt(a, b):
    return lax.dot_general(a, b, (((1,), (1,)), ((), ())), preferred_element_type=F32)


def _dot_tn(a, b):
    return lax.dot_general(a, b, (((0,), (0,)), ((), ())), preferred_element_type=F32)


def _rms(x):
    return x * lax.rsqrt(jnp.mean(x * x, axis=-1, keepdims=True) + NORM_EPS)


def _mod_kernel(cond_ref, w_ref, b_ref, o_ref):
    c = cond_ref[...]
    s = c * _sigmoid(c)
    o_ref[...] = jnp.dot(s, w_ref[...], precision=HIGHEST, preferred_element_type=F32) + b_ref[...]


def _modulation(cond, ada_w, ada_b):
    n = N_MOD * D_MODEL
    tn = n // 8
    out = pl.pallas_call(
        _mod_kernel,
        out_shape=jax.ShapeDtypeStruct((DEPTH, MOD_ROWS, n), F32),
        grid=(DEPTH, n // tn),
        in_specs=[
            pl.BlockSpec((MOD_ROWS, D_MODEL), lambda l, j: (0, 0)),
            pl.BlockSpec((None, D_MODEL, tn), lambda l, j: (l, 0, j)),
            pl.BlockSpec((None, 1, tn), lambda l, j: (l, 0, j)),
        ],
        out_specs=pl.BlockSpec((None, MOD_ROWS, tn), lambda l, j: (l, 0, j)),
        compiler_params=_params("parallel", "parallel"),
        name="modulation",
    )(cond, ada_w, ada_b.reshape(DEPTH, 1, n))
    return out.reshape(DEPTH, MOD_ROWS, N_MOD, 1, D_MODEL)


def _mod_spec(layer, k, row_of_block):
    return pl.BlockSpec((None, None, None, 1, D_MODEL),
                        lambda i, *_: (layer, row_of_block(i), k, 0, 0))


def _gain_spec(layer):
    return pl.BlockSpec((None, 1, D_MODEL), lambda i, *_: (layer, 0, 0))


def _resident(block_shape, index_map):
    return pl.BlockSpec(block_shape, index_map, pipeline_mode=pl.Buffered(1))


def _ffn_kernel(*refs, with_mix, with_proj, with_gate, with_final):
    it = iter(refs)
    h_ref = next(it)
    if with_mix:
        ma_ref, mb_ref, wa_ref, wb_ref, gm_ref = (next(it) for _ in range(5))
    sh_ref, sc_ref, g_ref, gain_ref, wi_ref, wo_ref = (next(it) for _ in range(6))
    if with_proj:
        psh_ref, psc_ref, pgain_ref, wp_ref = (next(it) for _ in range(4))
    if with_gate:
        wgt_ref = next(it)
    if with_final:
        fn_ref = next(it)
    o_ref = next(it)
    if with_proj:
        z_ref = next(it)
    if with_gate:
        zg_ref = next(it)
    a_scr = next(it)

    x = h_ref[...]
    if with_mix:
        x = x + gm_ref[...] * (_dot(ma_ref[...], wa_ref[...]) + _dot(mb_ref[...], wb_ref[...]))
    xn = (_rms(x) * gain_ref[...] * (1.0 + sc_ref[...]) + sh_ref[...]).astype(BF16)
    for j in range(FFN_HIDDEN // FFN_TILE):
        lo = j * FFN_TILE
        g = _dot(xn, wi_ref[:, lo:lo + FFN_TILE])
        u = _dot(xn, wi_ref[:, FFN_HIDDEN + lo:FFN_HIDDEN + lo + FFN_TILE])
        a_scr[:, lo:lo + FFN_TILE] = (g * _sigmoid(g) * u).astype(BF16)
    out = x + (0.5 * g_ref[...]) * _dot(a_scr[...], wo_ref[...])
    if with_final:
        out = _rms(out) * fn_ref[...]
    o_ref[...] = out
    if with_proj:
        xm = (_rms(out) * pgain_ref[...] * (1.0 + psc_ref[...]) + psh_ref[...]).astype(BF16)
        n_out = z_ref.shape[1]
        for lo in range(0, n_out, PROJ_TILE):
            z_ref[:, lo:lo + PROJ_TILE] = _dot(xm, wp_ref[:, lo:lo + PROJ_TILE]).astype(z_ref.dtype)
        if with_gate:
            zg_ref[...] = _dot(xm, wgt_ref[...])


def _ffn(h, mods, layer, mod_base, row_of_block, gain, w_in, w_out, *, tm, mix=None, proj=None, final_gain=None):
    m = h.shape[0]
    row = lambda i: (i, 0)
    args = [h]
    specs = [pl.BlockSpec((tm, D_MODEL), row)]
    if mix is not None:
        ma, mb, wa, wb = mix
        ka, kb = ma.shape[1], mb.shape[1]
        args += [ma, mb, wa, wb, mods]
        specs += [
            pl.BlockSpec((tm, ka), row),
            pl.BlockSpec((tm, kb), row),
            _resident((None, ka, D_MODEL), lambda i: (layer, 0, 0)),
            _resident((None, kb, D_MODEL), lambda i: (layer, 0, 0)),
            _mod_spec(layer, 5, row_of_block),
        ]
    args += [mods, mods, mods, gain, w_in, w_out]
    specs += [
        _mod_spec(layer, mod_base, row_of_block),
        _mod_spec(layer, mod_base + 1, row_of_block),
        _mod_spec(layer, mod_base + 2, row_of_block),
        _gain_spec(layer),
        _resident((None, D_MODEL, 2 * FFN_HIDDEN), lambda i: (layer, 0, 0)),
        _resident((None, FFN_HIDDEN, D_MODEL), lambda i: (layer, 0, 0)),
    ]
    out_shape = [jax.ShapeDtypeStruct((m, D_MODEL), F32)]
    out_specs = [pl.BlockSpec((tm, D_MODEL), row)]
    with_gate = False
    if proj is not None:
        pgain, wp, w_gate = proj
        n_out = wp.shape[1]
        assert n_out % PROJ_TILE == 0
        args += [mods, mods, pgain, wp]
        specs += [_mod_spec(layer, 3, row_of_block), _mod_spec(layer, 4, row_of_block), _gain_spec(layer),
                  _resident((D_MODEL, n_out), lambda i: (0, 0))]
        out_shape.append(jax.ShapeDtypeStruct((m, n_out), BF16))
        out_specs.append(pl.BlockSpec((tm, n_out), row))
        if w_gate is not None:
            with_gate = True
            ng = w_gate.shape[1]
            args.append(w_gate)
            specs.append(_resident((D_MODEL, ng), lambda i: (0, 0)))
            out_shape.append(jax.ShapeDtypeStruct((m, ng), F32))
            out_specs.append(pl.BlockSpec((tm, ng), row))
    if final_gain is not None:
        args.append(final_gain)
        specs.append(pl.BlockSpec((1, D_MODEL), lambda i: (0, 0)))
    outs = pl.pallas_call(
        functools.partial(_ffn_kernel, with_mix=mix is not None, with_proj=proj is not None, with_gate=with_gate,
                          with_final=final_gain is not None),
        out_shape=out_shape,
        grid=(m // tm,),
        in_specs=specs,
        out_specs=out_specs,
        scratch_shapes=[pltpu.VMEM((tm, FFN_HIDDEN), BF16)],
        compiler_params=_params("parallel"),
        name="ffn",
    )(*args)
    return outs if proj is not None else outs[0]


def _dft_tables(t):
    k = (np.arange(t, dtype=np.int64)[:, None] * np.arange(t, dtype=np.int64)[None, :]) % t
    ang = 2.0 * np.pi * k.astype(np.float64) / t
    pos = np.concatenate([np.cos(ang), -np.sin(ang)], axis=1).astype(np.float32)
    kc = (np.arange(HEAD_DIM)[:, None] * np.arange(HEAD_DIM)[None, :]) % HEAD_DIM
    angc = 2.0 * np.pi * kc.astype(np.float64) / HEAD_DIM
    eye = np.eye(FOURIER_GROUPS)
    chan = np.concatenate([np.kron(eye, np.cos(angc)), np.kron(eye, np.sin(angc))], axis=1).astype(np.float32)
    return pos, chan


def _chan_dft_kernel(z_ref, c_ref, o_ref, *, t):
    ab = _dot(z_ref[...].astype(BF16), c_ref[...].astype(BF16))
    o_ref[0:t, :] = ab[:, :FOURIER_WIDTH].astype(BF16)
    o_ref[t:2 * t, :] = ab[:, FOURIER_WIDTH:].astype(BF16)


def _pos_dft_kernel(p_ref, ab_ref, o_ref, p_scr, *, scale):
    @pl.when(pl.program_id(1) == 0)
    def _():
        p_scr[...] = p_ref[...].astype(BF16)

    o_ref[...] = (_dot(p_scr[...], ab_ref[...]) * scale).astype(BF16)


def _fourier(z, t, bn):
    pos, chan = _dft_tables(t)
    ab = pl.pallas_call(
        functools.partial(_chan_dft_kernel, t=t),
        out_shape=jax.ShapeDtypeStruct((bn, 2 * t, FOURIER_WIDTH), BF16),
        grid=(bn,),
        in_specs=[pl.BlockSpec((None, t, FOURIER_WIDTH), lambda b: (b, 0, 0)),
                  pl.BlockSpec((FOURIER_WIDTH, 2 * FOURIER_WIDTH), lambda b: (0, 0))],
        out_specs=pl.BlockSpec((None, 2 * t, FOURIER_WIDTH), lambda b: (b, 0, 0)),
        compiler_params=_params("parallel"),
        name="chan_dft",
    )(z, jnp.asarray(chan))
    tq = min(t, 512)
    return pl.pallas_call(
        functools.partial(_pos_dft_kernel, scale=1.0 / math.sqrt(t * HEAD_DIM)),
        out_shape=jax.ShapeDtypeStruct((bn, t, FOURIER_WIDTH), BF16),
        grid=(t // tq, bn),
        in_specs=[pl.BlockSpec((tq, 2 * t), lambda i, b: (i, 0)),
                  pl.BlockSpec((None, 2 * t, FOURIER_WIDTH), lambda i, b: (b, 0, 0))],
        out_specs=pl.BlockSpec((None, tq, FOURIER_WIDTH), lambda i, b: (b, i, 0)),
        scratch_shapes=[pltpu.VMEM((tq, 2 * t), BF16)],
        compiler_params=_params("parallel", "arbitrary"),
        name="pos_dft",
    )(jnp.asarray(pos), ab)


def _gla_segment(q_ref, k_ref, v_ref, r_ref, g_ref, y_ref, gw_ref, gb_ref, gn_ref,
                 b_scr, o_scr, s_scr, n_rows):
    c_len = GLA_CHUNK
    n = n_rows // c_len
    pair_k = 2 * GLA_DK
    pair_v = 2 * GLA_DV

    for d in range(2):
        zg = g_ref[:, d * GLA_GATE_RANK:(d + 1) * GLA_GATE_RANK]
        logit = _dot(zg.astype(BF16), gw_ref[d].astype(BF16)) + gb_ref[d]
        b_scr[d, 0:n_rows, :] = (jnp.minimum(logit, 0.0) - jnp.log(1.0 + jnp.exp(-jnp.abs(logit)))) * (1.0 / GLA_TAU)

    lane = lax.broadcasted_iota(jnp.int32, (c_len, pair_k), 1)
    col = lax.broadcasted_iota(jnp.int32, (c_len, pair_v), 1)
    ti = lax.broadcasted_iota(jnp.int32, (c_len, c_len), 0)
    tj = lax.broadcasted_iota(jnp.int32, (c_len, c_len), 1)
    blk = (lax.broadcasted_iota(jnp.int32, (pair_v, pair_k), 0) // GLA_DV
           == lax.broadcasted_iota(jnp.int32, (pair_v, pair_k), 1) // GLA_DK)

    group = math.gcd(GLA_GROUP, n)
    keep2 = [jnp.concatenate([m, m], axis=0) for m in (tj <= ti, tj >= ti)]
    ti2 = lax.broadcasted_iota(jnp.int32, (c_len, 2 * c_len), 0)
    tj2 = lax.broadcasted_iota(jnp.int32, (c_len, 2 * c_len), 1) & (c_len - 1)
    tri2 = [jnp.where(m, 1.0, 0.0).astype(BF16) for m in (tj2 <= ti2, tj2 >= ti2)]

    def body(i, carry):
        units = []
        for d in range(2):
            for g in range(group):
                c = i * group + g if d == 0 else n - 1 - (i * group + g)
                rows = pl.ds(pl.multiple_of(c * c_len, c_len), c_len)
                lg = b_scr[d, rows, :]
                hi = lg.astype(BF16)
                rest = (lg - hi.astype(F32)).astype(BF16)
                units.append(dict(d=d, rows=rows, b=_dot(tri2[d], jnp.concatenate([hi, rest], axis=0))))
        for u in units:
            d, rows, b = u["d"], u["rows"], u["b"]
            bl = b[c_len - 1:c_len, :] if d == 0 else b[0:1, :]
            k = k_ref[rows, :].astype(F32)
            v16 = v_ref[rows, :].astype(BF16)
            qd = q_ref[rows, :].astype(F32) * (GLA_DK ** -0.5) * jnp.exp(b)
            kd = (k * jnp.exp(-b)).astype(BF16)
            kdec = (k * jnp.exp(bl - b)).astype(BF16)
            q2 = jnp.concatenate([jnp.where((lane // GLA_DK) == hh, qd, 0.0) for hh in range(2)], axis=0)
            att = _dot_nt(q2.astype(BF16), kd)
            upd = _dot_tn(v16, kdec)
            u.update(qd=qd.astype(BF16), v16=v16, att=att, upd=upd, dec=jnp.exp(bl))
        for d in range(2):
            s = s_scr[d]
            for u in units:
                if u["d"] == d:
                    u["o"] = _dot_nt(u["qd"], s.astype(BF16))
                    s = jnp.where(blk, s * u["dec"] + u["upd"], 0.0)
            s_scr[d] = s
        for u in units:
            att = jnp.where(keep2[u["d"]], u["att"], 0.0).astype(BF16)
            pv = _dot(att, u["v16"])
            intra = jnp.where((col // GLA_DV) == 0, pv[0:c_len, :], pv[c_len:, :])
            o_scr[u["d"], u["rows"], :] = u["o"] + intra
        return carry

    lax.fori_loop(0, n // group, body, 0)

    gn = gn_ref[...]
    for hh in range(2):
        sl = slice(hh * GLA_DV, (hh + 1) * GLA_DV)
        o = o_scr[0, 0:n_rows, sl] + o_scr[1, 0:n_rows, sl]
        rr = r_ref[:, sl].astype(F32)
        y_ref[:, sl] = (_rms(o) * gn * (rr * _sigmoid(rr))).astype(y_ref.dtype)


def _gla_kernel(qc_ref, kc_ref, vc_ref, rc_ref, gc_ref, ql_ref, kl_ref, vl_ref, rl_ref, gl_ref,
                gw_ref, gb_ref, gn_ref, yc_ref, yl_ref, b_scr, o_scr, s_scr, *, t_ctx, t_lat):
    s_scr[...] = jnp.zeros_like(s_scr)
    _gla_segment(qc_ref, kc_ref, vc_ref, rc_ref, gc_ref, yc_ref, gw_ref, gb_ref, gn_ref,
                 b_scr, o_scr, s_scr, t_ctx)
    _gla_segment(ql_ref, kl_ref, vl_ref, rl_ref, gl_ref, yl_ref, gw_ref, gb_ref, gn_ref,
                 b_scr, o_scr, s_scr, t_lat)


def _gla(zc, gc, zl, gl, gate_w, gate_b, gla_g, bn, t_ctx, t_lat):
    pk, pv = 2 * GLA_DK, 2 * GLA_DV
    q0 = FOURIER_WIDTH // pk
    k0 = (FOURIER_WIDTH + GLA_HEADS * GLA_DK) // pk
    v0 = (FOURIER_WIDTH + 2 * GLA_HEADS * GLA_DK) // pv
    r0 = (FOURIER_WIDTH + 2 * GLA_HEADS * GLA_DK + GLA_HEADS * GLA_DV) // pv

    def seg_specs(t):
        return [pl.BlockSpec((None, t, pk), lambda b, p: (b, 0, q0 + p)),
                pl.BlockSpec((None, t, pk), lambda b, p: (b, 0, k0 + p)),
                pl.BlockSpec((None, t, pv), lambda b, p: (b, 0, v0 + p)),
                pl.BlockSpec((None, t, pv), lambda b, p: (b, 0, r0 + p)),
                pl.BlockSpec((None, t, 2 * GLA_GATE_RANK), lambda b, p: (b, 0, 0))]

    wdt = GLA_HEADS * GLA_DV
    return pl.pallas_call(
        functools.partial(_gla_kernel, t_ctx=t_ctx, t_lat=t_lat),
        out_shape=[jax.ShapeDtypeStruct((bn, t_ctx, wdt), BF16), jax.ShapeDtypeStruct((bn, t_lat, wdt), BF16)],
        grid=(bn, GLA_HEADS // 2),
        in_specs=seg_specs(t_ctx) + seg_specs(t_lat) + [
            pl.BlockSpec((2, GLA_GATE_RANK, pk), lambda b, p: (0, 0, p)),
            pl.BlockSpec((2, 1, pk), lambda b, p: (0, 0, p)),
            pl.BlockSpec((1, GLA_DV), lambda b, p: (0, 0))],
        out_specs=[pl.BlockSpec((None, t_ctx, pv), lambda b, p: (b, 0, p)),
                   pl.BlockSpec((None, t_lat, pv), lambda b, p: (b, 0, p))],
        scratch_shapes=[pltpu.VMEM((2, t_lat, pk), F32), pltpu.VMEM((2, t_lat, pv), F32),
                        pltpu.VMEM((2, pv, pk), F32)],
        compiler_params=_params("parallel", "parallel"),
        name="gla",
    )(zc, zc, zc, zc, gc, zl, zl, zl, zl, gl, gate_w, gate_b.reshape(2, 1, GLA_HEADS * GLA_DK),
      gla_g.reshape(1, GLA_DV))


def _conv_kernel(zb_ref, zc_ref, zx_ref, w_ref, b_ref, o_ref, *, t):
    u = zc_ref[...].astype(F32) * zx_ref[...].astype(F32)
    row = lax.broadcasted_iota(jnp.int32, u.shape, 0)
    prev = jnp.where(row >= 1, pltpu.roll(u, 1, 0), 0.0)
    nxt = jnp.where(row < t - 1, pltpu.roll(u, t - 1, 0), 0.0)
    y = prev * w_ref[0:1, :] + u * w_ref[1:2, :] + nxt * w_ref[2:3, :] + b_ref[...]
    o_ref[...] = (zb_ref[...].astype(F32) * y).astype(o_ref.dtype)


def _short_conv(z, conv_w, conv_b, bn, t):
    cw = CONV_WIDTH
    return pl.pallas_call(
        functools.partial(_conv_kernel, t=t),
        out_shape=jax.ShapeDtypeStruct((bn, t, cw), BF16),
        grid=(bn,),
        in_specs=[pl.BlockSpec((None, t, cw), lambda b: (b, 0, 0)),
                  pl.BlockSpec((None, t, cw), lambda b: (b, 0, 1)),
                  pl.BlockSpec((None, t, cw), lambda b: (b, 0, 2)),
                  pl.BlockSpec((3, cw), lambda b: (0, 0)),
                  pl.BlockSpec((1, cw), lambda b: (0, 0))],
        out_specs=pl.BlockSpec((None, t, cw), lambda b: (b, 0, 0)),
        compiler_params=_params("parallel"),
        name="short_conv",
    )(z, z, z, conv_w, conv_b.reshape(1, cw))


def _rope_tables(t):
    rows = t // GRID_W
    row = jnp.repeat(jnp.arange(rows), GRID_W).astype(F32)
    col = jnp.tile(jnp.arange(GRID_W), rows).astype(F32)
    n = ROPE_AXIS_DIM // 2
    inv = ROPE_THETA ** (-jnp.arange(n, dtype=F32) / n)
    ar, ac = row[:, None] * inv, col[:, None] * inv
    ang = jnp.concatenate([ar, ar, ac, ac], axis=-1)
    sign = jnp.tile(jnp.concatenate([-jnp.ones((n,), F32), jnp.ones((n,), F32)]), 2)
    cos = jnp.cos(ang)
    sin = jnp.sin(ang) * sign
    return jnp.tile(cos, (1, 2)), jnp.tile(sin, (1, 2))


def _rope(x, cos, sin):
    lane = lax.broadcasted_iota(jnp.int32, x.shape, 1)
    n = ROPE_AXIS_DIM // 2
    w = x.shape[1]
    partner = jnp.where((lane & (2 * n - 1)) < n, pltpu.roll(x, w - n, 1), pltpu.roll(x, n, 1))
    return x * cos + partner * sin


ATTN_ONES_ROWS = 16


def _attn_kernel(lam_ref, q_ref, kl_ref, vl_ref, kc_ref, vc_ref, cos_ref, sin_ref, dn_ref,
                 o_ref, k_scr, vt_scr, s_scr, m_scr, *, t_lat, lam_init, tq):
    hw = 2 * HEAD_DIM
    t_all = k_scr.shape[0]

    k_scr[0:t_lat, :] = _rope(kl_ref[...].astype(F32), cos_ref[...], sin_ref[...]).astype(BF16)
    k_scr[t_lat:, :] = kc_ref[...].astype(BF16)
    vt_scr[0:hw, 0:t_lat] = vl_ref[...].astype(F32).T.astype(BF16)
    vt_scr[0:hw, t_lat:] = vc_ref[...].astype(F32).T.astype(BF16)
    vt_scr[hw:, :] = jnp.ones((ATTN_ONES_ROWS, t_all), BF16)

    lv = lam_ref[...]
    lam = (jnp.exp(jnp.sum(lv[0:1] * lv[1:2], axis=-1, keepdims=True))
           - jnp.exp(jnp.sum(lv[2:3] * lv[3:4], axis=-1, keepdims=True)) + lam_init)
    lane = lax.broadcasted_iota(jnp.int32, (tq, hw), 1)

    n_blocks = t_lat // tq

    def scores(i):
        rows = pl.ds(pl.multiple_of(i * tq, tq), tq)
        q = (_rope(q_ref[rows, :].astype(F32), cos_ref[rows, :], sin_ref[rows, :])
             * (HEAD_DIM ** -0.5 * math.log2(math.e)))
        slot = i % 2
        for half in range(2):
            qb = jnp.where((lane // HEAD_DIM) == half, q, 0.0).astype(BF16)
            s = _dot_nt(k_scr[...], qb)
            s_scr[slot, half] = s
            m_scr[slot, half] = jnp.max(s, axis=0, keepdims=True)

    def outputs(i):
        rows = pl.ds(pl.multiple_of(i * tq, tq), tq)
        slot = i % 2
        outs = []
        for half in range(2):
            p = jnp.exp2(s_scr[slot, half] - m_scr[slot, half]).astype(BF16)
            acc = _dot(vt_scr[...], p)
            outs.append(acc[0:hw, :] / acc[hw:hw + 1, :])
        o = (outs[0] - lam * outs[1]).T
        o_ref[rows, :] = (_rms(o) * dn_ref[...] * (1.0 - lam_init)).astype(o_ref.dtype)

    scores(0)

    def body(i, carry):
        scores(i + 1)
        outputs(i)
        return carry

    lax.fori_loop(0, n_blocks - 1, body, 0)
    outputs(n_blocks - 1)


def _diff_attn(zl, zc, lam_vecs, dnorm, lam_init, bn, t_lat, t_ctx, c_k0):
    hw = 2 * HEAD_DIM
    q0 = 3 * CONV_WIDTH // hw
    k0 = q0 + DIFF_HEADS
    v0 = k0 + DIFF_HEADS
    tq = 256
    t_all = t_lat + t_ctx
    cos, sin = _rope_tables(t_lat)
    return pl.pallas_call(
        functools.partial(_attn_kernel, t_lat=t_lat, lam_init=lam_init, tq=tq),
        out_shape=jax.ShapeDtypeStruct((bn, t_lat, DIFF_HEADS * DIFF_DV), BF16),
        grid=(bn, DIFF_HEADS),
        in_specs=[
            pl.BlockSpec((4, HEAD_DIM), lambda b, h: (0, 0)),
            pl.BlockSpec((None, t_lat, hw), lambda b, h: (b, 0, q0 + h)),
            pl.BlockSpec((None, t_lat, hw), lambda b, h: (b, 0, k0 + h)),
            pl.BlockSpec((None, t_lat, hw), lambda b, h: (b, 0, v0 + h)),
            pl.BlockSpec((None, t_ctx, hw), lambda b, h: (b, 0, c_k0 + h)),
            pl.BlockSpec((None, t_ctx, hw), lambda b, h: (b, 0, c_k0 + DIFF_HEADS + h)),
            pl.BlockSpec((t_lat, hw), lambda b, h: (0, 0)),
            pl.BlockSpec((t_lat, hw), lambda b, h: (0, 0)),
            pl.BlockSpec((1, DIFF_DV), lambda b, h: (0, 0)),
        ],
        out_specs=pl.BlockSpec((None, t_lat, hw), lambda b, h: (b, 0, h)),
        scratch_shapes=[pltpu.VMEM((t_all, hw), BF16), pltpu.VMEM((hw + ATTN_ONES_ROWS, t_all), BF16),
                        pltpu.VMEM((2, 2, t_all, tq), F32), pltpu.VMEM((2, 2, 1, tq), F32)],
        compiler_params=_params("parallel", "parallel"),
        name="diff_attn",
    )(lam_vecs, zl, zl, zl, zc, zc, cos, sin, dnorm.reshape(1, DIFF_DV))


def kernel(x, c, ctx, c_ctx, ada_w, ada_b, norm_ffn1, norm_mix, norm_ffn2, ffn1_w_in, ffn1_w_out, ffn2_w_in,
           ffn2_w_out, mix_w_out, even_w_in, gla_gate_w, gla_gate_b, gla_norm, odd_w_in, conv_w, conv_b,
           lambda_q1, lambda_k1, lambda_q2, lambda_k2, diff_norm, final_norm):
    assert DEPTH == 2
    bn, t_lat, d = x.shape
    t_ctx = ctx.shape[1]
    assert bn < MOD_ROWS
    ctx_row = bn

    cond = jnp.concatenate([c, c_ctx[None, :], jnp.zeros((MOD_ROWS - bn - 1, d), F32)], axis=0)
    mods = _modulation(cond, ada_w, ada_b)

    tm = 512
    lat_row = lambda i: i // (t_lat // tm)
    ctx_rows = lambda i: ctx_row
    g1 = norm_ffn1.reshape(DEPTH, 1, d)
    gm = norm_mix.reshape(DEPTH, 1, d)
    g2 = norm_ffn2.reshape(DEPTH, 1, d)
    w1i, w1o = ffn1_w_in.astype(BF16), ffn1_w_out.astype(BF16)
    w2i, w2o = ffn2_w_in.astype(BF16), ffn2_w_out.astype(BF16)
    wmix_a = mix_w_out[:, :FOURIER_WIDTH, :].astype(BF16)
    wmix_b = mix_w_out[:, FOURIER_WIDTH:, :].astype(BF16)

    h = x.reshape(bn * t_lat, d)
    hc = ctx.reshape(bn * t_ctx, d)

    w_even = even_w_in[0].astype(BF16)
    proj0 = (gm, w_even[:, :EVEN_MAIN], w_even[:, EVEN_MAIN:])
    h, zl, gl = _ffn(h, mods, 0, 0, lat_row, g1, w1i, w1o, tm=tm, proj=proj0)
    hc, zc, gc = _ffn(hc, mods, 0, 0, ctx_rows, g1, w1i, w1o, tm=tm, proj=proj0)
    zl = zl.reshape(bn, t_lat, EVEN_MAIN)
    zc = zc.reshape(bn, t_ctx, EVEN_MAIN)
    gl = gl.reshape(bn, t_lat, 2 * GLA_GATE_RANK)
    gc = gc.reshape(bn, t_ctx, 2 * GLA_GATE_RANK)
    yf_l = _fourier(zl, t_lat, bn)
    yf_c = _fourier(zc, t_ctx, bn)
    yg_c, yg_l = _gla(zc, gc, zl, gl, gla_gate_w[0], gla_gate_b[0], gla_norm[0], bn, t_ctx, t_lat)
    mix_l = (yf_l.reshape(bn * t_lat, -1), yg_l.reshape(bn * t_lat, -1), wmix_a, wmix_b)
    mix_c = (yf_c.reshape(bn * t_ctx, -1), yg_c.reshape(bn * t_ctx, -1), wmix_a, wmix_b)
    h = _ffn(h, mods, 0, 6, lat_row, g2, w2i, w2o, tm=tm, mix=mix_l)
    hc = _ffn(hc, mods, 0, 6, ctx_rows, g2, w2i, w2o, tm=tm, mix=mix_c)

    w_odd = odd_w_in[0].astype(BF16)
    kv_w = 2 * DIFF_HEADS * DIFF_DV
    h, zl = _ffn(h, mods, 1, 0, lat_row, g1, w1i, w1o, tm=tm, proj=(gm, w_odd, None))
    hc, zc = _ffn(hc, mods, 1, 0, ctx_rows, g1, w1i, w1o, tm=tm, proj=(gm, w_odd[:, ODD_IN - kv_w:], None))
    zl = zl.reshape(bn, t_lat, ODD_IN)
    zc = zc.reshape(bn, t_ctx, kv_w)
    lam_init = 0.8 - 0.6 * math.exp(-0.3 * 1)
    lam_vecs = jnp.stack([lambda_q1[0], lambda_k1[0], lambda_q2[0], lambda_k2[0]]).astype(F32)
    y_conv = _short_conv(zl, conv_w[0], conv_b[0], bn, t_lat)
    y_att = _diff_attn(zl, zc, lam_vecs, diff_norm[0], lam_init, bn, t_lat, t_ctx, 0)
    mix_l = (y_conv.reshape(bn * t_lat, -1), y_att.reshape(bn * t_lat, -1), wmix_a, wmix_b)
    h = _ffn(h, mods, 1, 6, lat_row, g2, w2i, w2o, tm=tm, mix=mix_l, final_gain=final_norm.reshape(1, d))
    return h.reshape(bn, t_lat, d)
```

```python
import functools
import math

import numpy as np
import jax
import jax.numpy as jnp
from jax import lax
from jax.experimental import pallas as pl
from jax.experimental.pallas import tpu as pltpu

D_MODEL = 1024
DEPTH = 2
GRID_W = 64
HEAD_DIM = 64
N_MOD = 9
FFN_HIDDEN = 2816
NORM_EPS = 1e-6
FOURIER_GROUPS = 4
FOURIER_WIDTH = FOURIER_GROUPS * HEAD_DIM
GLA_HEADS = 6
GLA_DK = 64
GLA_DV = 128
GLA_GATE_RANK = 16
GLA_TAU = 16.0
GLA_CHUNK = 64
GLA_GROUP = 8
FFN_TILE = 256
PROJ_TILE = 512
CONV_WIDTH = 4 * HEAD_DIM
DIFF_HEADS = 6
DIFF_DV = 2 * HEAD_DIM
ROPE_THETA = 10000.0
ROPE_AXIS_DIM = HEAD_DIM // 2

EVEN_MAIN = FOURIER_WIDTH + 2 * GLA_HEADS * GLA_DK + 2 * GLA_HEADS * GLA_DV
ODD_IN = 3 * CONV_WIDTH + 3 * DIFF_HEADS * DIFF_DV

MOD_ROWS = 16
VMEM_LIMIT = 48 * 1024 * 1024

F32 = jnp.float32
BF16 = jnp.bfloat16


def _params(*sem):
    return pltpu.CompilerParams(dimension_semantics=sem, vmem_limit_bytes=VMEM_LIMIT)


def _sigmoid(x):
    return 1.0 / (1.0 + jnp.exp(-x))


def _dot(a, b):
    return jnp.dot(a, b, preferred_element_type=F32)


def _dot_nt(a, b):
    return lax.dot_general(a, b, (((1,), (1,)), ((), ())), preferred_element_type=F32)


def _dot_tn(a, b):
    return lax.dot_general(a, b, (((0,), (0,)), ((), ())), preferred_element_type=F32)


def _rms(x):
    return x * lax.rsqrt(jnp.mean(x * x, axis=-1, keepdims=True) + NORM_EPS)


def _mod_kernel(cond_ref, w_ref, b_ref, o_ref):
    c = cond_ref[...]
    s = c * _sigmoid(c)
    w = w_ref[...]
    w_hi = w.astype(BF16)
    w_lo = (w - w_hi.astype(F32)).astype(BF16)
    s_hi = s.astype(BF16)
    s_lo = (s - s_hi.astype(F32)).astype(BF16)
    o_ref[...] = _dot(s_hi, w_hi) + _dot(s_lo, w_hi) + _dot(s_hi, w_lo) + b_ref[...]


def _modulation(cond, ada_w, ada_b):
    n = N_MOD * D_MODEL
    tn = n // 8
    out = pl.pallas_call(
        _mod_kernel,
        out_shape=jax.ShapeDtypeStruct((DEPTH, MOD_ROWS, n), F32),
        grid=(DEPTH, n // tn),
        in_specs=[
            pl.BlockSpec((MOD_ROWS, D_MODEL), lambda l, j: (0, 0)),
            pl.BlockSpec((None, D_MODEL, tn), lambda l, j: (l, 0, j)),
            pl.BlockSpec((None, 1, tn), lambda l, j: (l, 0, j)),
        ],
        out_specs=pl.BlockSpec((None, MOD_ROWS, tn), lambda l, j: (l, 0, j)),
        compiler_params=_params("parallel", "parallel"),
        name="modulation",
    )(cond, ada_w, ada_b.reshape(DEPTH, 1, n))
    return out.reshape(DEPTH, MOD_ROWS, N_MOD, 1, D_MODEL)


def _mod_spec(layer, k, row_of_block):
    return pl.BlockSpec((None, None, None, 1, D_MODEL),
                        lambda i, *_: (layer, row_of_block(i), k, 0, 0))


def _gain_spec(layer):
    return pl.BlockSpec((None, 1, D_MODEL), lambda i, *_: (layer, 0, 0))


def _resident(block_shape, index_map):
    return pl.BlockSpec(block_shape, index_map, pipeline_mode=pl.Buffered(1))


def _ffn_kernel(*refs, with_mix, with_proj, with_gate, with_final):
    it = iter(refs)
    h_ref = next(it)
    if with_mix:
        ma_ref, mb_ref, wa_ref, wb_ref, gm_ref = (next(it) for _ in range(5))
    sh_ref, sc_ref, g_ref, gain_ref, wi_ref, wo_ref = (next(it) for _ in range(6))
    if with_proj:
        psh_ref, psc_ref, pgain_ref, wp_ref = (next(it) for _ in range(4))
    if with_gate:
        wgt_ref = next(it)
    if with_final:
        fn_ref = next(it)
    o_ref = next(it)
    if with_proj:
        z_ref = next(it)
    if with_gate:
        zg_ref = next(it)
    a_scr = next(it)

    x = h_ref[...]
    if with_mix:
        x = x + gm_ref[...] * (_dot(ma_ref[...], wa_ref[...]) + _dot(mb_ref[...], wb_ref[...]))
    xn = (_rms(x) * gain_ref[...] * (1.0 + sc_ref[...]) + sh_ref[...]).astype(BF16)
    for j in range(FFN_HIDDEN // FFN_TILE):
        lo = j * FFN_TILE
        g = _dot(xn, wi_ref[:, lo:lo + FFN_TILE])
        u = _dot(xn, wi_ref[:, FFN_HIDDEN + lo:FFN_HIDDEN + lo + FFN_TILE])
        a_scr[:, lo:lo + FFN_TILE] = (g * _sigmoid(g) * u).astype(BF16)
    out = x + (0.5 * g_ref[...]) * _dot(a_scr[...], wo_ref[...])
    if with_final:
        out = _rms(out) * fn_ref[...]
    o_ref[...] = out
    if with_proj:
        xm = (_rms(out) * pgain_ref[...] * (1.0 + psc_ref[...]) + psh_ref[...]).astype(BF16)
        n_out = z_ref.shape[1]
        for lo in range(0, n_out, PROJ_TILE):
            z_ref[:, lo:lo + PROJ_TILE] = _dot(xm, wp_ref[:, lo:lo + PROJ_TILE]).astype(z_ref.dtype)
        if with_gate:
            zg_ref[...] = _dot(xm, wgt_ref[...])


def _ffn(h, mods, layer, mod_base, row_of_block, gain, w_in, w_out, *, tm, mix=None, proj=None, final_gain=None):
    m = h.shape[0]
    row = lambda i: (i, 0)
    args = [h]
    specs = [pl.BlockSpec((tm, D_MODEL), row)]
    if mix is not None:
        ma, mb, wa, wb = mix
        ka, kb = ma.shape[1], mb.shape[1]
        args += [ma, mb, wa, wb, mods]
        specs += [
            pl.BlockSpec((tm, ka), row),
            pl.BlockSpec((tm, kb), row),
            _resident((None, ka, D_MODEL), lambda i: (layer, 0, 0)),
            _resident((None, kb, D_MODEL), lambda i: (layer, 0, 0)),
            _mod_spec(layer, 5, row_of_block),
        ]
    args += [mods, mods, mods, gain, w_in, w_out]
    specs += [
        _mod_spec(layer, mod_base, row_of_block),
        _mod_spec(layer, mod_base + 1, row_of_block),
        _mod_spec(layer, mod_base + 2, row_of_block),
        _gain_spec(layer),
        _resident((None, D_MODEL, 2 * FFN_HIDDEN), lambda i: (layer, 0, 0)),
        _resident((None, FFN_HIDDEN, D_MODEL), lambda i: (layer, 0, 0)),
    ]
    out_shape = [jax.ShapeDtypeStruct((m, D_MODEL), F32)]
    out_specs = [pl.BlockSpec((tm, D_MODEL), row)]
    with_gate = False
    if proj is not None:
        pgain, wp, w_gate = proj
        n_out = wp.shape[1]
        assert n_out % PROJ_TILE == 0
        args += [mods, mods, pgain, wp]
        specs += [_mod_spec(layer, 3, row_of_block), _mod_spec(layer, 4, row_of_block), _gain_spec(layer),
                  _resident((D_MODEL, n_out), lambda i: (0, 0))]
        out_shape.append(jax.ShapeDtypeStruct((m, n_out), BF16))
        out_specs.append(pl.BlockSpec((tm, n_out), row))
        if w_gate is not None:
            with_gate = True
            ng = w_gate.shape[1]
            args.append(w_gate)
            specs.append(_resident((D_MODEL, ng), lambda i: (0, 0)))
            out_shape.append(jax.ShapeDtypeStruct((m, ng), F32))
            out_specs.append(pl.BlockSpec((tm, ng), row))
    if final_gain is not None:
        args.append(final_gain)
        specs.append(pl.BlockSpec((1, D_MODEL), lambda i: (0, 0)))
    outs = pl.pallas_call(
        functools.partial(_ffn_kernel, with_mix=mix is not None, with_proj=proj is not None, with_gate=with_gate,
                          with_final=final_gain is not None),
        out_shape=out_shape,
        grid=(m // tm,),
        in_specs=specs,
        out_specs=out_specs,
        scratch_shapes=[pltpu.VMEM((tm, FFN_HIDDEN), BF16)],
        compiler_params=_params("parallel"),
        name="ffn",
    )(*args)
    return outs if proj is not None else outs[0]


def _dft_tables(t):
    k = (np.arange(t, dtype=np.int64)[:, None] * np.arange(t, dtype=np.int64)[None, :]) % t
    ang = 2.0 * np.pi * k.astype(np.float64) / t
    pos = np.concatenate([np.cos(ang), -np.sin(ang)], axis=1).astype(np.float32)
    kc = (np.arange(HEAD_DIM)[:, None] * np.arange(HEAD_DIM)[None, :]) % HEAD_DIM
    angc = 2.0 * np.pi * kc.astype(np.float64) / HEAD_DIM
    eye = np.eye(FOURIER_GROUPS)
    chan = np.concatenate([np.kron(eye, np.cos(angc)), np.kron(eye, np.sin(angc))], axis=1).astype(np.float32)
    return pos, chan


def _chan_dft_kernel(z_ref, c_ref, o_ref, *, t):
    ab = _dot(z_ref[...].astype(BF16), c_ref[...].astype(BF16))
    o_ref[0:t, :] = ab[:, :FOURIER_WIDTH].astype(BF16)
    o_ref[t:2 * t, :] = ab[:, FOURIER_WIDTH:].astype(BF16)


def _pos_dft_kernel(p_ref, ab_ref, o_ref, p_scr, *, scale):
    @pl.when(pl.program_id(1) == 0)
    def _():
        p_scr[...] = p_ref[...].astype(BF16)

    o_ref[...] = (_dot(p_scr[...], ab_ref[...]) * scale).astype(BF16)


def _fourier(z, t, bn):
    pos, chan = _dft_tables(t)
    ab = pl.pallas_call(
        functools.partial(_chan_dft_kernel, t=t),
        out_shape=jax.ShapeDtypeStruct((bn, 2 * t, FOURIER_WIDTH), BF16),
        grid=(bn,),
        in_specs=[pl.BlockSpec((None, t, FOURIER_WIDTH), lambda b: (b, 0, 0)),
                  pl.BlockSpec((FOURIER_WIDTH, 2 * FOURIER_WIDTH), lambda b: (0, 0))],
        out_specs=pl.BlockSpec((None, 2 * t, FOURIER_WIDTH), lambda b: (b, 0, 0)),
        compiler_params=_params("parallel"),
        name="chan_dft",
    )(z, jnp.asarray(chan))
    tq = min(t, 512)
    return pl.pallas_call(
        functools.partial(_pos_dft_kernel, scale=1.0 / math.sqrt(t * HEAD_DIM)),
        out_shape=jax.ShapeDtypeStruct((bn, t, FOURIER_WIDTH), BF16),
        grid=(t // tq, bn),
        in_specs=[pl.BlockSpec((tq, 2 * t), lambda i, b: (i, 0)),
                  pl.BlockSpec((None, 2 * t, FOURIER_WIDTH), lambda i, b: (b, 0, 0))],
        out_specs=pl.BlockSpec((None, tq, FOURIER_WIDTH), lambda i, b: (b, i, 0)),
        scratch_shapes=[pltpu.VMEM((tq, 2 * t), BF16)],
        compiler_params=_params("parallel", "arbitrary"),
        name="pos_dft",
    )(jnp.asarray(pos), ab)


def _gla_segment(q_ref, k_ref, v_ref, r_ref, g_ref, y_ref, gw_ref, gb_ref, gn_ref,
                 b_scr, o_scr, s_scr, n_rows):
    c_len = GLA_CHUNK
    n = n_rows // c_len
    pair_k = 2 * GLA_DK
    pair_v = 2 * GLA_DV

    for d in range(2):
        zg = g_ref[:, d * GLA_GATE_RANK:(d + 1) * GLA_GATE_RANK]
        logit = _dot(zg.astype(BF16), gw_ref[d].astype(BF16)) + gb_ref[d]
        b_scr[d, 0:n_rows, :] = (jnp.minimum(logit, 0.0) - jnp.log(1.0 + jnp.exp(-jnp.abs(logit)))) * (1.0 / GLA_TAU)

    lane = lax.broadcasted_iota(jnp.int32, (c_len, pair_k), 1)
    col = lax.broadcasted_iota(jnp.int32, (c_len, pair_v), 1)
    ti = lax.broadcasted_iota(jnp.int32, (c_len, c_len), 0)
    tj = lax.broadcasted_iota(jnp.int32, (c_len, c_len), 1)
    blk = (lax.broadcasted_iota(jnp.int32, (pair_v, pair_k), 0) // GLA_DV
           == lax.broadcasted_iota(jnp.int32, (pair_v, pair_k), 1) // GLA_DK)

    group = math.gcd(GLA_GROUP, n)
    keep2 = [jnp.concatenate([m, m], axis=0) for m in (tj <= ti, tj >= ti)]
    ti2 = lax.broadcasted_iota(jnp.int32, (c_len, 2 * c_len), 0)
    tj2 = lax.broadcasted_iota(jnp.int32, (c_len, 2 * c_len), 1) & (c_len - 1)
    tri2 = [jnp.where(m, 1.0, 0.0).astype(BF16) for m in (tj2 <= ti2, tj2 >= ti2)]

    def body(i, carry):
        units = []
        for d in range(2):
            for g in range(group):
                c = i * group + g if d == 0 else n - 1 - (i * group + g)
                rows = pl.ds(pl.multiple_of(c * c_len, c_len), c_len)
                lg = b_scr[d, rows, :]
                hi = lg.astype(BF16)
                rest = (lg - hi.astype(F32)).astype(BF16)
                units.append(dict(d=d, rows=rows, b=_dot(tri2[d], jnp.concatenate([hi, rest], axis=0))))
        for u in units:
            d, rows, b = u["d"], u["rows"], u["b"]
            bl = b[c_len - 1:c_len, :] if d == 0 else b[0:1, :]
            k = k_ref[rows, :].astype(F32)
            v16 = v_ref[rows, :].astype(BF16)
            qd = q_ref[rows, :].astype(F32) * (GLA_DK ** -0.5) * jnp.exp(b)
            kd = (k * jnp.exp(-b)).astype(BF16)
            kdec = (k * jnp.exp(bl - b)).astype(BF16)
            q2 = jnp.concatenate([jnp.where((lane // GLA_DK) == hh, qd, 0.0) for hh in range(2)], axis=0)
            att = _dot_nt(q2.astype(BF16), kd)
            upd = _dot_tn(v16, kdec)
            u.update(qd=qd.astype(BF16), v16=v16, att=att, upd=upd, dec=jnp.exp(bl))
        for d in range(2):
            s = s_scr[d]
            for u in units:
                if u["d"] == d:
                    u["o"] = _dot_nt(u["qd"], s.astype(BF16))
                    s = jnp.where(blk, s * u["dec"] + u["upd"], 0.0)
            s_scr[d] = s
        for u in units:
            att = jnp.where(keep2[u["d"]], u["att"], 0.0).astype(BF16)
            pv = _dot(att, u["v16"])
            intra = jnp.where((col // GLA_DV) == 0, pv[0:c_len, :], pv[c_len:, :])
            o_scr[u["d"], u["rows"], :] = u["o"] + intra
        return carry

    lax.fori_loop(0, n // group, body, 0)

    gn = gn_ref[...]
    for hh in range(2):
        sl = slice(hh * GLA_DV, (hh + 1) * GLA_DV)
        o = o_scr[0, 0:n_rows, sl] + o_scr[1, 0:n_rows, sl]
        rr = r_ref[:, sl].astype(F32)
        y_ref[:, sl] = (_rms(o) * gn * (rr * _sigmoid(rr))).astype(y_ref.dtype)


def _gla_kernel(qc_ref, kc_ref, vc_ref, rc_ref, gc_ref, ql_ref, kl_ref, vl_ref, rl_ref, gl_ref,
                gw_ref, gb_ref, gn_ref, yc_ref, yl_ref, b_scr, o_scr, s_scr, *, t_ctx, t_lat):
    s_scr[...] = jnp.zeros_like(s_scr)
    _gla_segment(qc_ref, kc_ref, vc_ref, rc_ref, gc_ref, yc_ref, gw_ref, gb_ref, gn_ref,
                 b_scr, o_scr, s_scr, t_ctx)
    _gla_segment(ql_ref, kl_ref, vl_ref, rl_ref, gl_ref, yl_ref, gw_ref, gb_ref, gn_ref,
                 b_scr, o_scr, s_scr, t_lat)


def _gla(zc, gc, zl, gl, gate_w, gate_b, gla_g, bn, t_ctx, t_lat):
    pk, pv = 2 * GLA_DK, 2 * GLA_DV
    q0 = FOURIER_WIDTH // pk
    k0 = (FOURIER_WIDTH + GLA_HEADS * GLA_DK) // pk
    v0 = (FOURIER_WIDTH + 2 * GLA_HEADS * GLA_DK) // pv
    r0 = (FOURIER_WIDTH + 2 * GLA_HEADS * GLA_DK + GLA_HEADS * GLA_DV) // pv

    def seg_specs(t):
        return [pl.BlockSpec((None, t, pk), lambda b, p: (b, 0, q0 + p)),
                pl.BlockSpec((None, t, pk), lambda b, p: (b, 0, k0 + p)),
                pl.BlockSpec((None, t, pv), lambda b, p: (b, 0, v0 + p)),
                pl.BlockSpec((None, t, pv), lambda b, p: (b, 0, r0 + p)),
                pl.BlockSpec((None, t, 2 * GLA_GATE_RANK), lambda b, p: (b, 0, 0))]

    wdt = GLA_HEADS * GLA_DV
    return pl.pallas_call(
        functools.partial(_gla_kernel, t_ctx=t_ctx, t_lat=t_lat),
        out_shape=[jax.ShapeDtypeStruct((bn, t_ctx, wdt), BF16), jax.ShapeDtypeStruct((bn, t_lat, wdt), BF16)],
        grid=(bn, GLA_HEADS // 2),
        in_specs=seg_specs(t_ctx) + seg_specs(t_lat) + [
            pl.BlockSpec((2, GLA_GATE_RANK, pk), lambda b, p: (0, 0, p)),
            pl.BlockSpec((2, 1, pk), lambda b, p: (0, 0, p)),
            pl.BlockSpec((1, GLA_DV), lambda b, p: (0, 0))],
        out_specs=[pl.BlockSpec((None, t_ctx, pv), lambda b, p: (b, 0, p)),
                   pl.BlockSpec((None, t_lat, pv), lambda b, p: (b, 0, p))],
        scratch_shapes=[pltpu.VMEM((2, t_lat, pk), F32), pltpu.VMEM((2, t_lat, pv), F32),
                        pltpu.VMEM((2, pv, pk), F32)],
        compiler_params=_params("parallel", "parallel"),
        name="gla",
    )(zc, zc, zc, zc, gc, zl, zl, zl, zl, gl, gate_w, gate_b.reshape(2, 1, GLA_HEADS * GLA_DK),
      gla_g.reshape(1, GLA_DV))


def _conv_kernel(zb_ref, zc_ref, zx_ref, w_ref, b_ref, o_ref, *, t):
    u = zc_ref[...].astype(F32) * zx_ref[...].astype(F32)
    row = lax.broadcasted_iota(jnp.int32, u.shape, 0)
    prev = jnp.where(row >= 1, pltpu.roll(u, 1, 0), 0.0)
    nxt = jnp.where(row < t - 1, pltpu.roll(u, t - 1, 0), 0.0)
    y = prev * w_ref[0:1, :] + u * w_ref[1:2, :] + nxt * w_ref[2:3, :] + b_ref[...]
    o_ref[...] = (zb_ref[...].astype(F32) * y).astype(o_ref.dtype)


def _short_conv(z, conv_w, conv_b, bn, t):
    cw = CONV_WIDTH
    return pl.pallas_call(
        functools.partial(_conv_kernel, t=t),
        out_shape=jax.ShapeDtypeStruct((bn, t, cw), BF16),
        grid=(bn,),
        in_specs=[pl.BlockSpec((None, t, cw), lambda b: (b, 0, 0)),
                  pl.BlockSpec((None, t, cw), lambda b: (b, 0, 1)),
                  pl.BlockSpec((None, t, cw), lambda b: (b, 0, 2)),
                  pl.BlockSpec((3, cw), lambda b: (0, 0)),
                  pl.BlockSpec((1, cw), lambda b: (0, 0))],
        out_specs=pl.BlockSpec((None, t, cw), lambda b: (b, 0, 0)),
        compiler_params=_params("parallel"),
        name="short_conv",
    )(z, z, z, conv_w, conv_b.reshape(1, cw))


def _rope_tables(t):
    rows = t // GRID_W
    row = jnp.repeat(jnp.arange(rows), GRID_W).astype(F32)
    col = jnp.tile(jnp.arange(GRID_W), rows).astype(F32)
    n = ROPE_AXIS_DIM // 2
    inv = ROPE_THETA ** (-jnp.arange(n, dtype=F32) / n)
    ar, ac = row[:, None] * inv, col[:, None] * inv
    ang = jnp.concatenate([ar, ar, ac, ac], axis=-1)
    sign = jnp.tile(jnp.concatenate([-jnp.ones((n,), F32), jnp.ones((n,), F32)]), 2)
    cos = jnp.cos(ang)
    sin = jnp.sin(ang) * sign
    return jnp.tile(cos, (1, 2)), jnp.tile(sin, (1, 2))


def _rope(x, cos, sin):
    lane = lax.broadcasted_iota(jnp.int32, x.shape, 1)
    n = ROPE_AXIS_DIM // 2
    w = x.shape[1]
    partner = jnp.where((lane & (2 * n - 1)) < n, pltpu.roll(x, w - n, 1), pltpu.roll(x, n, 1))
    return x * cos + partner * sin


ATTN_ONES_ROWS = 16


def _attn_kernel(lam_ref, q_ref, kl_ref, vl_ref, kc_ref, vc_ref, cos_ref, sin_ref, dn_ref,
                 o_ref, k_scr, vt_scr, s0_scr, m0_scr, s1_scr, m1_scr, *, t_lat, lam_init, tq):
    hw = 2 * HEAD_DIM
    t_all = k_scr.shape[0]

    k_scr[0:t_lat, :] = _rope(kl_ref[...].astype(F32), cos_ref[...], sin_ref[...]).astype(BF16)
    k_scr[t_lat:, :] = kc_ref[...].astype(BF16)
    vt_scr[0:hw, 0:t_lat] = vl_ref[...].astype(F32).T.astype(BF16)
    vt_scr[0:hw, t_lat:] = vc_ref[...].astype(F32).T.astype(BF16)
    vt_scr[hw:, :] = jnp.ones((ATTN_ONES_ROWS, t_all), BF16)

    lv = lam_ref[...]
    lam = (jnp.exp(jnp.sum(lv[0:1] * lv[1:2], axis=-1, keepdims=True))
           - jnp.exp(jnp.sum(lv[2:3] * lv[3:4], axis=-1, keepdims=True)) + lam_init)
    lane = lax.broadcasted_iota(jnp.int32, (tq, hw), 1)

    n_blocks = t_lat // tq
    slots = ((s0_scr, m0_scr), (s1_scr, m1_scr))

    def scores(i, slot):
        rows = pl.ds(pl.multiple_of(i * tq, tq), tq)
        q = (_rope(q_ref[rows, :].astype(F32), cos_ref[rows, :], sin_ref[rows, :])
             * (HEAD_DIM ** -0.5 * math.log2(math.e)))
        s_scr, m_scr = slots[slot]
        for half in range(2):
            qb = jnp.where((lane // HEAD_DIM) == half, q, 0.0).astype(BF16)
            s = _dot_nt(k_scr[...], qb)
            s_scr[half] = s
            m_scr[half] = jnp.max(s, axis=0, keepdims=True)

    def outputs(i, slot):
        rows = pl.ds(pl.multiple_of(i * tq, tq), tq)
        s_scr, m_scr = slots[slot]
        outs = []
        for half in range(2):
            p = jnp.exp2(s_scr[half] - m_scr[half]).astype(BF16)
            acc = _dot(vt_scr[...], p)
            outs.append(acc[0:hw, :] / acc[hw:hw + 1, :])
        o = (outs[0] - lam * outs[1]).T
        o_ref[rows, :] = (_rms(o) * dn_ref[...] * (1.0 - lam_init)).astype(o_ref.dtype)

    assert n_blocks % 2 == 0
    scores(0, 0)

    def body(j, carry):
        scores(2 * j + 1, 1)
        outputs(2 * j, 0)
        scores(2 * j + 2, 0)
        outputs(2 * j + 1, 1)
        return carry

    lax.fori_loop(0, n_blocks // 2 - 1, body, 0)
    scores(n_blocks - 1, 1)
    outputs(n_blocks - 2, 0)
    outputs(n_blocks - 1, 1)


def _diff_attn(zl, zc, lam_vecs, dnorm, lam_init, bn, t_lat, t_ctx, c_k0):
    hw = 2 * HEAD_DIM
    q0 = 3 * CONV_WIDTH // hw
    k0 = q0 + DIFF_HEADS
    v0 = k0 + DIFF_HEADS
    tq = 256
    t_all = t_lat + t_ctx
    cos, sin = _rope_tables(t_lat)
    return pl.pallas_call(
        functools.partial(_attn_kernel, t_lat=t_lat, lam_init=lam_init, tq=tq),
        out_shape=jax.ShapeDtypeStruct((bn, t_lat, DIFF_HEADS * DIFF_DV), BF16),
        grid=(bn, DIFF_HEADS),
        in_specs=[
            pl.BlockSpec((4, HEAD_DIM), lambda b, h: (0, 0)),
            pl.BlockSpec((None, t_lat, hw), lambda b, h: (b, 0, q0 + h)),
            pl.BlockSpec((None, t_lat, hw), lambda b, h: (b, 0, k0 + h)),
            pl.BlockSpec((None, t_lat, hw), lambda b, h: (b, 0, v0 + h)),
            pl.BlockSpec((None, t_ctx, hw), lambda b, h: (b, 0, c_k0 + h)),
            pl.BlockSpec((None, t_ctx, hw), lambda b, h: (b, 0, c_k0 + DIFF_HEADS + h)),
            pl.BlockSpec((t_lat, hw), lambda b, h: (0, 0)),
            pl.BlockSpec((t_lat, hw), lambda b, h: (0, 0)),
            pl.BlockSpec((1, DIFF_DV), lambda b, h: (0, 0)),
        ],
        out_specs=pl.BlockSpec((None, t_lat, hw), lambda b, h: (b, 0, h)),
        scratch_shapes=[pltpu.VMEM((t_all, hw), BF16), pltpu.VMEM((hw + ATTN_ONES_ROWS, t_all), BF16),
                        pltpu.VMEM((2, t_all, tq), F32), pltpu.VMEM((2, 1, tq), F32),
                        pltpu.VMEM((2, t_all, tq), F32), pltpu.VMEM((2, 1, tq), F32)],
        compiler_params=_params("parallel", "parallel"),
        name="diff_attn",
    )(lam_vecs, zl, zl, zl, zc, zc, cos, sin, dnorm.reshape(1, DIFF_DV))


def kernel(x, c, ctx, c_ctx, ada_w, ada_b, norm_ffn1, norm_mix, norm_ffn2, ffn1_w_in, ffn1_w_out, ffn2_w_in,
           ffn2_w_out, mix_w_out, even_w_in, gla_gate_w, gla_gate_b, gla_norm, odd_w_in, conv_w, conv_b,
           lambda_q1, lambda_k1, lambda_q2, lambda_k2, diff_norm, final_norm):
    assert DEPTH == 2
    bn, t_lat, d = x.shape
    t_ctx = ctx.shape[1]
    assert bn < MOD_ROWS
    ctx_row = bn

    cond = jnp.concatenate([c, c_ctx[None, :], jnp.zeros((MOD_ROWS - bn - 1, d), F32)], axis=0)
    mods = _modulation(cond, ada_w, ada_b)

    tm = 512
    lat_row = lambda i: i // (t_lat // tm)
    ctx_rows = lambda i: ctx_row
    g1 = norm_ffn1.reshape(DEPTH, 1, d)
    gm = norm_mix.reshape(DEPTH, 1, d)
    g2 = norm_ffn2.reshape(DEPTH, 1, d)
    w1i, w1o = ffn1_w_in.astype(BF16), ffn1_w_out.astype(BF16)
    w2i, w2o = ffn2_w_in.astype(BF16), ffn2_w_out.astype(BF16)
    wmix_a = mix_w_out[:, :FOURIER_WIDTH, :].astype(BF16)
    wmix_b = mix_w_out[:, FOURIER_WIDTH:, :].astype(BF16)

    h = x.reshape(bn * t_lat, d)
    hc = ctx.reshape(bn * t_ctx, d)

    w_even = even_w_in[0].astype(BF16)
    proj0 = (gm, w_even[:, :EVEN_MAIN], w_even[:, EVEN_MAIN:])
    h, zl, gl = _ffn(h, mods, 0, 0, lat_row, g1, w1i, w1o, tm=tm, proj=proj0)
    hc, zc, gc = _ffn(hc, mods, 0, 0, ctx_rows, g1, w1i, w1o, tm=tm, proj=proj0)
    zl = zl.reshape(bn, t_lat, EVEN_MAIN)
    zc = zc.reshape(bn, t_ctx, EVEN_MAIN)
    gl = gl.reshape(bn, t_lat, 2 * GLA_GATE_RANK)
    gc = gc.reshape(bn, t_ctx, 2 * GLA_GATE_RANK)
    yf_l = _fourier(zl, t_lat, bn)
    yf_c = _fourier(zc, t_ctx, bn)
    yg_c, yg_l = _gla(zc, gc, zl, gl, gla_gate_w[0], gla_gate_b[0], gla_norm[0], bn, t_ctx, t_lat)
    mix_l = (yf_l.reshape(bn * t_lat, -1), yg_l.reshape(bn * t_lat, -1), wmix_a, wmix_b)
    mix_c = (yf_c.reshape(bn * t_ctx, -1), yg_c.reshape(bn * t_ctx, -1), wmix_a, wmix_b)
    h = _ffn(h, mods, 0, 6, lat_row, g2, w2i, w2o, tm=tm, mix=mix_l)
    hc = _ffn(hc, mods, 0, 6, ctx_rows, g2, w2i, w2o, tm=tm, mix=mix_c)

    w_odd = odd_w_in[0].astype(BF16)
    kv_w = 2 * DIFF_HEADS * DIFF_DV
    h, zl = _ffn(h, mods, 1, 0, lat_row, g1, w1i, w1o, tm=tm, proj=(gm, w_odd, None))
    hc, zc = _ffn(hc, mods, 1, 0, ctx_rows, g1, w1i, w1o, tm=tm, proj=(gm, w_odd[:, ODD_IN - kv_w:], None))
    zl = zl.reshape(bn, t_lat, ODD_IN)
    zc = zc.reshape(bn, t_ctx, kv_w)
    lam_init = 0.8 - 0.6 * math.exp(-0.3 * 1)
    lam_vecs = jnp.stack([lambda_q1[0], lambda_k1[0], lambda_q2[0], lambda_k2[0]]).astype(F32)
    y_conv = _short_conv(zl, conv_w[0], conv_b[0], bn, t_lat)
    y_att = _diff_attn(zl, zc, lam_vecs, diff_norm[0], lam_init, bn, t_lat, t_ctx, 0)
    mix_l = (y_conv.reshape(bn * t_lat, -1), y_att.reshape(bn * t_lat, -1), wmix_a, wmix_b)
    h = _ffn(h, mods, 1, 6, lat_row, g2, w2i, w2o, tm=tm, mix=mix_l, final_gain=final_norm.reshape(1, d))
    return h.reshape(bn, t_lat, d)
```

```python
import functools
import math

import numpy as np
import jax
import jax.numpy as jnp
from jax import lax
from jax.experimental import pallas as pl
from jax.experimental.pallas import tpu as pltpu

D_MODEL = 1024
DEPTH = 2
GRID_W = 64
HEAD_DIM = 64
N_MOD = 9
FFN_HIDDEN = 2816
NORM_EPS = 1e-6
FOURIER_GROUPS = 4
FOURIER_WIDTH = FOURIER_GROUPS * HEAD_DIM
GLA_HEADS = 6
GLA_DK = 64
GLA_DV = 128
GLA_GATE_RANK = 16
GLA_TAU = 16.0
GLA_CHUNK = 64
GLA_GROUP = 8
FFN_TILE = 256
PROJ_TILE = 512
CONV_WIDTH = 4 * HEAD_DIM
DIFF_HEADS = 6
DIFF_DV = 2 * HEAD_DIM
ROPE_THETA = 10000.0
ROPE_AXIS_DIM = HEAD_DIM // 2

EVEN_MAIN = FOURIER_WIDTH + 2 * GLA_HEADS * GLA_DK + 2 * GLA_HEADS * GLA_DV
ODD_IN = 3 * CONV_WIDTH + 3 * DIFF_HEADS * DIFF_DV

MOD_ROWS = 16
VMEM_LIMIT = 48 * 1024 * 1024

F32 = jnp.float32
BF16 = jnp.bfloat16


def _params(*sem):
    return pltpu.CompilerParams(dimension_semantics=sem, vmem_limit_bytes=VMEM_LIMIT)


def _sigmoid(x):
    return 1.0 / (1.0 + jnp.exp(-x))


def _dot(a, b):
    return jnp.dot(a, b, preferred_element_type=F32)


def _dot_nt(a, b):
    return lax.dot_general(a, b, (((1,), (1,)), ((), ())), preferred_element_type=F32)


def _dot_tn(a, b):
    return lax.dot_general(a, b, (((0,), (0,)), ((), ())), preferred_element_type=F32)


def _rms(x):
    return x * lax.rsqrt(jnp.mean(x * x, axis=-1, keepdims=True) + NORM_EPS)


def _mod_kernel(cond_ref, w_ref, b_ref, o_ref):
    c = cond_ref[...]
    s = c * _sigmoid(c)
    w = w_ref[...]
    w_hi = w.astype(BF16)
    w_lo = (w - w_hi.astype(F32)).astype(BF16)
    s_hi = s.astype(BF16)
    s_lo = (s - s_hi.astype(F32)).astype(BF16)
    o_ref[...] = _dot(s_hi, w_hi) + _dot(s_lo, w_hi) + _dot(s_hi, w_lo) + b_ref[...]


def _modulation(cond, ada_w, ada_b):
    n = N_MOD * D_MODEL
    tn = n // 8
    out = pl.pallas_call(
        _mod_kernel,
        out_shape=jax.ShapeDtypeStruct((DEPTH, MOD_ROWS, n), F32),
        grid=(DEPTH, n // tn),
        in_specs=[
            pl.BlockSpec((MOD_ROWS, D_MODEL), lambda l, j: (0, 0)),
            pl.BlockSpec((None, D_MODEL, tn), lambda l, j: (l, 0, j)),
            pl.BlockSpec((None, 1, tn), lambda l, j: (l, 0, j)),
        ],
        out_specs=pl.BlockSpec((None, MOD_ROWS, tn), lambda l, j: (l, 0, j)),
        compiler_params=_params("parallel", "parallel"),
        name="modulation",
    )(cond, ada_w, ada_b.reshape(DEPTH, 1, n))
    return out.reshape(DEPTH, MOD_ROWS, N_MOD, 1, D_MODEL)


def _mod_spec(layer, k, row_of_block):
    return pl.BlockSpec((None, None, None, 1, D_MODEL),
                        lambda i, *_: (layer, row_of_block(i), k, 0, 0))


def _gain_spec(layer):
    return pl.BlockSpec((None, 1, D_MODEL), lambda i, *_: (layer, 0, 0))


def _resident(block_shape, index_map):
    return pl.BlockSpec(block_shape, index_map, pipeline_mode=pl.Buffered(1))


def _ffn_kernel(*refs, with_mix, with_proj, with_gate, with_final, paired, n_first):
    it = iter(refs)
    pairs = iter(paired)

    def rows_in():
        a = next(it)
        if not next(pairs):
            return a[...]
        b = next(it)
        return jnp.where(pl.program_id(0) < n_first, a[...], b[...])

    x = rows_in()
    if with_mix:
        ma, mb = rows_in(), rows_in()
        wa_ref, wb_ref, gm_ref = (next(it) for _ in range(3))
    sh_ref, sc_ref, g_ref, gain_ref, wi_ref, wo_ref = (next(it) for _ in range(6))
    if with_proj:
        psh_ref, psc_ref, pgain_ref, wp_ref = (next(it) for _ in range(4))
    if with_gate:
        wgt_ref = next(it)
    if with_final:
        fn_ref = next(it)
    o_ref = next(it)
    if with_proj:
        z_ref = next(it)
    if with_gate:
        zg_ref = next(it)
    a_scr = next(it)

    if with_mix:
        x = x + gm_ref[...] * (_dot(ma, wa_ref[...]) + _dot(mb, wb_ref[...]))
    xn = (_rms(x) * gain_ref[...] * (1.0 + sc_ref[...]) + sh_ref[...]).astype(BF16)
    for j in range(FFN_HIDDEN // FFN_TILE):
        lo = j * FFN_TILE
        g = _dot(xn, wi_ref[:, lo:lo + FFN_TILE])
        u = _dot(xn, wi_ref[:, FFN_HIDDEN + lo:FFN_HIDDEN + lo + FFN_TILE])
        a_scr[:, lo:lo + FFN_TILE] = (g * _sigmoid(g) * u).astype(BF16)
    out = x + (0.5 * g_ref[...]) * _dot(a_scr[...], wo_ref[...])
    if with_final:
        out = _rms(out) * fn_ref[...]
    o_ref[...] = out
    if with_proj:
        xm = (_rms(out) * pgain_ref[...] * (1.0 + psc_ref[...]) + psh_ref[...]).astype(BF16)
        n_out = z_ref.shape[1]
        for lo in range(0, n_out, PROJ_TILE):
            z_ref[:, lo:lo + PROJ_TILE] = _dot(xm, wp_ref[:, lo:lo + PROJ_TILE]).astype(z_ref.dtype)
        if with_gate:
            zg_ref[...] = _dot(xm, wgt_ref[...])


def _ffn(h, mods, layer, mod_base, row_of_block, gain, w_in, w_out, *, tm, n_first=None, n_rows=None, mix=None,
         proj=None, final_gain=None):
    m = n_rows or (sum(a.shape[0] for a in h) if isinstance(h, tuple) else h.shape[0])
    row = lambda i: (i, 0)
    args, specs, paired = [], [], []

    def add_rows(a):
        paired.append(isinstance(a, tuple))
        if paired[-1]:
            first, second = a
            assert first.shape[0] == n_first * tm
            args.extend([first, second])
            specs.extend([pl.BlockSpec((tm, first.shape[1]), lambda i: (jnp.minimum(i, n_first - 1), 0)),
                          pl.BlockSpec((tm, second.shape[1]), lambda i: (jnp.maximum(i - n_first, 0), 0))])
        else:
            args.append(a)
            specs.append(pl.BlockSpec((tm, a.shape[1]), row))

    add_rows(h)
    if mix is not None:
        ma, mb, wa, wb = mix
        add_rows(ma)
        add_rows(mb)
        ka, kb = wa.shape[1], wb.shape[1]
        args += [wa, wb, mods]
        specs += [
            _resident((None, ka, D_MODEL), lambda i: (layer, 0, 0)),
            _resident((None, kb, D_MODEL), lambda i: (layer, 0, 0)),
            _mod_spec(layer, 5, row_of_block),
        ]
    args += [mods, mods, mods, gain, w_in, w_out]
    specs += [
        _mod_spec(layer, mod_base, row_of_block),
        _mod_spec(layer, mod_base + 1, row_of_block),
        _mod_spec(layer, mod_base + 2, row_of_block),
        _gain_spec(layer),
        _resident((None, D_MODEL, 2 * FFN_HIDDEN), lambda i: (layer, 0, 0)),
        _resident((None, FFN_HIDDEN, D_MODEL), lambda i: (layer, 0, 0)),
    ]
    out_shape = [jax.ShapeDtypeStruct((m, D_MODEL), F32)]
    out_specs = [pl.BlockSpec((tm, D_MODEL), row)]
    with_gate = False
    if proj is not None:
        pgain, wp, w_gate = proj
        n_out = wp.shape[1]
        assert n_out % PROJ_TILE == 0
        args += [mods, mods, pgain, wp]
        specs += [_mod_spec(layer, 3, row_of_block), _mod_spec(layer, 4, row_of_block), _gain_spec(layer),
                  _resident((D_MODEL, n_out), lambda i: (0, 0))]
        out_shape.append(jax.ShapeDtypeStruct((m, n_out), BF16))
        out_specs.append(pl.BlockSpec((tm, n_out), row))
        if w_gate is not None:
            with_gate = True
            ng = w_gate.shape[1]
            args.append(w_gate)
            specs.append(_resident((D_MODEL, ng), lambda i: (0, 0)))
            out_shape.append(jax.ShapeDtypeStruct((m, ng), F32))
            out_specs.append(pl.BlockSpec((tm, ng), row))
    if final_gain is not None:
        args.append(final_gain)
        specs.append(pl.BlockSpec((1, D_MODEL), lambda i: (0, 0)))
    outs = pl.pallas_call(
        functools.partial(_ffn_kernel, with_mix=mix is not None, with_proj=proj is not None, with_gate=with_gate,
                          with_final=final_gain is not None, paired=tuple(paired), n_first=n_first),
        out_shape=out_shape,
        grid=(m // tm,),
        in_specs=specs,
        out_specs=out_specs,
        scratch_shapes=[pltpu.VMEM((tm, FFN_HIDDEN), BF16)],
        compiler_params=_params("parallel"),
        name="ffn",
    )(*args)
    return outs if proj is not None else outs[0]


def _dft_tables(t):
    k = (np.arange(t, dtype=np.int64)[:, None] * np.arange(t, dtype=np.int64)[None, :]) % t
    ang = 2.0 * np.pi * k.astype(np.float64) / t
    pos = np.concatenate([np.cos(ang), -np.sin(ang)], axis=1).astype(np.float32)
    kc = (np.arange(HEAD_DIM)[:, None] * np.arange(HEAD_DIM)[None, :]) % HEAD_DIM
    angc = 2.0 * np.pi * kc.astype(np.float64) / HEAD_DIM
    eye = np.eye(FOURIER_GROUPS)
    chan = np.concatenate([np.kron(eye, np.cos(angc)), np.kron(eye, np.sin(angc))], axis=1).astype(np.float32)
    return pos, chan


def _chan_dft_kernel(z_ref, c_ref, o_ref, *, t):
    ab = _dot(z_ref[...].astype(BF16), c_ref[...].astype(BF16))
    o_ref[0:t, :] = ab[:, :FOURIER_WIDTH].astype(BF16)
    o_ref[t:2 * t, :] = ab[:, FOURIER_WIDTH:].astype(BF16)


def _pos_dft_kernel(p_ref, ab_ref, o_ref, p_scr, *, scale):
    @pl.when(pl.program_id(1) == 0)
    def _():
        p_scr[...] = p_ref[...].astype(BF16)

    o_ref[...] = (_dot(p_scr[...], ab_ref[...]) * scale).astype(BF16)


def _fourier(z, t, bn, row0=0):
    pos, chan = _dft_tables(t)
    off = row0 // t
    ab = pl.pallas_call(
        functools.partial(_chan_dft_kernel, t=t),
        out_shape=jax.ShapeDtypeStruct((bn, 2 * t, FOURIER_WIDTH), BF16),
        grid=(bn,),
        in_specs=[pl.BlockSpec((t, FOURIER_WIDTH), lambda b: (off + b, 0)),
                  pl.BlockSpec((FOURIER_WIDTH, 2 * FOURIER_WIDTH), lambda b: (0, 0))],
        out_specs=pl.BlockSpec((None, 2 * t, FOURIER_WIDTH), lambda b: (b, 0, 0)),
        compiler_params=_params("parallel"),
        name="chan_dft",
    )(z, jnp.asarray(chan))
    tq = min(t, 512)
    return pl.pallas_call(
        functools.partial(_pos_dft_kernel, scale=1.0 / math.sqrt(t * HEAD_DIM)),
        out_shape=jax.ShapeDtypeStruct((bn * t, FOURIER_WIDTH), BF16),
        grid=(t // tq, bn),
        in_specs=[pl.BlockSpec((tq, 2 * t), lambda i, b: (i, 0)),
                  pl.BlockSpec((None, 2 * t, FOURIER_WIDTH), lambda i, b: (b, 0, 0))],
        out_specs=pl.BlockSpec((tq, FOURIER_WIDTH), lambda i, b: (b * (t // tq) + i, 0)),
        scratch_shapes=[pltpu.VMEM((tq, 2 * t), BF16)],
        compiler_params=_params("parallel", "arbitrary"),
        name="pos_dft",
    )(jnp.asarray(pos), ab)


def _gla_segment(q_ref, k_ref, v_ref, r_ref, g_ref, y_ref, gw_ref, gb_ref, gn_ref,
                 b_scr, o_scr, s_scr, n_rows):
    c_len = GLA_CHUNK
    n = n_rows // c_len
    pair_k = 2 * GLA_DK
    pair_v = 2 * GLA_DV

    for d in range(2):
        zg = g_ref[:, d * GLA_GATE_RANK:(d + 1) * GLA_GATE_RANK]
        logit = _dot(zg.astype(BF16), gw_ref[d].astype(BF16)) + gb_ref[d]
        b_scr[d, 0:n_rows, :] = (jnp.minimum(logit, 0.0) - jnp.log(1.0 + jnp.exp(-jnp.abs(logit)))) * (1.0 / GLA_TAU)

    lane = lax.broadcasted_iota(jnp.int32, (c_len, pair_k), 1)
    col = lax.broadcasted_iota(jnp.int32, (c_len, pair_v), 1)
    ti = lax.broadcasted_iota(jnp.int32, (c_len, c_len), 0)
    tj = lax.broadcasted_iota(jnp.int32, (c_len, c_len), 1)
    blk = (lax.broadcasted_iota(jnp.int32, (pair_v, pair_k), 0) // GLA_DV
           == lax.broadcasted_iota(jnp.int32, (pair_v, pair_k), 1) // GLA_DK)

    group = math.gcd(GLA_GROUP, n)
    keep2 = [jnp.concatenate([m, m], axis=0) for m in (tj <= ti, tj >= ti)]
    ti2 = lax.broadcasted_iota(jnp.int32, (c_len, 2 * c_len), 0)
    tj2 = lax.broadcasted_iota(jnp.int32, (c_len, 2 * c_len), 1) & (c_len - 1)
    tri2 = [jnp.where(m, 1.0, 0.0).astype(BF16) for m in (tj2 <= ti2, tj2 >= ti2)]

    def body(i, carry):
        units = []
        for d in range(2):
            for g in range(group):
                c = i * group + g if d == 0 else n - 1 - (i * group + g)
                rows = pl.ds(pl.multiple_of(c * c_len, c_len), c_len)
                lg = b_scr[d, rows, :]
                hi = lg.astype(BF16)
                rest = (lg - hi.astype(F32)).astype(BF16)
                units.append(dict(d=d, rows=rows, b=_dot(tri2[d], jnp.concatenate([hi, rest], axis=0))))
        for u in units:
            d, rows, b = u["d"], u["rows"], u["b"]
            bl = b[c_len - 1:c_len, :] if d == 0 else b[0:1, :]
            k = k_ref[rows, :].astype(F32)
            v16 = v_ref[rows, :].astype(BF16)
            qd = q_ref[rows, :].astype(F32) * (GLA_DK ** -0.5) * jnp.exp(b)
            kd = (k * jnp.exp(-b)).astype(BF16)
            kdec = (k * jnp.exp(bl - b)).astype(BF16)
            q2 = jnp.concatenate([jnp.where((lane // GLA_DK) == hh, qd, 0.0) for hh in range(2)], axis=0)
            att = _dot_nt(q2.astype(BF16), kd)
            upd = _dot_tn(v16, kdec)
            u.update(qd=qd.astype(BF16), v16=v16, att=att, upd=upd, dec=jnp.exp(bl))
        for d in range(2):
            s = s_scr[d]
            for u in units:
                if u["d"] == d:
                    u["o"] = _dot_nt(u["qd"], s.astype(BF16))
                    s = jnp.where(blk, s * u["dec"] + u["upd"], 0.0)
            s_scr[d] = s
        for u in units:
            att = jnp.where(keep2[u["d"]], u["att"], 0.0).astype(BF16)
            pv = _dot(att, u["v16"])
            intra = jnp.where((col // GLA_DV) == 0, pv[0:c_len, :], pv[c_len:, :])
            o_scr[u["d"], u["rows"], :] = u["o"] + intra
        return carry

    lax.fori_loop(0, n // group, body, 0)

    gn = gn_ref[...]
    for hh in range(2):
        sl = slice(hh * GLA_DV, (hh + 1) * GLA_DV)
        o = o_scr[0, 0:n_rows, sl] + o_scr[1, 0:n_rows, sl]
        rr = r_ref[:, sl].astype(F32)
        y_ref[:, sl] = (_rms(o) * gn * (rr * _sigmoid(rr))).astype(y_ref.dtype)


def _gla_kernel(qc_ref, kc_ref, vc_ref, rc_ref, gc_ref, ql_ref, kl_ref, vl_ref, rl_ref, gl_ref,
                gw_ref, gb_ref, gn_ref, yc_ref, yl_ref, b_scr, o_scr, s_scr, *, t_ctx, t_lat):
    s_scr[...] = jnp.zeros_like(s_scr)
    _gla_segment(qc_ref, kc_ref, vc_ref, rc_ref, gc_ref, yc_ref, gw_ref, gb_ref, gn_ref,
                 b_scr, o_scr, s_scr, t_ctx)
    _gla_segment(ql_ref, kl_ref, vl_ref, rl_ref, gl_ref, yl_ref, gw_ref, gb_ref, gn_ref,
                 b_scr, o_scr, s_scr, t_lat)


def _gla(z, zg, gate_w, gate_b, gla_g, bn, t_ctx, t_lat, ctx_row0):
    pk, pv = 2 * GLA_DK, 2 * GLA_DV
    q0 = FOURIER_WIDTH // pk
    k0 = (FOURIER_WIDTH + GLA_HEADS * GLA_DK) // pk
    v0 = (FOURIER_WIDTH + 2 * GLA_HEADS * GLA_DK) // pv
    r0 = (FOURIER_WIDTH + 2 * GLA_HEADS * GLA_DK + GLA_HEADS * GLA_DV) // pv

    def seg_specs(t, row0):
        off = row0 // t
        return [pl.BlockSpec((t, pk), lambda b, p: (off + b, q0 + p)),
                pl.BlockSpec((t, pk), lambda b, p: (off + b, k0 + p)),
                pl.BlockSpec((t, pv), lambda b, p: (off + b, v0 + p)),
                pl.BlockSpec((t, pv), lambda b, p: (off + b, r0 + p)),
                pl.BlockSpec((t, 2 * GLA_GATE_RANK), lambda b, p: (off + b, 0))]

    wdt = GLA_HEADS * GLA_DV
    return pl.pallas_call(
        functools.partial(_gla_kernel, t_ctx=t_ctx, t_lat=t_lat),
        out_shape=[jax.ShapeDtypeStruct((bn * t_ctx, wdt), BF16), jax.ShapeDtypeStruct((bn * t_lat, wdt), BF16)],
        grid=(bn, GLA_HEADS // 2),
        in_specs=seg_specs(t_ctx, ctx_row0) + seg_specs(t_lat, 0) + [
            pl.BlockSpec((2, GLA_GATE_RANK, pk), lambda b, p: (0, 0, p)),
            pl.BlockSpec((2, 1, pk), lambda b, p: (0, 0, p)),
            pl.BlockSpec((1, GLA_DV), lambda b, p: (0, 0))],
        out_specs=[pl.BlockSpec((t_ctx, pv), lambda b, p: (b, p)),
                   pl.BlockSpec((t_lat, pv), lambda b, p: (b, p))],
        scratch_shapes=[pltpu.VMEM((2, t_lat, pk), F32), pltpu.VMEM((2, t_lat, pv), F32),
                        pltpu.VMEM((2, pv, pk), F32)],
        compiler_params=_params("parallel", "parallel"),
        name="gla",
    )(z, z, z, z, zg, z, z, z, z, zg, gate_w, gate_b.reshape(2, 1, GLA_HEADS * GLA_DK),
      gla_g.reshape(1, GLA_DV))


def _conv_kernel(zb_ref, zc_ref, zx_ref, w_ref, b_ref, o_ref, *, t):
    u = zc_ref[...].astype(F32) * zx_ref[...].astype(F32)
    row = lax.broadcasted_iota(jnp.int32, u.shape, 0)
    prev = jnp.where(row >= 1, pltpu.roll(u, 1, 0), 0.0)
    nxt = jnp.where(row < t - 1, pltpu.roll(u, t - 1, 0), 0.0)
    y = prev * w_ref[0:1, :] + u * w_ref[1:2, :] + nxt * w_ref[2:3, :] + b_ref[...]
    o_ref[...] = (zb_ref[...].astype(F32) * y).astype(o_ref.dtype)


def _short_conv(z, conv_w, conv_b, bn, t):
    cw = CONV_WIDTH
    return pl.pallas_call(
        functools.partial(_conv_kernel, t=t),
        out_shape=jax.ShapeDtypeStruct((bn * t, cw), BF16),
        grid=(bn,),
        in_specs=[pl.BlockSpec((t, cw), lambda b: (b, 0)),
                  pl.BlockSpec((t, cw), lambda b: (b, 1)),
                  pl.BlockSpec((t, cw), lambda b: (b, 2)),
                  pl.BlockSpec((3, cw), lambda b: (0, 0)),
                  pl.BlockSpec((1, cw), lambda b: (0, 0))],
        out_specs=pl.BlockSpec((t, cw), lambda b: (b, 0)),
        compiler_params=_params("parallel"),
        name="short_conv",
    )(z, z, z, conv_w, conv_b.reshape(1, cw))


def _rope_tables(t):
    rows = t // GRID_W
    row = jnp.repeat(jnp.arange(rows), GRID_W).astype(F32)
    col = jnp.tile(jnp.arange(GRID_W), rows).astype(F32)
    n = ROPE_AXIS_DIM // 2
    inv = ROPE_THETA ** (-jnp.arange(n, dtype=F32) / n)
    ar, ac = row[:, None] * inv, col[:, None] * inv
    ang = jnp.concatenate([ar, ar, ac, ac], axis=-1)
    sign = jnp.tile(jnp.concatenate([-jnp.ones((n,), F32), jnp.ones((n,), F32)]), 2)
    cos = jnp.cos(ang)
    sin = jnp.sin(ang) * sign
    return jnp.tile(cos, (1, 2)), jnp.tile(sin, (1, 2))


def _rope(x, cos, sin):
    lane = lax.broadcasted_iota(jnp.int32, x.shape, 1)
    n = ROPE_AXIS_DIM // 2
    w = x.shape[1]
    partner = jnp.where((lane & (2 * n - 1)) < n, pltpu.roll(x, w - n, 1), pltpu.roll(x, n, 1))
    return x * cos + partner * sin


ATTN_ONES_ROWS = 16


def _attn_kernel(lam_ref, q_ref, kl_ref, vl_ref, kc_ref, vc_ref, cos_ref, sin_ref, dn_ref,
                 o_ref, k_scr, vt_scr, s0_scr, m0_scr, s1_scr, m1_scr, *, t_lat, lam_init, tq):
    hw = 2 * HEAD_DIM
    t_all = k_scr.shape[0]

    k_scr[0:t_lat, :] = _rope(kl_ref[...].astype(F32), cos_ref[...], sin_ref[...]).astype(BF16)
    k_scr[t_lat:, :] = kc_ref[...].astype(BF16)
    vt_scr[0:hw, 0:t_lat] = vl_ref[...].astype(F32).T.astype(BF16)
    vt_scr[0:hw, t_lat:] = vc_ref[...].astype(F32).T.astype(BF16)
    vt_scr[hw:, :] = jnp.ones((ATTN_ONES_ROWS, t_all), BF16)

    lv = lam_ref[...]
    lam = (jnp.exp(jnp.sum(lv[0:1] * lv[1:2], axis=-1, keepdims=True))
           - jnp.exp(jnp.sum(lv[2:3] * lv[3:4], axis=-1, keepdims=True)) + lam_init)
    lane = lax.broadcasted_iota(jnp.int32, (tq, hw), 1)

    n_blocks = t_lat // tq
    slots = ((s0_scr, m0_scr), (s1_scr, m1_scr))

    def scores(i, slot):
        rows = pl.ds(pl.multiple_of(i * tq, tq), tq)
        q = (_rope(q_ref[rows, :].astype(F32), cos_ref[rows, :], sin_ref[rows, :])
             * (HEAD_DIM ** -0.5 * math.log2(math.e)))
        s_scr, m_scr = slots[slot]
        for half in range(2):
            qb = jnp.where((lane // HEAD_DIM) == half, q, 0.0).astype(BF16)
            s = _dot_nt(k_scr[...], qb)
            s_scr[half] = s
            m_scr[half] = jnp.max(s, axis=0, keepdims=True)

    def outputs(i, slot):
        rows = pl.ds(pl.multiple_of(i * tq, tq), tq)
        s_scr, m_scr = slots[slot]
        outs = []
        for half in range(2):
            p = jnp.exp2(s_scr[half] - m_scr[half]).astype(BF16)
            acc = _dot(vt_scr[...], p)
            outs.append(acc[0:hw, :] / acc[hw:hw + 1, :])
        o = (outs[0] - lam * outs[1]).T
        o_ref[rows, :] = (_rms(o) * dn_ref[...] * (1.0 - lam_init)).astype(o_ref.dtype)

    assert n_blocks % 2 == 0
    scores(0, 0)

    def body(j, carry):
        scores(2 * j + 1, 1)
        outputs(2 * j, 0)
        scores(2 * j + 2, 0)
        outputs(2 * j + 1, 1)
        return carry

    lax.fori_loop(0, n_blocks // 2 - 1, body, 0)
    scores(n_blocks - 1, 1)
    outputs(n_blocks - 2, 0)
    outputs(n_blocks - 1, 1)


def _diff_attn(z, lam_vecs, dnorm, lam_init, bn, t_lat, t_ctx, ctx_row0):
    hw = 2 * HEAD_DIM
    q0 = 3 * CONV_WIDTH // hw
    k0 = q0 + DIFF_HEADS
    v0 = k0 + DIFF_HEADS
    tq = 256
    t_all = t_lat + t_ctx
    coff = ctx_row0 // t_ctx
    cos, sin = _rope_tables(t_lat)
    return pl.pallas_call(
        functools.partial(_attn_kernel, t_lat=t_lat, lam_init=lam_init, tq=tq),
        out_shape=jax.ShapeDtypeStruct((bn * t_lat, DIFF_HEADS * DIFF_DV), BF16),
        grid=(bn, DIFF_HEADS),
        in_specs=[
            pl.BlockSpec((4, HEAD_DIM), lambda b, h: (0, 0)),
            pl.BlockSpec((t_lat, hw), lambda b, h: (b, q0 + h)),
            pl.BlockSpec((t_lat, hw), lambda b, h: (b, k0 + h)),
            pl.BlockSpec((t_lat, hw), lambda b, h: (b, v0 + h)),
            pl.BlockSpec((t_ctx, hw), lambda b, h: (coff + b, k0 + h)),
            pl.BlockSpec((t_ctx, hw), lambda b, h: (coff + b, v0 + h)),
            pl.BlockSpec((t_lat, hw), lambda b, h: (0, 0)),
            pl.BlockSpec((t_lat, hw), lambda b, h: (0, 0)),
            pl.BlockSpec((1, DIFF_DV), lambda b, h: (0, 0)),
        ],
        out_specs=pl.BlockSpec((t_lat, hw), lambda b, h: (b, h)),
        scratch_shapes=[pltpu.VMEM((t_all, hw), BF16), pltpu.VMEM((hw + ATTN_ONES_ROWS, t_all), BF16),
                        pltpu.VMEM((2, t_all, tq), F32), pltpu.VMEM((2, 1, tq), F32),
                        pltpu.VMEM((2, t_all, tq), F32), pltpu.VMEM((2, 1, tq), F32)],
        compiler_params=_params("parallel", "parallel"),
        name="diff_attn",
    )(lam_vecs, z, z, z, z, z, cos, sin, dnorm.reshape(1, DIFF_DV))


def kernel(x, c, ctx, c_ctx, ada_w, ada_b, norm_ffn1, norm_mix, norm_ffn2, ffn1_w_in, ffn1_w_out, ffn2_w_in,
           ffn2_w_out, mix_w_out, even_w_in, gla_gate_w, gla_gate_b, gla_norm, odd_w_in, conv_w, conv_b,
           lambda_q1, lambda_k1, lambda_q2, lambda_k2, diff_norm, final_norm):
    assert DEPTH == 2
    bn, t_lat, d = x.shape
    t_ctx = ctx.shape[1]
    assert bn < MOD_ROWS
    ctx_row = bn

    cond = jnp.concatenate([c, c_ctx[None, :], jnp.zeros((MOD_ROWS - bn - 1, d), F32)], axis=0)
    mods = _modulation(cond, ada_w, ada_b)

    tm = 512
    n_lat = bn * t_lat
    lat_blocks = n_lat // tm
    lat_row = lambda i: i // (t_lat // tm)
    all_row = lambda i: jnp.where(i < lat_blocks, i // (t_lat // tm), ctx_row)
    g1 = norm_ffn1.reshape(DEPTH, 1, d)
    gm = norm_mix.reshape(DEPTH, 1, d)
    g2 = norm_ffn2.reshape(DEPTH, 1, d)
    w1i, w1o = ffn1_w_in.astype(BF16), ffn1_w_out.astype(BF16)
    w2i, w2o = ffn2_w_in.astype(BF16), ffn2_w_out.astype(BF16)
    wmix_a = mix_w_out[:, :FOURIER_WIDTH, :].astype(BF16)
    wmix_b = mix_w_out[:, FOURIER_WIDTH:, :].astype(BF16)

    w_even = even_w_in[0].astype(BF16)
    proj0 = (gm, w_even[:, :EVEN_MAIN], w_even[:, EVEN_MAIN:])
    h, z, zg = _ffn((x.reshape(n_lat, d), ctx.reshape(bn * t_ctx, d)), mods, 0, 0, all_row, g1, w1i, w1o,
                    tm=tm, n_first=lat_blocks, proj=proj0)
    yf_l = _fourier(z, t_lat, bn)
    yf_c = _fourier(z, t_ctx, bn, row0=n_lat)
    yg_c, yg_l = _gla(z, zg, gla_gate_w[0], gla_gate_b[0], gla_norm[0], bn, t_ctx, t_lat, n_lat)
    h = _ffn(h, mods, 0, 6, all_row, g2, w2i, w2o, tm=tm, n_first=lat_blocks,
             mix=((yf_l, yf_c), (yg_l, yg_c), wmix_a, wmix_b))

    w_odd = odd_w_in[0].astype(BF16)
    h, z = _ffn(h, mods, 1, 0, all_row, g1, w1i, w1o, tm=tm, proj=(gm, w_odd, None))
    lam_init = 0.8 - 0.6 * math.exp(-0.3 * 1)
    lam_vecs = jnp.stack([lambda_q1[0], lambda_k1[0], lambda_q2[0], lambda_k2[0]]).astype(F32)
    y_conv = _short_conv(z, conv_w[0], conv_b[0], bn, t_lat)
    y_att = _diff_attn(z, lam_vecs, diff_norm[0], lam_init, bn, t_lat, t_ctx, n_lat)
    h = _ffn(h, mods, 1, 6, lat_row, g2, w2i, w2o, tm=tm, mix=(y_conv, y_att, wmix_a, wmix_b),
             final_gain=final_norm.reshape(1, d), n_rows=n_lat)
    return h.reshape(bn, t_lat, d)
```

```python
import functools
import math

import numpy as np
import jax
import jax.numpy as jnp
from jax import lax
from jax.experimental import pallas as pl
from jax.experimental.pallas import tpu as pltpu

D_MODEL = 1024
DEPTH = 2
GRID_W = 64
HEAD_DIM = 64
N_MOD = 9
FFN_HIDDEN = 2816
NORM_EPS = 1e-6
FOURIER_GROUPS = 4
FOURIER_WIDTH = FOURIER_GROUPS * HEAD_DIM
GLA_HEADS = 6
GLA_DK = 64
GLA_DV = 128
GLA_GATE_RANK = 16
GLA_TAU = 16.0
GLA_CHUNK = 64
GLA_GROUP = 8
FFN_TILE = 256
PROJ_TILE = 512
CONV_WIDTH = 4 * HEAD_DIM
DIFF_HEADS = 6
DIFF_DV = 2 * HEAD_DIM
ROPE_THETA = 10000.0
ROPE_AXIS_DIM = HEAD_DIM // 2

EVEN_MAIN = FOURIER_WIDTH + 2 * GLA_HEADS * GLA_DK + 2 * GLA_HEADS * GLA_DV
ODD_IN = 3 * CONV_WIDTH + 3 * DIFF_HEADS * DIFF_DV

MOD_ROWS = 16
VMEM_LIMIT = 48 * 1024 * 1024

F32 = jnp.float32
BF16 = jnp.bfloat16


def _params(*sem):
    return pltpu.CompilerParams(dimension_semantics=sem, vmem_limit_bytes=VMEM_LIMIT)


def _sigmoid(x):
    return 1.0 / (1.0 + jnp.exp(-x))


def _dot(a, b):
    return jnp.dot(a, b, preferred_element_type=F32)


def _dot_nt(a, b):
    return lax.dot_general(a, b, (((1,), (1,)), ((), ())), preferred_element_type=F32)


def _dot_tn(a, b):
    return lax.dot_general(a, b, (((0,), (0,)), ((), ())), preferred_element_type=F32)


def _rms(x):
    return x * lax.rsqrt(jnp.mean(x * x, axis=-1, keepdims=True) + NORM_EPS)


def _mod_kernel(cond_ref, w_ref, b_ref, o_ref):
    c = cond_ref[...]
    s = c * _sigmoid(c)
    w = w_ref[...]
    w_hi = w.astype(BF16)
    w_lo = (w - w_hi.astype(F32)).astype(BF16)
    s_hi = s.astype(BF16)
    s_lo = (s - s_hi.astype(F32)).astype(BF16)
    o_ref[...] = _dot(s_hi, w_hi) + _dot(s_lo, w_hi) + _dot(s_hi, w_lo) + b_ref[...]


def _modulation(cond, ada_w, ada_b):
    n = N_MOD * D_MODEL
    tn = n // 8
    out = pl.pallas_call(
        _mod_kernel,
        out_shape=jax.ShapeDtypeStruct((DEPTH, MOD_ROWS, n), F32),
        grid=(DEPTH, n // tn),
        in_specs=[
            pl.BlockSpec((MOD_ROWS, D_MODEL), lambda l, j: (0, 0)),
            pl.BlockSpec((None, D_MODEL, tn), lambda l, j: (l, 0, j)),
            pl.BlockSpec((None, 1, tn), lambda l, j: (l, 0, j)),
        ],
        out_specs=pl.BlockSpec((None, MOD_ROWS, tn), lambda l, j: (l, 0, j)),
        compiler_params=_params("parallel", "parallel"),
        name="modulation",
    )(cond, ada_w, ada_b.reshape(DEPTH, 1, n))
    return out.reshape(DEPTH, MOD_ROWS, N_MOD, 1, D_MODEL)


def _mod_spec(layer, k, row_of_block):
    return pl.BlockSpec((None, None, None, 1, D_MODEL),
                        lambda i, *_: (layer, row_of_block(i), k, 0, 0))


def _gain_spec(layer):
    return pl.BlockSpec((None, 1, D_MODEL), lambda i, *_: (layer, 0, 0))


def _resident(block_shape, index_map):
    return pl.BlockSpec(block_shape, index_map, pipeline_mode=pl.Buffered(1))


CAST_STEPS = 32


def _cast_riders(sources, step_of):
    args, in_specs, out_shape, out_specs = [], [], [], []
    for arr, lead, nb in sources:
        _, r, c = arr.shape
        rows, stride = r // nb, CAST_STEPS // nb
        assert rows * nb == r and stride * nb == CAST_STEPS and rows % 16 == 0
        idx = lambda *g, nb=nb, stride=stride: jnp.minimum(step_of(*g) // stride, nb - 1)
        args.append(arr)
        in_specs.append(pl.BlockSpec((None, rows, c), lambda *g, idx=idx, lead=lead: (lead, idx(*g), 0)))
        out_shape.append(jax.ShapeDtypeStruct((r, c), BF16))
        out_specs.append(pl.BlockSpec((rows, c), lambda *g, idx=idx: (idx(*g), 0)))
    return args, in_specs, out_shape, out_specs


def _cast_blocks(src_refs, dst_refs):
    for src, dst in zip(src_refs, dst_refs):
        dst[...] = src[...].astype(BF16)


def _ffn_kernel(*refs, with_mix, with_proj, with_gate, with_final, paired, n_first):
    it = iter(refs)
    pairs = iter(paired)

    def rows_in():
        a = next(it)
        if not next(pairs):
            return a[...]
        b = next(it)
        return jnp.where(pl.program_id(0) < n_first, a[...], b[...])

    x = rows_in()
    if with_mix:
        ma, mb = rows_in(), rows_in()
        wm_ref, gm_ref = next(it), next(it)
    sh_ref, sc_ref, g_ref, gain_ref, wi_ref, wo_ref = (next(it) for _ in range(6))
    if with_proj:
        psh_ref, psc_ref, pgain_ref, wp_ref = (next(it) for _ in range(4))
    if with_gate:
        wgt_ref = next(it)
    if with_final:
        fn_ref = next(it)
    o_ref = next(it)
    if with_proj:
        z_ref = next(it)
    if with_gate:
        zg_ref = next(it)
    a_scr = next(it)

    if with_mix:
        ka = ma.shape[1]
        x = x + gm_ref[...] * (_dot(ma, wm_ref[0:ka, :]) + _dot(mb, wm_ref[ka:, :]))
    xn = (_rms(x) * gain_ref[...] * (1.0 + sc_ref[...]) + sh_ref[...]).astype(BF16)
    for j in range(FFN_HIDDEN // FFN_TILE):
        lo = j * FFN_TILE
        g = _dot(xn, wi_ref[:, lo:lo + FFN_TILE])
        u = _dot(xn, wi_ref[:, FFN_HIDDEN + lo:FFN_HIDDEN + lo + FFN_TILE])
        a_scr[:, lo:lo + FFN_TILE] = (g * _sigmoid(g) * u).astype(BF16)
    out = x + (0.5 * g_ref[...]) * _dot(a_scr[...], wo_ref[...])
    if with_final:
        out = _rms(out) * fn_ref[...]
    o_ref[...] = out
    if with_proj:
        xm = (_rms(out) * pgain_ref[...] * (1.0 + psc_ref[...]) + psh_ref[...]).astype(BF16)
        n_out = z_ref.shape[1]
        for lo in range(0, n_out, PROJ_TILE):
            z_ref[:, lo:lo + PROJ_TILE] = _dot(xm, wp_ref[:, lo:lo + PROJ_TILE]).astype(z_ref.dtype)
        if with_gate:
            zg_ref[...] = _dot(xm, wgt_ref[...])


def _ffn(h, mods, layer, mod_base, row_of_block, gain, w_in, w_out, *, tm, n_first=None, n_rows=None, mix=None,
         proj=None, final_gain=None):
    m = n_rows or (sum(a.shape[0] for a in h) if isinstance(h, tuple) else h.shape[0])
    row = lambda i: (i, 0)
    args, specs, paired = [], [], []

    def add_rows(a):
        paired.append(isinstance(a, tuple))
        if paired[-1]:
            first, second = a
            assert first.shape[0] == n_first * tm
            args.extend([first, second])
            specs.extend([pl.BlockSpec((tm, first.shape[1]), lambda i: (jnp.minimum(i, n_first - 1), 0)),
                          pl.BlockSpec((tm, second.shape[1]), lambda i: (jnp.maximum(i - n_first, 0), 0))])
        else:
            args.append(a)
            specs.append(pl.BlockSpec((tm, a.shape[1]), row))

    add_rows(h)
    if mix is not None:
        ma, mb, w_mix = mix
        add_rows(ma)
        add_rows(mb)
        args += [w_mix, mods]
        specs += [_resident(w_mix.shape, lambda i: (0, 0)), _mod_spec(layer, 5, row_of_block)]
    args += [mods, mods, mods, gain, w_in, w_out]
    specs += [
        _mod_spec(layer, mod_base, row_of_block),
        _mod_spec(layer, mod_base + 1, row_of_block),
        _mod_spec(layer, mod_base + 2, row_of_block),
        _gain_spec(layer),
        _resident((D_MODEL, 2 * FFN_HIDDEN), lambda i: (0, 0)),
        _resident((FFN_HIDDEN, D_MODEL), lambda i: (0, 0)),
    ]
    out_shape = [jax.ShapeDtypeStruct((m, D_MODEL), F32)]
    out_specs = [pl.BlockSpec((tm, D_MODEL), row)]
    with_gate = False
    if proj is not None:
        pgain, wp, w_gate = proj
        n_out = wp.shape[1]
        assert n_out % PROJ_TILE == 0
        args += [mods, mods, pgain, wp]
        specs += [_mod_spec(layer, 3, row_of_block), _mod_spec(layer, 4, row_of_block), _gain_spec(layer),
                  _resident((D_MODEL, n_out), lambda i: (0, 0))]
        out_shape.append(jax.ShapeDtypeStruct((m, n_out), BF16))
        out_specs.append(pl.BlockSpec((tm, n_out), row))
        if w_gate is not None:
            with_gate = True
            ng = w_gate.shape[1]
            args.append(w_gate)
            specs.append(_resident((D_MODEL, ng), lambda i: (0, 0)))
            out_shape.append(jax.ShapeDtypeStruct((m, ng), F32))
            out_specs.append(pl.BlockSpec((tm, ng), row))
    if final_gain is not None:
        args.append(final_gain)
        specs.append(pl.BlockSpec((1, D_MODEL), lambda i: (0, 0)))
    outs = pl.pallas_call(
        functools.partial(_ffn_kernel, with_mix=mix is not None, with_proj=proj is not None, with_gate=with_gate,
                          with_final=final_gain is not None, paired=tuple(paired), n_first=n_first),
        out_shape=out_shape,
        grid=(m // tm,),
        in_specs=specs,
        out_specs=out_specs,
        scratch_shapes=[pltpu.VMEM((tm, FFN_HIDDEN), BF16)],
        compiler_params=_params("parallel"),
        name="ffn",
    )(*args)
    return outs if proj is not None else outs[0]


def _dft_tables(t):
    k = (np.arange(t, dtype=np.int64)[:, None] * np.arange(t, dtype=np.int64)[None, :]) % t
    ang = 2.0 * np.pi * k.astype(np.float64) / t
    pos = np.concatenate([np.cos(ang), -np.sin(ang)], axis=1).astype(np.float32)
    kc = (np.arange(HEAD_DIM)[:, None] * np.arange(HEAD_DIM)[None, :]) % HEAD_DIM
    angc = 2.0 * np.pi * kc.astype(np.float64) / HEAD_DIM
    eye = np.eye(FOURIER_GROUPS)
    chan = np.concatenate([np.kron(eye, np.cos(angc)), np.kron(eye, np.sin(angc))], axis=1).astype(np.float32)
    return pos, chan


def _chan_dft_kernel(z_ref, c_ref, o_ref, *, t):
    ab = _dot(z_ref[...].astype(BF16), c_ref[...].astype(BF16))
    o_ref[0:t, :] = ab[:, :FOURIER_WIDTH].astype(BF16)
    o_ref[t:2 * t, :] = ab[:, FOURIER_WIDTH:].astype(BF16)


def _pos_dft_kernel(*refs, scale, n_cast):
    p_ref, ab_ref = refs[:2]
    o_ref = refs[2 + n_cast]
    p_scr = refs[-1]

    @pl.when(pl.program_id(1) == 0)
    def _():
        p_scr[...] = p_ref[...].astype(BF16)

    o_ref[...] = (_dot(p_scr[...], ab_ref[...]) * scale).astype(BF16)
    _cast_blocks(refs[2:2 + n_cast], refs[3 + n_cast:3 + 2 * n_cast])


def _fourier(z, t, bn, row0=0, cast=()):
    pos, chan = _dft_tables(t)
    off = row0 // t
    ab = pl.pallas_call(
        functools.partial(_chan_dft_kernel, t=t),
        out_shape=jax.ShapeDtypeStruct((bn, 2 * t, FOURIER_WIDTH), BF16),
        grid=(bn,),
        in_specs=[pl.BlockSpec((t, FOURIER_WIDTH), lambda b: (off + b, 0)),
                  pl.BlockSpec((FOURIER_WIDTH, 2 * FOURIER_WIDTH), lambda b: (0, 0))],
        out_specs=pl.BlockSpec((None, 2 * t, FOURIER_WIDTH), lambda b: (b, 0, 0)),
        compiler_params=_params("parallel"),
        name="chan_dft",
    )(z, jnp.asarray(chan))
    tq = min(t, 512)
    c_args, c_in, c_shape, c_out = _cast_riders(cast, lambda i, b: i * bn + b)
    assert not cast or (t // tq) * bn >= CAST_STEPS
    outs = pl.pallas_call(
        functools.partial(_pos_dft_kernel, scale=1.0 / math.sqrt(t * HEAD_DIM), n_cast=len(cast)),
        out_shape=[jax.ShapeDtypeStruct((bn * t, FOURIER_WIDTH), BF16)] + c_shape,
        grid=(t // tq, bn),
        in_specs=[pl.BlockSpec((tq, 2 * t), lambda i, b: (i, 0)),
                  pl.BlockSpec((None, 2 * t, FOURIER_WIDTH), lambda i, b: (b, 0, 0))] + c_in,
        out_specs=[pl.BlockSpec((tq, FOURIER_WIDTH), lambda i, b: (b * (t // tq) + i, 0))] + c_out,
        scratch_shapes=[pltpu.VMEM((tq, 2 * t), BF16)],
        compiler_params=_params("arbitrary", "arbitrary"),
        name="pos_dft",
    )(jnp.asarray(pos), ab, *c_args)
    return outs if cast else outs[0]


def _gla_segment(q_ref, k_ref, v_ref, r_ref, g_ref, y_ref, gw_ref, gb_ref, gn_ref,
                 b_scr, o_scr, s_scr, n_rows):
    c_len = GLA_CHUNK
    n = n_rows // c_len
    pair_k = 2 * GLA_DK
    pair_v = 2 * GLA_DV

    for d in range(2):
        zg = g_ref[:, d * GLA_GATE_RANK:(d + 1) * GLA_GATE_RANK]
        logit = _dot(zg.astype(BF16), gw_ref[d].astype(BF16)) + gb_ref[d]
        b_scr[d, 0:n_rows, :] = (jnp.minimum(logit, 0.0) - jnp.log(1.0 + jnp.exp(-jnp.abs(logit)))) * (1.0 / GLA_TAU)

    lane = lax.broadcasted_iota(jnp.int32, (c_len, pair_k), 1)
    col = lax.broadcasted_iota(jnp.int32, (c_len, pair_v), 1)
    ti = lax.broadcasted_iota(jnp.int32, (c_len, c_len), 0)
    tj = lax.broadcasted_iota(jnp.int32, (c_len, c_len), 1)
    blk = (lax.broadcasted_iota(jnp.int32, (pair_v, pair_k), 0) // GLA_DV
           == lax.broadcasted_iota(jnp.int32, (pair_v, pair_k), 1) // GLA_DK)

    group = math.gcd(GLA_GROUP, n)
    keep2 = [jnp.concatenate([m, m], axis=0) for m in (tj <= ti, tj >= ti)]
    ti2 = lax.broadcasted_iota(jnp.int32, (c_len, 2 * c_len), 0)
    tj2 = lax.broadcasted_iota(jnp.int32, (c_len, 2 * c_len), 1) & (c_len - 1)
    tri2 = [jnp.where(m, 1.0, 0.0).astype(BF16) for m in (tj2 <= ti2, tj2 >= ti2)]

    def body(i, carry):
        units = []
        for d in range(2):
            for g in range(group):
                c = i * group + g if d == 0 else n - 1 - (i * group + g)
                rows = pl.ds(pl.multiple_of(c * c_len, c_len), c_len)
                lg = b_scr[d, rows, :]
                hi = lg.astype(BF16)
                rest = (lg - hi.astype(F32)).astype(BF16)
                units.append(dict(d=d, rows=rows, b=_dot(tri2[d], jnp.concatenate([hi, rest], axis=0))))
        for u in units:
            d, rows, b = u["d"], u["rows"], u["b"]
            bl = b[c_len - 1:c_len, :] if d == 0 else b[0:1, :]
            k = k_ref[rows, :].astype(F32)
            v16 = v_ref[rows, :].astype(BF16)
            qd = q_ref[rows, :].astype(F32) * (GLA_DK ** -0.5) * jnp.exp(b)
            kd = (k * jnp.exp(-b)).astype(BF16)
            kdec = (k * jnp.exp(bl - b)).astype(BF16)
            q2 = jnp.concatenate([jnp.where((lane // GLA_DK) == hh, qd, 0.0) for hh in range(2)], axis=0)
            att = _dot_nt(q2.astype(BF16), kd)
            upd = _dot_tn(v16, kdec)
            u.update(qd=qd.astype(BF16), v16=v16, att=att, upd=upd, dec=jnp.exp(bl))
        for d in range(2):
            s = s_scr[d]
            for u in units:
                if u["d"] == d:
                    u["o"] = _dot_nt(u["qd"], s.astype(BF16))
                    s = jnp.where(blk, s * u["dec"] + u["upd"], 0.0)
            s_scr[d] = s
        for u in units:
            att = jnp.where(keep2[u["d"]], u["att"], 0.0).astype(BF16)
            pv = _dot(att, u["v16"])
            intra = jnp.where((col // GLA_DV) == 0, pv[0:c_len, :], pv[c_len:, :])
            o_scr[u["d"], u["rows"], :] = u["o"] + intra
        return carry

    lax.fori_loop(0, n // group, body, 0)

    gn = gn_ref[...]
    for hh in range(2):
        sl = slice(hh * GLA_DV, (hh + 1) * GLA_DV)
        o = o_scr[0, 0:n_rows, sl] + o_scr[1, 0:n_rows, sl]
        rr = r_ref[:, sl].astype(F32)
        y_ref[:, sl] = (_rms(o) * gn * (rr * _sigmoid(rr))).astype(y_ref.dtype)


def _gla_kernel(qc_ref, kc_ref, vc_ref, rc_ref, gc_ref, ql_ref, kl_ref, vl_ref, rl_ref, gl_ref,
                gw_ref, gb_ref, gn_ref, yc_ref, yl_ref, b_scr, o_scr, s_scr, *, t_ctx, t_lat):
    s_scr[...] = jnp.zeros_like(s_scr)
    _gla_segment(qc_ref, kc_ref, vc_ref, rc_ref, gc_ref, yc_ref, gw_ref, gb_ref, gn_ref,
                 b_scr, o_scr, s_scr, t_ctx)
    _gla_segment(ql_ref, kl_ref, vl_ref, rl_ref, gl_ref, yl_ref, gw_ref, gb_ref, gn_ref,
                 b_scr, o_scr, s_scr, t_lat)


def _gla(z, zg, gate_w, gate_b, gla_g, bn, t_ctx, t_lat, ctx_row0):
    pk, pv = 2 * GLA_DK, 2 * GLA_DV
    q0 = FOURIER_WIDTH // pk
    k0 = (FOURIER_WIDTH + GLA_HEADS * GLA_DK) // pk
    v0 = (FOURIER_WIDTH + 2 * GLA_HEADS * GLA_DK) // pv
    r0 = (FOURIER_WIDTH + 2 * GLA_HEADS * GLA_DK + GLA_HEADS * GLA_DV) // pv

    def seg_specs(t, row0):
        off = row0 // t
        return [pl.BlockSpec((t, pk), lambda b, p: (off + b, q0 + p)),
                pl.BlockSpec((t, pk), lambda b, p: (off + b, k0 + p)),
                pl.BlockSpec((t, pv), lambda b, p: (off + b, v0 + p)),
                pl.BlockSpec((t, pv), lambda b, p: (off + b, r0 + p)),
                pl.BlockSpec((t, 2 * GLA_GATE_RANK), lambda b, p: (off + b, 0))]

    wdt = GLA_HEADS * GLA_DV
    return pl.pallas_call(
        functools.partial(_gla_kernel, t_ctx=t_ctx, t_lat=t_lat),
        out_shape=[jax.ShapeDtypeStruct((bn * t_ctx, wdt), BF16), jax.ShapeDtypeStruct((bn * t_lat, wdt), BF16)],
        grid=(bn, GLA_HEADS // 2),
        in_specs=seg_specs(t_ctx, ctx_row0) + seg_specs(t_lat, 0) + [
            pl.BlockSpec((2, GLA_GATE_RANK, pk), lambda b, p: (0, 0, p)),
            pl.BlockSpec((2, 1, pk), lambda b, p: (0, 0, p)),
            pl.BlockSpec((1, GLA_DV), lambda b, p: (0, 0))],
        out_specs=[pl.BlockSpec((t_ctx, pv), lambda b, p: (b, p)),
                   pl.BlockSpec((t_lat, pv), lambda b, p: (b, p))],
        scratch_shapes=[pltpu.VMEM((2, t_lat, pk), F32), pltpu.VMEM((2, t_lat, pv), F32),
                        pltpu.VMEM((2, pv, pk), F32)],
        compiler_params=_params("parallel", "parallel"),
        name="gla",
    )(z, z, z, z, zg, z, z, z, z, zg, gate_w, gate_b.reshape(2, 1, GLA_HEADS * GLA_DK),
      gla_g.reshape(1, GLA_DV))


def _conv_kernel(zb_ref, zc_ref, zx_ref, w_ref, b_ref, o_ref, *, t):
    u = zc_ref[...].astype(F32) * zx_ref[...].astype(F32)
    row = lax.broadcasted_iota(jnp.int32, u.shape, 0)
    prev = jnp.where(row >= 1, pltpu.roll(u, 1, 0), 0.0)
    nxt = jnp.where(row < t - 1, pltpu.roll(u, t - 1, 0), 0.0)
    y = prev * w_ref[0:1, :] + u * w_ref[1:2, :] + nxt * w_ref[2:3, :] + b_ref[...]
    o_ref[...] = (zb_ref[...].astype(F32) * y).astype(o_ref.dtype)


def _short_conv(z, conv_w, conv_b, bn, t):
    cw = CONV_WIDTH
    return pl.pallas_call(
        functools.partial(_conv_kernel, t=t),
        out_shape=jax.ShapeDtypeStruct((bn * t, cw), BF16),
        grid=(bn,),
        in_specs=[pl.BlockSpec((t, cw), lambda b: (b, 0)),
                  pl.BlockSpec((t, cw), lambda b: (b, 1)),
                  pl.BlockSpec((t, cw), lambda b: (b, 2)),
                  pl.BlockSpec((3, cw), lambda b: (0, 0)),
                  pl.BlockSpec((1, cw), lambda b: (0, 0))],
        out_specs=pl.BlockSpec((t, cw), lambda b: (b, 0)),
        compiler_params=_params("parallel"),
        name="short_conv",
    )(z, z, z, conv_w, conv_b.reshape(1, cw))


def _rope_tables(t):
    rows = t // GRID_W
    row = jnp.repeat(jnp.arange(rows), GRID_W).astype(F32)
    col = jnp.tile(jnp.arange(GRID_W), rows).astype(F32)
    n = ROPE_AXIS_DIM // 2
    inv = ROPE_THETA ** (-jnp.arange(n, dtype=F32) / n)
    ar, ac = row[:, None] * inv, col[:, None] * inv
    ang = jnp.concatenate([ar, ar, ac, ac], axis=-1)
    sign = jnp.tile(jnp.concatenate([-jnp.ones((n,), F32), jnp.ones((n,), F32)]), 2)
    cos = jnp.cos(ang)
    sin = jnp.sin(ang) * sign
    return jnp.tile(cos, (1, 2)), jnp.tile(sin, (1, 2))


def _rope(x, cos, sin):
    lane = lax.broadcasted_iota(jnp.int32, x.shape, 1)
    n = ROPE_AXIS_DIM // 2
    w = x.shape[1]
    partner = jnp.where((lane & (2 * n - 1)) < n, pltpu.roll(x, w - n, 1), pltpu.roll(x, n, 1))
    return x * cos + partner * sin


ATTN_ONES_ROWS = 16


def _attn_kernel(*refs, t_lat, lam_init, tq, n_cast):
    lam_ref, q_ref, kl_ref, vl_ref, kc_ref, vc_ref, cos_ref, sin_ref, dn_ref = refs[:9]
    o_ref = refs[9 + n_cast]
    k_scr, vt_scr, s0_scr, m0_scr, s1_scr, m1_scr = refs[-6:]
    _cast_blocks(refs[9:9 + n_cast], refs[10 + n_cast:10 + 2 * n_cast])
    hw = 2 * HEAD_DIM
    t_all = k_scr.shape[0]

    k_scr[0:t_lat, :] = _rope(kl_ref[...].astype(F32), cos_ref[...], sin_ref[...]).astype(BF16)
    k_scr[t_lat:, :] = kc_ref[...].astype(BF16)
    vt_scr[0:hw, 0:t_lat] = vl_ref[...].astype(F32).T.astype(BF16)
    vt_scr[0:hw, t_lat:] = vc_ref[...].astype(F32).T.astype(BF16)
    vt_scr[hw:, :] = jnp.ones((ATTN_ONES_ROWS, t_all), BF16)

    lv = lam_ref[...]
    lam = (jnp.exp(jnp.sum(lv[0:1] * lv[1:2], axis=-1, keepdims=True))
           - jnp.exp(jnp.sum(lv[2:3] * lv[3:4], axis=-1, keepdims=True)) + lam_init)
    lane = lax.broadcasted_iota(jnp.int32, (tq, hw), 1)

    n_blocks = t_lat // tq
    slots = ((s0_scr, m0_scr), (s1_scr, m1_scr))

    def scores(i, slot):
        rows = pl.ds(pl.multiple_of(i * tq, tq), tq)
        q = (_rope(q_ref[rows, :].astype(F32), cos_ref[rows, :], sin_ref[rows, :])
             * (HEAD_DIM ** -0.5 * math.log2(math.e)))
        s_scr, m_scr = slots[slot]
        for half in range(2):
            qb = jnp.where((lane // HEAD_DIM) == half, q, 0.0).astype(BF16)
            s = _dot_nt(k_scr[...], qb)
            s_scr[half] = s
            m_scr[half] = jnp.max(s, axis=0, keepdims=True)

    def outputs(i, slot):
        rows = pl.ds(pl.multiple_of(i * tq, tq), tq)
        s_scr, m_scr = slots[slot]
        outs = []
        for half in range(2):
            p = jnp.exp2(s_scr[half] - m_scr[half]).astype(BF16)
            acc = _dot(vt_scr[...], p)
            outs.append(acc[0:hw, :] / acc[hw:hw + 1, :])
        o = (outs[0] - lam * outs[1]).T
        o_ref[rows, :] = (_rms(o) * dn_ref[...] * (1.0 - lam_init)).astype(o_ref.dtype)

    assert n_blocks % 2 == 0
    scores(0, 0)

    def body(j, carry):
        scores(2 * j + 1, 1)
        outputs(2 * j, 0)
        scores(2 * j + 2, 0)
        outputs(2 * j + 1, 1)
        return carry

    lax.fori_loop(0, n_blocks // 2 - 1, body, 0)
    scores(n_blocks - 1, 1)
    outputs(n_blocks - 2, 0)
    outputs(n_blocks - 1, 1)


def _diff_attn(z, lam_vecs, dnorm, lam_init, bn, t_lat, t_ctx, ctx_row0, cast=()):
    hw = 2 * HEAD_DIM
    q0 = 3 * CONV_WIDTH // hw
    k0 = q0 + DIFF_HEADS
    v0 = k0 + DIFF_HEADS
    tq = 256
    t_all = t_lat + t_ctx
    coff = ctx_row0 // t_ctx
    cos, sin = _rope_tables(t_lat)
    c_args, c_in, c_shape, c_out = _cast_riders(cast, lambda b, h: b * DIFF_HEADS + h)
    assert not cast or bn * DIFF_HEADS >= CAST_STEPS
    outs = pl.pallas_call(
        functools.partial(_attn_kernel, t_lat=t_lat, lam_init=lam_init, tq=tq, n_cast=len(cast)),
        out_shape=[jax.ShapeDtypeStruct((bn * t_lat, DIFF_HEADS * DIFF_DV), BF16)] + c_shape,
        grid=(bn, DIFF_HEADS),
        in_specs=[
            pl.BlockSpec((4, HEAD_DIM), lambda b, h: (0, 0)),
            pl.BlockSpec((t_lat, hw), lambda b, h: (b, q0 + h)),
            pl.BlockSpec((t_lat, hw), lambda b, h: (b, k0 + h)),
            pl.BlockSpec((t_lat, hw), lambda b, h: (b, v0 + h)),
            pl.BlockSpec((t_ctx, hw), lambda b, h: (coff + b, k0 + h)),
            pl.BlockSpec((t_ctx, hw), lambda b, h: (coff + b, v0 + h)),
            pl.BlockSpec((t_lat, hw), lambda b, h: (0, 0)),
            pl.BlockSpec((t_lat, hw), lambda b, h: (0, 0)),
            pl.BlockSpec((1, DIFF_DV), lambda b, h: (0, 0)),
        ] + c_in,
        out_specs=[pl.BlockSpec((t_lat, hw), lambda b, h: (b, h))] + c_out,
        scratch_shapes=[pltpu.VMEM((t_all, hw), BF16), pltpu.VMEM((hw + ATTN_ONES_ROWS, t_all), BF16),
                        pltpu.VMEM((2, t_all, tq), F32), pltpu.VMEM((2, 1, tq), F32),
                        pltpu.VMEM((2, t_all, tq), F32), pltpu.VMEM((2, 1, tq), F32)],
        compiler_params=_params("arbitrary", "arbitrary"),
        name="diff_attn",
    )(lam_vecs, z, z, z, z, z, cos, sin, dnorm.reshape(1, DIFF_DV), *c_args)
    return outs if cast else outs[0]


def kernel(x, c, ctx, c_ctx, ada_w, ada_b, norm_ffn1, norm_mix, norm_ffn2, ffn1_w_in, ffn1_w_out, ffn2_w_in,
           ffn2_w_out, mix_w_out, even_w_in, gla_gate_w, gla_gate_b, gla_norm, odd_w_in, conv_w, conv_b,
           lambda_q1, lambda_k1, lambda_q2, lambda_k2, diff_norm, final_norm):
    assert DEPTH == 2
    bn, t_lat, d = x.shape
    t_ctx = ctx.shape[1]
    assert bn < MOD_ROWS
    ctx_row = bn

    cond = jnp.concatenate([c, c_ctx[None, :], jnp.zeros((MOD_ROWS - bn - 1, d), F32)], axis=0)
    mods = _modulation(cond, ada_w, ada_b)

    tm = 512
    n_lat = bn * t_lat
    lat_blocks = n_lat // tm
    lat_row = lambda i: i // (t_lat // tm)
    all_row = lambda i: jnp.where(i < lat_blocks, i // (t_lat // tm), ctx_row)
    g1 = norm_ffn1.reshape(DEPTH, 1, d)
    gm = norm_mix.reshape(DEPTH, 1, d)
    g2 = norm_ffn2.reshape(DEPTH, 1, d)
    w1i0, w1o0 = ffn1_w_in[0].astype(BF16), ffn1_w_out[0].astype(BF16)
    w_even = even_w_in[0].astype(BF16)
    in_blocks, out_blocks = CAST_STEPS, CAST_STEPS // 2
    cast_a = ((ffn2_w_in, 0, in_blocks), (ffn2_w_out, 0, out_blocks), (mix_w_out, 0, in_blocks),
              (ffn1_w_in, 1, in_blocks), (ffn1_w_out, 1, out_blocks), (odd_w_in, 0, in_blocks))
    cast_b = ((ffn2_w_in, 1, in_blocks), (ffn2_w_out, 1, out_blocks), (mix_w_out, 1, in_blocks))

    proj0 = (gm, w_even[:, :EVEN_MAIN], w_even[:, EVEN_MAIN:])
    h, z, zg = _ffn((x.reshape(n_lat, d), ctx.reshape(bn * t_ctx, d)), mods, 0, 0, all_row, g1, w1i0, w1o0,
                    tm=tm, n_first=lat_blocks, proj=proj0)
    yf_l, w2i0, w2o0, wmix0, w1i1, w1o1, w_odd = _fourier(z, t_lat, bn, cast=cast_a)
    yf_c = _fourier(z, t_ctx, bn, row0=n_lat)
    yg_c, yg_l = _gla(z, zg, gla_gate_w[0], gla_gate_b[0], gla_norm[0], bn, t_ctx, t_lat, n_lat)
    h = _ffn(h, mods, 0, 6, all_row, g2, w2i0, w2o0, tm=tm, n_first=lat_blocks,
             mix=((yf_l, yf_c), (yg_l, yg_c), wmix0))

    h, z = _ffn(h, mods, 1, 0, all_row, g1, w1i1, w1o1, tm=tm, proj=(gm, w_odd, None))
    lam_init = 0.8 - 0.6 * math.exp(-0.3 * 1)
    lam_vecs = jnp.stack([lambda_q1[0], lambda_k1[0], lambda_q2[0], lambda_k2[0]]).astype(F32)
    y_conv = _short_conv(z, conv_w[0], conv_b[0], bn, t_lat)
    y_att, w2i1, w2o1, wmix1 = _diff_attn(z, lam_vecs, diff_norm[0], lam_init, bn, t_lat, t_ctx, n_lat, cast=cast_b)
    h = _ffn(h, mods, 1, 6, lat_row, g2, w2i1, w2o1, tm=tm, mix=(y_conv, y_att, wmix1),
             final_gain=final_norm.reshape(1, d), n_rows=n_lat)
    return h.reshape(bn, t_lat, d)
```

```python
import functools
import math

import numpy as np
import jax
import jax.numpy as jnp
from jax import lax
from jax.experimental import pallas as pl
from jax.experimental.pallas import tpu as pltpu

D_MODEL = 1024
DEPTH = 2
GRID_W = 64
HEAD_DIM = 64
N_MOD = 9
FFN_HIDDEN = 2816
NORM_EPS = 1e-6
FOURIER_GROUPS = 4
FOURIER_WIDTH = FOURIER_GROUPS * HEAD_DIM
GLA_HEADS = 6
GLA_DK = 64
GLA_DV = 128
GLA_GATE_RANK = 16
GLA_TAU = 16.0
GLA_CHUNK = 64
GLA_GROUP = 8
FFN_TILE = 256
FFN_CAST_STEPS = 32
PROJ_TILE = 512
CONV_WIDTH = 4 * HEAD_DIM
DIFF_HEADS = 6
DIFF_DV = 2 * HEAD_DIM
ROPE_THETA = 10000.0
ROPE_AXIS_DIM = HEAD_DIM // 2

EVEN_MAIN = FOURIER_WIDTH + 2 * GLA_HEADS * GLA_DK + 2 * GLA_HEADS * GLA_DV
ODD_IN = 3 * CONV_WIDTH + 3 * DIFF_HEADS * DIFF_DV

MOD_ROWS = 16
VMEM_LIMIT = 48 * 1024 * 1024

F32 = jnp.float32
BF16 = jnp.bfloat16


def _params(*sem):
    return pltpu.CompilerParams(dimension_semantics=sem, vmem_limit_bytes=VMEM_LIMIT)


def _sigmoid(x):
    return 1.0 / (1.0 + jnp.exp(-x))


def _dot(a, b):
    return jnp.dot(a, b, preferred_element_type=F32)


def _dot_nt(a, b):
    return lax.dot_general(a, b, (((1,), (1,)), ((), ())), preferred_element_type=F32)


def _dot_tn(a, b):
    return lax.dot_general(a, b, (((0,), (0,)), ((), ())), preferred_element_type=F32)


def _rms(x):
    return x * lax.rsqrt(jnp.mean(x * x, axis=-1, keepdims=True) + NORM_EPS)


def _mod_kernel(cond_ref, w_ref, b_ref, o_ref):
    c = cond_ref[...]
    s = c * _sigmoid(c)
    w = w_ref[...]
    w_hi = w.astype(BF16)
    w_lo = (w - w_hi.astype(F32)).astype(BF16)
    s_hi = s.astype(BF16)
    s_lo = (s - s_hi.astype(F32)).astype(BF16)
    o_ref[...] = _dot(s_hi, w_hi) + _dot(s_lo, w_hi) + _dot(s_hi, w_lo) + b_ref[...]


def _modulation(cond, ada_w, ada_b):
    n = N_MOD * D_MODEL
    tn = n // 8
    out = pl.pallas_call(
        _mod_kernel,
        out_shape=jax.ShapeDtypeStruct((DEPTH, MOD_ROWS, n), F32),
        grid=(DEPTH, n // tn),
        in_specs=[
            pl.BlockSpec((MOD_ROWS, D_MODEL), lambda l, j: (0, 0)),
            pl.BlockSpec((None, D_MODEL, tn), lambda l, j: (l, 0, j)),
            pl.BlockSpec((None, 1, tn), lambda l, j: (l, 0, j)),
        ],
        out_specs=pl.BlockSpec((None, MOD_ROWS, tn), lambda l, j: (l, 0, j)),
        compiler_params=_params("parallel", "parallel"),
        name="modulation",
    )(cond, ada_w, ada_b.reshape(DEPTH, 1, n))
    return out.reshape(DEPTH, MOD_ROWS, N_MOD, 1, D_MODEL)


def _mod_spec(layer, k, row_of_block):
    return pl.BlockSpec((None, None, None, 1, D_MODEL),
                        lambda i, *_: (layer, row_of_block(i), k, 0, 0))


def _gain_spec(layer):
    return pl.BlockSpec((None, 1, D_MODEL), lambda i, *_: (layer, 0, 0))


def _resident(block_shape, index_map):
    return pl.BlockSpec(block_shape, index_map, pipeline_mode=pl.Buffered(1))


def _cast_riders(sources, step_of, n_steps):
    args, in_specs, out_shape, out_specs = [], [], [], []
    for arr, lead, nb in sources:
        _, r, c = arr.shape
        rows, stride = r // nb, n_steps // nb
        assert rows * nb == r and stride * nb == n_steps and rows % 16 == 0
        idx = lambda *g, nb=nb, stride=stride: jnp.minimum(step_of(*g) // stride, nb - 1)
        args.append(arr)
        in_specs.append(pl.BlockSpec((None, rows, c), lambda *g, idx=idx, lead=lead: (lead, idx(*g), 0)))
        out_shape.append(jax.ShapeDtypeStruct((r, c), BF16))
        out_specs.append(pl.BlockSpec((rows, c), lambda *g, idx=idx: (idx(*g), 0)))
    return args, in_specs, out_shape, out_specs


def _cast_blocks(src_refs, dst_refs):
    for src, dst in zip(src_refs, dst_refs):
        dst[...] = src[...].astype(BF16)


def _ffn_kernel(*refs, with_mix, with_proj, with_gate, with_final, paired, n_first, n_cast):
    it = iter(refs)
    pairs = iter(paired)

    def rows_in():
        a = next(it)
        if not next(pairs):
            return a[...]
        b = next(it)
        return jnp.where(pl.program_id(0) < n_first, a[...], b[...])

    x = rows_in()
    if with_mix:
        ma, mb = rows_in(), rows_in()
        wm_ref, gm_ref = next(it), next(it)
    sh_ref, sc_ref, g_ref, gain_ref, wi_ref, wo_ref = (next(it) for _ in range(6))
    if with_proj:
        psh_ref, psc_ref, pgain_ref, wp_ref = (next(it) for _ in range(4))
    if with_gate:
        wgt_ref = next(it)
    if with_final:
        fn_ref = next(it)
    cast_src = [next(it) for _ in range(n_cast)]
    o_ref = next(it)
    if with_proj:
        z_ref = next(it)
    if with_gate:
        zg_ref = next(it)
    cast_dst = [next(it) for _ in range(n_cast)]
    a_scr = next(it)
    _cast_blocks(cast_src, cast_dst)

    if with_mix:
        ka = ma.shape[1]
        x = x + gm_ref[...] * (_dot(ma, wm_ref[0:ka, :]) + _dot(mb, wm_ref[ka:, :]))
    xn = (_rms(x) * gain_ref[...] * (1.0 + sc_ref[...]) + sh_ref[...]).astype(BF16)
    for j in range(FFN_HIDDEN // FFN_TILE):
        lo = j * FFN_TILE
        g = _dot(xn, wi_ref[:, lo:lo + FFN_TILE])
        u = _dot(xn, wi_ref[:, FFN_HIDDEN + lo:FFN_HIDDEN + lo + FFN_TILE])
        a_scr[:, lo:lo + FFN_TILE] = (g * _sigmoid(g) * u).astype(BF16)
    out = x + (0.5 * g_ref[...]) * _dot(a_scr[...], wo_ref[...])
    if with_final:
        out = _rms(out) * fn_ref[...]
    o_ref[...] = out
    if with_proj:
        xm = (_rms(out) * pgain_ref[...] * (1.0 + psc_ref[...]) + psh_ref[...]).astype(BF16)
        n_out = z_ref.shape[1]
        for lo in range(0, n_out, PROJ_TILE):
            z_ref[:, lo:lo + PROJ_TILE] = _dot(xm, wp_ref[:, lo:lo + PROJ_TILE]).astype(z_ref.dtype)
        if with_gate:
            zg_ref[...] = _dot(xm, wgt_ref[...])


def _ffn(h, mods, layer, mod_base, row_of_block, gain, w_in, w_out, *, tm, n_first=None, n_rows=None, mix=None,
         proj=None, final_gain=None, cast=()):
    m = n_rows or (sum(a.shape[0] for a in h) if isinstance(h, tuple) else h.shape[0])
    row = lambda i: (i, 0)
    args, specs, paired = [], [], []

    def add_rows(a):
        paired.append(isinstance(a, tuple))
        if paired[-1]:
            first, second = a
            assert first.shape[0] == n_first * tm
            args.extend([first, second])
            specs.extend([pl.BlockSpec((tm, first.shape[1]), lambda i: (jnp.minimum(i, n_first - 1), 0)),
                          pl.BlockSpec((tm, second.shape[1]), lambda i: (jnp.maximum(i - n_first, 0), 0))])
        else:
            args.append(a)
            specs.append(pl.BlockSpec((tm, a.shape[1]), row))

    add_rows(h)
    if mix is not None:
        ma, mb, w_mix = mix
        add_rows(ma)
        add_rows(mb)
        args += [w_mix, mods]
        specs += [_resident(w_mix.shape, lambda i: (0, 0)), _mod_spec(layer, 5, row_of_block)]
    args += [mods, mods, mods, gain, w_in, w_out]
    specs += [
        _mod_spec(layer, mod_base, row_of_block),
        _mod_spec(layer, mod_base + 1, row_of_block),
        _mod_spec(layer, mod_base + 2, row_of_block),
        _gain_spec(layer),
        _resident((D_MODEL, 2 * FFN_HIDDEN), lambda i: (0, 0)),
        _resident((FFN_HIDDEN, D_MODEL), lambda i: (0, 0)),
    ]
    out_shape = [jax.ShapeDtypeStruct((m, D_MODEL), F32)]
    out_specs = [pl.BlockSpec((tm, D_MODEL), row)]
    with_gate = False
    if proj is not None:
        pgain, wp, w_gate = proj
        n_out = wp.shape[1]
        assert n_out % PROJ_TILE == 0
        args += [mods, mods, pgain, wp]
        specs += [_mod_spec(layer, 3, row_of_block), _mod_spec(layer, 4, row_of_block), _gain_spec(layer),
                  _resident((D_MODEL, n_out), lambda i: (0, 0))]
        out_shape.append(jax.ShapeDtypeStruct((m, n_out), BF16))
        out_specs.append(pl.BlockSpec((tm, n_out), row))
        if w_gate is not None:
            with_gate = True
            ng = w_gate.shape[1]
            args.append(w_gate)
            specs.append(_resident((D_MODEL, ng), lambda i: (0, 0)))
            out_shape.append(jax.ShapeDtypeStruct((m, ng), F32))
            out_specs.append(pl.BlockSpec((tm, ng), row))
    if final_gain is not None:
        args.append(final_gain)
        specs.append(pl.BlockSpec((1, D_MODEL), lambda i: (0, 0)))
    c_args, c_in, c_shape, c_out = _cast_riders(cast, lambda i: i, FFN_CAST_STEPS)
    assert not cast or m // tm >= FFN_CAST_STEPS
    args += c_args
    specs += c_in
    out_shape += c_shape
    out_specs += c_out
    outs = pl.pallas_call(
        functools.partial(_ffn_kernel, with_mix=mix is not None, with_proj=proj is not None, with_gate=with_gate,
                          with_final=final_gain is not None, paired=tuple(paired), n_first=n_first,
                          n_cast=len(cast)),
        out_shape=out_shape,
        grid=(m // tm,),
        in_specs=specs,
        out_specs=out_specs,
        scratch_shapes=[pltpu.VMEM((tm, FFN_HIDDEN), BF16)],
        compiler_params=_params("arbitrary" if cast else "parallel"),
        name="ffn",
    )(*args)
    return outs if len(outs) > 1 else outs[0]


def _dft_tables(t):
    k = (np.arange(t, dtype=np.int64)[:, None] * np.arange(t, dtype=np.int64)[None, :]) % t
    ang = 2.0 * np.pi * k.astype(np.float64) / t
    pos = np.concatenate([np.cos(ang), -np.sin(ang)], axis=1).astype(np.float32)
    kc = (np.arange(HEAD_DIM)[:, None] * np.arange(HEAD_DIM)[None, :]) % HEAD_DIM
    angc = 2.0 * np.pi * kc.astype(np.float64) / HEAD_DIM
    eye = np.eye(FOURIER_GROUPS)
    chan = np.concatenate([np.kron(eye, np.cos(angc)), np.kron(eye, np.sin(angc))], axis=1).astype(np.float32)
    return pos, chan


def _chan_dft_kernel(z_ref, c_ref, o_ref, *, t):
    ab = _dot(z_ref[...].astype(BF16), c_ref[...].astype(BF16))
    o_ref[0:t, :] = ab[:, :FOURIER_WIDTH].astype(BF16)
    o_ref[t:2 * t, :] = ab[:, FOURIER_WIDTH:].astype(BF16)


def _pos_dft_kernel(p_ref, ab_ref, o_ref, p_scr, *, scale):
    @pl.when(pl.program_id(1) == 0)
    def _():
        p_scr[...] = p_ref[...].astype(BF16)

    o_ref[...] = (_dot(p_scr[...], ab_ref[...]) * scale).astype(BF16)


def _fourier(z, t, bn, row0=0):
    pos, chan = _dft_tables(t)
    off = row0 // t
    ab = pl.pallas_call(
        functools.partial(_chan_dft_kernel, t=t),
        out_shape=jax.ShapeDtypeStruct((bn, 2 * t, FOURIER_WIDTH), BF16),
        grid=(bn,),
        in_specs=[pl.BlockSpec((t, FOURIER_WIDTH), lambda b: (off + b, 0)),
                  pl.BlockSpec((FOURIER_WIDTH, 2 * FOURIER_WIDTH), lambda b: (0, 0))],
        out_specs=pl.BlockSpec((None, 2 * t, FOURIER_WIDTH), lambda b: (b, 0, 0)),
        compiler_params=_params("parallel"),
        name="chan_dft",
    )(z, jnp.asarray(chan))
    tq = min(t, 512)
    return pl.pallas_call(
        functools.partial(_pos_dft_kernel, scale=1.0 / math.sqrt(t * HEAD_DIM)),
        out_shape=jax.ShapeDtypeStruct((bn * t, FOURIER_WIDTH), BF16),
        grid=(t // tq, bn),
        in_specs=[pl.BlockSpec((tq, 2 * t), lambda i, b: (i, 0)),
                  pl.BlockSpec((None, 2 * t, FOURIER_WIDTH), lambda i, b: (b, 0, 0))],
        out_specs=pl.BlockSpec((tq, FOURIER_WIDTH), lambda i, b: (b * (t // tq) + i, 0)),
        scratch_shapes=[pltpu.VMEM((tq, 2 * t), BF16)],
        compiler_params=_params("parallel", "arbitrary"),
        name="pos_dft",
    )(jnp.asarray(pos), ab)


def _gla_segment(q_ref, k_ref, v_ref, r_ref, g_ref, y_ref, gw_ref, gb_ref, gn_ref,
                 b_scr, o_scr, s_scr, n_rows):
    c_len = GLA_CHUNK
    n = n_rows // c_len
    pair_k = 2 * GLA_DK
    pair_v = 2 * GLA_DV

    for d in range(2):
        zg = g_ref[:, d * GLA_GATE_RANK:(d + 1) * GLA_GATE_RANK]
        logit = _dot(zg.astype(BF16), gw_ref[d].astype(BF16)) + gb_ref[d]
        b_scr[d, 0:n_rows, :] = (jnp.minimum(logit, 0.0) - jnp.log(1.0 + jnp.exp(-jnp.abs(logit)))) * (1.0 / GLA_TAU)

    lane = lax.broadcasted_iota(jnp.int32, (c_len, pair_k), 1)
    col = lax.broadcasted_iota(jnp.int32, (c_len, pair_v), 1)
    ti = lax.broadcasted_iota(jnp.int32, (c_len, c_len), 0)
    tj = lax.broadcasted_iota(jnp.int32, (c_len, c_len), 1)
    blk = (lax.broadcasted_iota(jnp.int32, (pair_v, pair_k), 0) // GLA_DV
           == lax.broadcasted_iota(jnp.int32, (pair_v, pair_k), 1) // GLA_DK)

    group = math.gcd(GLA_GROUP, n)
    keep2 = [jnp.concatenate([m, m], axis=0) for m in (tj <= ti, tj >= ti)]
    ti2 = lax.broadcasted_iota(jnp.int32, (c_len, 2 * c_len), 0)
    tj2 = lax.broadcasted_iota(jnp.int32, (c_len, 2 * c_len), 1) & (c_len - 1)
    tri2 = [jnp.where(m, 1.0, 0.0).astype(BF16) for m in (tj2 <= ti2, tj2 >= ti2)]

    def body(i, carry):
        units = []
        for d in range(2):
            for g in range(group):
                c = i * group + g if d == 0 else n - 1 - (i * group + g)
                rows = pl.ds(pl.multiple_of(c * c_len, c_len), c_len)
                lg = b_scr[d, rows, :]
                hi = lg.astype(BF16)
                rest = (lg - hi.astype(F32)).astype(BF16)
                units.append(dict(d=d, rows=rows, b=_dot(tri2[d], jnp.concatenate([hi, rest], axis=0))))
        for u in units:
            d, rows, b = u["d"], u["rows"], u["b"]
            bl = b[c_len - 1:c_len, :] if d == 0 else b[0:1, :]
            k = k_ref[rows, :].astype(F32)
            v16 = v_ref[rows, :].astype(BF16)
            qd = q_ref[rows, :].astype(F32) * (GLA_DK ** -0.5) * jnp.exp(b)
            kd = (k * jnp.exp(-b)).astype(BF16)
            kdec = (k * jnp.exp(bl - b)).astype(BF16)
            q2 = jnp.concatenate([jnp.where((lane // GLA_DK) == hh, qd, 0.0) for hh in range(2)], axis=0)
            att = _dot_nt(q2.astype(BF16), kd)
            upd = _dot_tn(v16, kdec)
            u.update(qd=qd.astype(BF16), v16=v16, att=att, upd=upd, dec=jnp.exp(bl))
        for d in range(2):
            s = s_scr[d]
            for u in units:
                if u["d"] == d:
                    u["o"] = _dot_nt(u["qd"], s.astype(BF16))
                    s = jnp.where(blk, s * u["dec"] + u["upd"], 0.0)
            s_scr[d] = s
        for u in units:
            att = jnp.where(keep2[u["d"]], u["att"], 0.0).astype(BF16)
            pv = _dot(att, u["v16"])
            intra = jnp.where((col // GLA_DV) == 0, pv[0:c_len, :], pv[c_len:, :])
            o_scr[u["d"], u["rows"], :] = u["o"] + intra
        return carry

    lax.fori_loop(0, n // group, body, 0)

    gn = gn_ref[...]
    for hh in range(2):
        sl = slice(hh * GLA_DV, (hh + 1) * GLA_DV)
        o = o_scr[0, 0:n_rows, sl] + o_scr[1, 0:n_rows, sl]
        rr = r_ref[:, sl].astype(F32)
        y_ref[:, sl] = (_rms(o) * gn * (rr * _sigmoid(rr))).astype(y_ref.dtype)


def _gla_kernel(*refs, t_ctx, t_lat, n_cast):
    qc_ref, kc_ref, vc_ref, rc_ref, gc_ref, ql_ref, kl_ref, vl_ref, rl_ref, gl_ref, gw_ref, gb_ref, gn_ref = refs[:13]
    yc_ref, yl_ref = refs[13 + n_cast:15 + n_cast]
    b_scr, o_scr, s_scr = refs[-3:]
    _cast_blocks(refs[13:13 + n_cast], refs[15 + n_cast:15 + 2 * n_cast])
    s_scr[...] = jnp.zeros_like(s_scr)
    _gla_segment(qc_ref, kc_ref, vc_ref, rc_ref, gc_ref, yc_ref, gw_ref, gb_ref, gn_ref,
                 b_scr, o_scr, s_scr, t_ctx)
    _gla_segment(ql_ref, kl_ref, vl_ref, rl_ref, gl_ref, yl_ref, gw_ref, gb_ref, gn_ref,
                 b_scr, o_scr, s_scr, t_lat)


def _gla(z, zg, gate_w, gate_b, gla_g, bn, t_ctx, t_lat, ctx_row0, cast=()):
    pk, pv = 2 * GLA_DK, 2 * GLA_DV
    q0 = FOURIER_WIDTH // pk
    k0 = (FOURIER_WIDTH + GLA_HEADS * GLA_DK) // pk
    v0 = (FOURIER_WIDTH + 2 * GLA_HEADS * GLA_DK) // pv
    r0 = (FOURIER_WIDTH + 2 * GLA_HEADS * GLA_DK + GLA_HEADS * GLA_DV) // pv

    def seg_specs(t, row0):
        off = row0 // t
        return [pl.BlockSpec((t, pk), lambda b, p: (off + b, q0 + p)),
                pl.BlockSpec((t, pk), lambda b, p: (off + b, k0 + p)),
                pl.BlockSpec((t, pv), lambda b, p: (off + b, v0 + p)),
                pl.BlockSpec((t, pv), lambda b, p: (off + b, r0 + p)),
                pl.BlockSpec((t, 2 * GLA_GATE_RANK), lambda b, p: (off + b, 0))]

    wdt = GLA_HEADS * GLA_DV
    c_args, c_in, c_shape, c_out = _cast_riders(cast, lambda b, p: b, bn)
    return pl.pallas_call(
        functools.partial(_gla_kernel, t_ctx=t_ctx, t_lat=t_lat, n_cast=len(cast)),
        out_shape=[jax.ShapeDtypeStruct((bn * t_ctx, wdt), BF16), jax.ShapeDtypeStruct((bn * t_lat, wdt), BF16)] + c_shape,
        grid=(bn, GLA_HEADS // 2),
        in_specs=seg_specs(t_ctx, ctx_row0) + seg_specs(t_lat, 0) + [
            pl.BlockSpec((2, GLA_GATE_RANK, pk), lambda b, p: (0, 0, p)),
            pl.BlockSpec((2, 1, pk), lambda b, p: (0, 0, p)),
            pl.BlockSpec((1, GLA_DV), lambda b, p: (0, 0))] + c_in,
        out_specs=[pl.BlockSpec((t_ctx, pv), lambda b, p: (b, p)),
                   pl.BlockSpec((t_lat, pv), lambda b, p: (b, p))] + c_out,
        scratch_shapes=[pltpu.VMEM((2, t_lat, pk), F32), pltpu.VMEM((2, t_lat, pv), F32),
                        pltpu.VMEM((2, pv, pk), F32)],
        compiler_params=_params("arbitrary", "arbitrary"),
        name="gla",
    )(z, z, z, z, zg, z, z, z, z, zg, gate_w, gate_b.reshape(2, 1, GLA_HEADS * GLA_DK),
      gla_g.reshape(1, GLA_DV), *c_args)


def _conv_kernel(zb_ref, zc_ref, zx_ref, w_ref, b_ref, o_ref, *, t):
    u = zc_ref[...].astype(F32) * zx_ref[...].astype(F32)
    row = lax.broadcasted_iota(jnp.int32, u.shape, 0)
    prev = jnp.where(row >= 1, pltpu.roll(u, 1, 0), 0.0)
    nxt = jnp.where(row < t - 1, pltpu.roll(u, t - 1, 0), 0.0)
    y = prev * w_ref[0:1, :] + u * w_ref[1:2, :] + nxt * w_ref[2:3, :] + b_ref[...]
    o_ref[...] = (zb_ref[...].astype(F32) * y).astype(o_ref.dtype)


def _short_conv(z, conv_w, conv_b, bn, t):
    cw = CONV_WIDTH
    return pl.pallas_call(
        functools.partial(_conv_kernel, t=t),
        out_shape=jax.ShapeDtypeStruct((bn * t, cw), BF16),
        grid=(bn,),
        in_specs=[pl.BlockSpec((t, cw), lambda b: (b, 0)),
                  pl.BlockSpec((t, cw), lambda b: (b, 1)),
                  pl.BlockSpec((t, cw), lambda b: (b, 2)),
                  pl.BlockSpec((3, cw), lambda b: (0, 0)),
                  pl.BlockSpec((1, cw), lambda b: (0, 0))],
        out_specs=pl.BlockSpec((t, cw), lambda b: (b, 0)),
        compiler_params=_params("parallel"),
        name="short_conv",
    )(z, z, z, conv_w, conv_b.reshape(1, cw))


def _rope_tables(t):
    rows = t // GRID_W
    row = jnp.repeat(jnp.arange(rows), GRID_W).astype(F32)
    col = jnp.tile(jnp.arange(GRID_W), rows).astype(F32)
    n = ROPE_AXIS_DIM // 2
    inv = ROPE_THETA ** (-jnp.arange(n, dtype=F32) / n)
    ar, ac = row[:, None] * inv, col[:, None] * inv
    ang = jnp.concatenate([ar, ar, ac, ac], axis=-1)
    sign = jnp.tile(jnp.concatenate([-jnp.ones((n,), F32), jnp.ones((n,), F32)]), 2)
    cos = jnp.cos(ang)
    sin = jnp.sin(ang) * sign
    return jnp.tile(cos, (1, 2)), jnp.tile(sin, (1, 2))


def _rope(x, cos, sin):
    lane = lax.broadcasted_iota(jnp.int32, x.shape, 1)
    n = ROPE_AXIS_DIM // 2
    w = x.shape[1]
    partner = jnp.where((lane & (2 * n - 1)) < n, pltpu.roll(x, w - n, 1), pltpu.roll(x, n, 1))
    return x * cos + partner * sin


ATTN_ONES_ROWS = 16
ATTN_CAST_STEPS = 32


def _attn_kernel(*refs, t_lat, lam_init, tq, n_cast):
    lam_ref, q_ref, kl_ref, vl_ref, kc_ref, vc_ref, cos_ref, sin_ref, dn_ref = refs[:9]
    o_ref = refs[9 + n_cast]
    k_scr, vt_scr, s0_scr, m0_scr, s1_scr, m1_scr = refs[-6:]
    _cast_blocks(refs[9:9 + n_cast], refs[10 + n_cast:10 + 2 * n_cast])
    hw = 2 * HEAD_DIM
    t_all = k_scr.shape[0]

    k_scr[0:t_lat, :] = _rope(kl_ref[...].astype(F32), cos_ref[...], sin_ref[...]).astype(BF16)
    k_scr[t_lat:, :] = kc_ref[...].astype(BF16)
    vt_scr[0:hw, 0:t_lat] = vl_ref[...].astype(F32).T.astype(BF16)
    vt_scr[0:hw, t_lat:] = vc_ref[...].astype(F32).T.astype(BF16)
    vt_scr[hw:, :] = jnp.ones((ATTN_ONES_ROWS, t_all), BF16)

    lv = lam_ref[...]
    lam = (jnp.exp(jnp.sum(lv[0:1] * lv[1:2], axis=-1, keepdims=True))
           - jnp.exp(jnp.sum(lv[2:3] * lv[3:4], axis=-1, keepdims=True)) + lam_init)
    lane = lax.broadcasted_iota(jnp.int32, (tq, hw), 1)

    n_blocks = t_lat // tq
    slots = ((s0_scr, m0_scr), (s1_scr, m1_scr))

    def scores(i, slot):
        rows = pl.ds(pl.multiple_of(i * tq, tq), tq)
        q = (_rope(q_ref[rows, :].astype(F32), cos_ref[rows, :], sin_ref[rows, :])
             * (HEAD_DIM ** -0.5 * math.log2(math.e)))
        s_scr, m_scr = slots[slot]
        for half in range(2):
            qb = jnp.where((lane // HEAD_DIM) == half, q, 0.0).astype(BF16)
            s = _dot_nt(k_scr[...], qb)
            s_scr[half] = s
            m_scr[half] = jnp.max(s, axis=0, keepdims=True)

    def outputs(i, slot):
        rows = pl.ds(pl.multiple_of(i * tq, tq), tq)
        s_scr, m_scr = slots[slot]
        outs = []
        for half in range(2):
            p = jnp.exp2(s_scr[half] - m_scr[half]).astype(BF16)
            acc = _dot(vt_scr[...], p)
            outs.append(acc[0:hw, :] / acc[hw:hw + 1, :])
        o = (outs[0] - lam * outs[1]).T
        o_ref[rows, :] = (_rms(o) * dn_ref[...] * (1.0 - lam_init)).astype(o_ref.dtype)

    assert n_blocks % 2 == 0
    scores(0, 0)

    def body(j, carry):
        scores(2 * j + 1, 1)
        outputs(2 * j, 0)
        scores(2 * j + 2, 0)
        outputs(2 * j + 1, 1)
        return carry

    lax.fori_loop(0, n_blocks // 2 - 1, body, 0)
    scores(n_blocks - 1, 1)
    outputs(n_blocks - 2, 0)
    outputs(n_blocks - 1, 1)


def _diff_attn(z, lam_vecs, dnorm, lam_init, bn, t_lat, t_ctx, ctx_row0, cast=()):
    hw = 2 * HEAD_DIM
    q0 = 3 * CONV_WIDTH // hw
    k0 = q0 + DIFF_HEADS
    v0 = k0 + DIFF_HEADS
    tq = 256
    t_all = t_lat + t_ctx
    coff = ctx_row0 // t_ctx
    cos, sin = _rope_tables(t_lat)
    c_args, c_in, c_shape, c_out = _cast_riders(cast, lambda b, h: b * DIFF_HEADS + h, ATTN_CAST_STEPS)
    assert not cast or bn * DIFF_HEADS >= ATTN_CAST_STEPS
    outs = pl.pallas_call(
        functools.partial(_attn_kernel, t_lat=t_lat, lam_init=lam_init, tq=tq, n_cast=len(cast)),
        out_shape=[jax.ShapeDtypeStruct((bn * t_lat, DIFF_HEADS * DIFF_DV), BF16)] + c_shape,
        grid=(bn, DIFF_HEADS),
        in_specs=[
            pl.BlockSpec((4, HEAD_DIM), lambda b, h: (0, 0)),
            pl.BlockSpec((t_lat, hw), lambda b, h: (b, q0 + h)),
            pl.BlockSpec((t_lat, hw), lambda b, h: (b, k0 + h)),
            pl.BlockSpec((t_lat, hw), lambda b, h: (b, v0 + h)),
            pl.BlockSpec((t_ctx, hw), lambda b, h: (coff + b, k0 + h)),
            pl.BlockSpec((t_ctx, hw), lambda b, h: (coff + b, v0 + h)),
            pl.BlockSpec((t_lat, hw), lambda b, h: (0, 0)),
            pl.BlockSpec((t_lat, hw), lambda b, h: (0, 0)),
            pl.BlockSpec((1, DIFF_DV), lambda b, h: (0, 0)),
        ] + c_in,
        out_specs=[pl.BlockSpec((t_lat, hw), lambda b, h: (b, h))] + c_out,
        scratch_shapes=[pltpu.VMEM((t_all, hw), BF16), pltpu.VMEM((hw + ATTN_ONES_ROWS, t_all), BF16),
                        pltpu.VMEM((2, t_all, tq), F32), pltpu.VMEM((2, 1, tq), F32),
                        pltpu.VMEM((2, t_all, tq), F32), pltpu.VMEM((2, 1, tq), F32)],
        compiler_params=_params("arbitrary", "arbitrary"),
        name="diff_attn",
    )(lam_vecs, z, z, z, z, z, cos, sin, dnorm.reshape(1, DIFF_DV), *c_args)
    return outs if cast else outs[0]


def kernel(x, c, ctx, c_ctx, ada_w, ada_b, norm_ffn1, norm_mix, norm_ffn2, ffn1_w_in, ffn1_w_out, ffn2_w_in,
           ffn2_w_out, mix_w_out, even_w_in, gla_gate_w, gla_gate_b, gla_norm, odd_w_in, conv_w, conv_b,
           lambda_q1, lambda_k1, lambda_q2, lambda_k2, diff_norm, final_norm):
    assert DEPTH == 2
    bn, t_lat, d = x.shape
    t_ctx = ctx.shape[1]
    assert bn < MOD_ROWS
    ctx_row = bn

    cond = jnp.concatenate([c, c_ctx[None, :], jnp.zeros((MOD_ROWS - bn - 1, d), F32)], axis=0)
    mods = _modulation(cond, ada_w, ada_b)

    tm = 512
    n_lat = bn * t_lat
    lat_blocks = n_lat // tm
    lat_row = lambda i: i // (t_lat // tm)
    all_row = lambda i: jnp.where(i < lat_blocks, i // (t_lat // tm), ctx_row)
    g1 = norm_ffn1.reshape(DEPTH, 1, d)
    gm = norm_mix.reshape(DEPTH, 1, d)
    g2 = norm_ffn2.reshape(DEPTH, 1, d)
    w1i0, w1o0 = ffn1_w_in[0].astype(BF16), ffn1_w_out[0].astype(BF16)
    w_even = even_w_in[0].astype(BF16)
    cast_gla = ((ffn2_w_in, 0, bn), (ffn2_w_out, 0, bn), (mix_w_out, 0, bn))
    cast_ffn = ((ffn1_w_in, 1, FFN_CAST_STEPS), (ffn1_w_out, 1, FFN_CAST_STEPS // 2), (odd_w_in, 0, FFN_CAST_STEPS))
    cast_attn = ((ffn2_w_in, 1, ATTN_CAST_STEPS), (ffn2_w_out, 1, ATTN_CAST_STEPS // 2),
                 (mix_w_out, 1, ATTN_CAST_STEPS))

    proj0 = (gm, w_even[:, :EVEN_MAIN], w_even[:, EVEN_MAIN:])
    h, z, zg = _ffn((x.reshape(n_lat, d), ctx.reshape(bn * t_ctx, d)), mods, 0, 0, all_row, g1, w1i0, w1o0,
                    tm=tm, n_first=lat_blocks, proj=proj0)
    yf_l = _fourier(z, t_lat, bn)
    yf_c = _fourier(z, t_ctx, bn, row0=n_lat)
    yg_c, yg_l, w2i0, w2o0, wmix0 = _gla(z, zg, gla_gate_w[0], gla_gate_b[0], gla_norm[0], bn, t_ctx, t_lat, n_lat,
                                         cast=cast_gla)
    h, w1i1, w1o1, w_odd = _ffn(h, mods, 0, 6, all_row, g2, w2i0, w2o0, tm=tm, n_first=lat_blocks,
                                mix=((yf_l, yf_c), (yg_l, yg_c), wmix0), cast=cast_ffn)

    h, z = _ffn(h, mods, 1, 0, all_row, g1, w1i1, w1o1, tm=tm, proj=(gm, w_odd, None))
    lam_init = 0.8 - 0.6 * math.exp(-0.3 * 1)
    lam_vecs = jnp.stack([lambda_q1[0], lambda_k1[0], lambda_q2[0], lambda_k2[0]]).astype(F32)
    y_conv = _short_conv(z, conv_w[0], conv_b[0], bn, t_lat)
    y_att, w2i1, w2o1, wmix1 = _diff_attn(z, lam_vecs, diff_norm[0], lam_init, bn, t_lat, t_ctx, n_lat,
                                          cast=cast_attn)
    h = _ffn(h, mods, 1, 6, lat_row, g2, w2i1, w2o1, tm=tm, mix=(y_conv, y_att, wmix1),
             final_gain=final_norm.reshape(1, d), n_rows=n_lat)
    return h.reshape(bn, t_lat, d)
```

```python
import functools
import math

import numpy as np
import jax
import jax.numpy as jnp
from jax import lax
from jax.experimental import pallas as pl
from jax.experimental.pallas import tpu as pltpu

D_MODEL = 1024
DEPTH = 2
GRID_W = 64
HEAD_DIM = 64
N_MOD = 9
FFN_HIDDEN = 2816
NORM_EPS = 1e-6
FOURIER_GROUPS = 4
FOURIER_WIDTH = FOURIER_GROUPS * HEAD_DIM
GLA_HEADS = 6
GLA_DK = 64
GLA_DV = 128
GLA_GATE_RANK = 16
GLA_TAU = 16.0
GLA_CHUNK = 64
GLA_GROUP = 8
FFN_TILE = 256
FFN_CAST_STEPS = 32
PROJ_TILE = 512
CONV_WIDTH = 4 * HEAD_DIM
DIFF_HEADS = 6
DIFF_DV = 2 * HEAD_DIM
ROPE_THETA = 10000.0
ROPE_AXIS_DIM = HEAD_DIM // 2

EVEN_MAIN = FOURIER_WIDTH + 2 * GLA_HEADS * GLA_DK + 2 * GLA_HEADS * GLA_DV
ODD_IN = 3 * CONV_WIDTH + 3 * DIFF_HEADS * DIFF_DV

MOD_ROWS = 16
VMEM_LIMIT = 48 * 1024 * 1024

F32 = jnp.float32
BF16 = jnp.bfloat16


def _params(*sem):
    return pltpu.CompilerParams(dimension_semantics=sem, vmem_limit_bytes=VMEM_LIMIT)


def _sigmoid(x):
    return 1.0 / (1.0 + jnp.exp(-x))


def _dot(a, b):
    return jnp.dot(a, b, preferred_element_type=F32)


def _dot_nt(a, b):
    return lax.dot_general(a, b, (((1,), (1,)), ((), ())), preferred_element_type=F32)


def _dot_tn(a, b):
    return lax.dot_general(a, b, (((0,), (0,)), ((), ())), preferred_element_type=F32)


def _rms(x):
    return x * lax.rsqrt(jnp.mean(x * x, axis=-1, keepdims=True) + NORM_EPS)


def _mod_kernel(cond_ref, w_ref, b_ref, o_ref):
    c = cond_ref[...]
    s = c * _sigmoid(c)
    w = w_ref[...]
    w_hi = w.astype(BF16)
    w_lo = (w - w_hi.astype(F32)).astype(BF16)
    s_hi = s.astype(BF16)
    s_lo = (s - s_hi.astype(F32)).astype(BF16)
    o_ref[...] = _dot(s_hi, w_hi) + _dot(s_lo, w_hi) + _dot(s_hi, w_lo) + b_ref[...]


def _modulation(cond, ada_w, ada_b):
    n = N_MOD * D_MODEL
    tn = n // 8
    out = pl.pallas_call(
        _mod_kernel,
        out_shape=jax.ShapeDtypeStruct((DEPTH, MOD_ROWS, n), F32),
        grid=(DEPTH, n // tn),
        in_specs=[
            pl.BlockSpec((MOD_ROWS, D_MODEL), lambda l, j: (0, 0)),
            pl.BlockSpec((None, D_MODEL, tn), lambda l, j: (l, 0, j)),
            pl.BlockSpec((None, 1, tn), lambda l, j: (l, 0, j)),
        ],
        out_specs=pl.BlockSpec((None, MOD_ROWS, tn), lambda l, j: (l, 0, j)),
        compiler_params=_params("parallel", "parallel"),
        name="modulation",
    )(cond, ada_w, ada_b.reshape(DEPTH, 1, n))
    return out.reshape(DEPTH, MOD_ROWS, N_MOD, 1, D_MODEL)


def _mod_spec(layer, k, row_of_block):
    return pl.BlockSpec((None, None, None, 1, D_MODEL),
                        lambda i, *_: (layer, row_of_block(i), k, 0, 0))


def _gain_spec(layer):
    return pl.BlockSpec((None, 1, D_MODEL), lambda i, *_: (layer, 0, 0))


def _resident(block_shape, index_map):
    return pl.BlockSpec(block_shape, index_map, pipeline_mode=pl.Buffered(1))


def _cast_riders(sources, step_of, n_steps):
    args, in_specs, out_shape, out_specs = [], [], [], []
    for arr, lead, nb in sources:
        _, r, c = arr.shape
        rows, stride = r // nb, n_steps // nb
        assert rows * nb == r and stride * nb == n_steps and rows % 16 == 0
        idx = lambda *g, nb=nb, stride=stride: jnp.minimum(step_of(*g) // stride, nb - 1)
        args.append(arr)
        in_specs.append(pl.BlockSpec((None, rows, c), lambda *g, idx=idx, lead=lead: (lead, idx(*g), 0)))
        out_shape.append(jax.ShapeDtypeStruct((r, c), BF16))
        out_specs.append(pl.BlockSpec((rows, c), lambda *g, idx=idx: (idx(*g), 0)))
    return args, in_specs, out_shape, out_specs


def _cast_blocks(src_refs, dst_refs):
    for src, dst in zip(src_refs, dst_refs):
        dst[...] = src[...].astype(BF16)


def _ffn_kernel(*refs, with_mix, with_proj, with_gate, with_final, paired, n_first, n_cast):
    it = iter(refs)
    pairs = iter(paired)

    def rows_in():
        a = next(it)
        if not next(pairs):
            return a[...]
        b = next(it)
        return jnp.where(pl.program_id(0) < n_first, a[...], b[...])

    x = rows_in()
    if with_mix:
        ma, mb = rows_in(), rows_in()
        wm_ref, gm_ref = next(it), next(it)
    sh_ref, sc_ref, g_ref, gain_ref, wi_ref, wo_ref = (next(it) for _ in range(6))
    if with_proj:
        psh_ref, psc_ref, pgain_ref, wp_ref = (next(it) for _ in range(4))
    if with_final:
        fn_ref = next(it)
    cast_src = [next(it) for _ in range(n_cast)]
    o_ref = next(it)
    if with_proj:
        z_ref = next(it)
    if with_gate:
        zg_ref = next(it)
    cast_dst = [next(it) for _ in range(n_cast)]
    a_scr = next(it)
    _cast_blocks(cast_src, cast_dst)

    if with_mix:
        ka = ma.shape[1]
        x = x + gm_ref[...] * (_dot(ma, wm_ref[0:ka, :]) + _dot(mb, wm_ref[ka:, :]))
    xn = (_rms(x) * gain_ref[...] * (1.0 + sc_ref[...]) + sh_ref[...]).astype(BF16)
    for j in range(FFN_HIDDEN // FFN_TILE):
        lo = j * FFN_TILE
        g = _dot(xn, wi_ref[:, lo:lo + FFN_TILE])
        u = _dot(xn, wi_ref[:, FFN_HIDDEN + lo:FFN_HIDDEN + lo + FFN_TILE])
        a_scr[:, lo:lo + FFN_TILE] = (g * _sigmoid(g) * u).astype(BF16)
    out = x + (0.5 * g_ref[...]) * _dot(a_scr[...], wo_ref[...])
    if with_final:
        out = _rms(out) * fn_ref[...]
    o_ref[...] = out
    if with_proj:
        xm = (_rms(out) * pgain_ref[...] * (1.0 + psc_ref[...]) + psh_ref[...]).astype(BF16)
        n_out = z_ref.shape[1]
        for lo in range(0, n_out, PROJ_TILE):
            z_ref[:, lo:lo + PROJ_TILE] = _dot(xm, wp_ref[:, lo:lo + PROJ_TILE]).astype(z_ref.dtype)
        if with_gate:
            zg_ref[...] = _dot(xm, wp_ref[:, n_out:])


def _ffn(h, mods, layer, mod_base, row_of_block, gain, w_in, w_out, *, tm, n_first=None, n_rows=None, mix=None,
         proj=None, final_gain=None, cast=()):
    m = n_rows or (sum(a.shape[0] for a in h) if isinstance(h, tuple) else h.shape[0])
    row = lambda i: (i, 0)
    args, specs, paired = [], [], []

    def add_rows(a):
        paired.append(isinstance(a, tuple))
        if paired[-1]:
            first, second = a
            assert first.shape[0] == n_first * tm
            args.extend([first, second])
            specs.extend([pl.BlockSpec((tm, first.shape[1]), lambda i: (jnp.minimum(i, n_first - 1), 0)),
                          pl.BlockSpec((tm, second.shape[1]), lambda i: (jnp.maximum(i - n_first, 0), 0))])
        else:
            args.append(a)
            specs.append(pl.BlockSpec((tm, a.shape[1]), row))

    add_rows(h)
    if mix is not None:
        ma, mb, w_mix = mix
        add_rows(ma)
        add_rows(mb)
        args += [w_mix, mods]
        specs += [_resident(w_mix.shape, lambda i: (0, 0)), _mod_spec(layer, 5, row_of_block)]
    args += [mods, mods, mods, gain, w_in, w_out]
    specs += [
        _mod_spec(layer, mod_base, row_of_block),
        _mod_spec(layer, mod_base + 1, row_of_block),
        _mod_spec(layer, mod_base + 2, row_of_block),
        _gain_spec(layer),
        _resident((D_MODEL, 2 * FFN_HIDDEN), lambda i: (0, 0)),
        _resident((FFN_HIDDEN, D_MODEL), lambda i: (0, 0)),
    ]
    out_shape = [jax.ShapeDtypeStruct((m, D_MODEL), F32)]
    out_specs = [pl.BlockSpec((tm, D_MODEL), row)]
    n_gate = 0
    if proj is not None:
        pgain, wp, n_gate = proj
        n_out = wp.shape[1] - n_gate
        assert n_out % PROJ_TILE == 0
        args += [mods, mods, pgain, wp]
        specs += [_mod_spec(layer, 3, row_of_block), _mod_spec(layer, 4, row_of_block), _gain_spec(layer),
                  _resident(wp.shape, lambda i: (0, 0))]
        out_shape.append(jax.ShapeDtypeStruct((m, n_out), BF16))
        out_specs.append(pl.BlockSpec((tm, n_out), row))
        if n_gate:
            out_shape.append(jax.ShapeDtypeStruct((m, n_gate), F32))
            out_specs.append(pl.BlockSpec((tm, n_gate), row))
    if final_gain is not None:
        args.append(final_gain)
        specs.append(pl.BlockSpec((1, D_MODEL), lambda i: (0, 0)))
    c_args, c_in, c_shape, c_out = _cast_riders(cast, lambda i: i, FFN_CAST_STEPS)
    assert not cast or m // tm >= FFN_CAST_STEPS
    args += c_args
    specs += c_in
    out_shape += c_shape
    out_specs += c_out
    outs = pl.pallas_call(
        functools.partial(_ffn_kernel, with_mix=mix is not None, with_proj=proj is not None, with_gate=n_gate > 0,
                          with_final=final_gain is not None, paired=tuple(paired), n_first=n_first,
                          n_cast=len(cast)),
        out_shape=out_shape,
        grid=(m // tm,),
        in_specs=specs,
        out_specs=out_specs,
        scratch_shapes=[pltpu.VMEM((tm, FFN_HIDDEN), BF16)],
        compiler_params=_params("arbitrary" if cast else "parallel"),
        name="ffn",
    )(*args)
    return outs if len(outs) > 1 else outs[0]


def _dft_tables(t):
    k = (np.arange(t, dtype=np.int64)[:, None] * np.arange(t, dtype=np.int64)[None, :]) % t
    ang = 2.0 * np.pi * k.astype(np.float64) / t
    pos = np.concatenate([np.cos(ang), -np.sin(ang)], axis=1).astype(np.float32)
    kc = (np.arange(HEAD_DIM)[:, None] * np.arange(HEAD_DIM)[None, :]) % HEAD_DIM
    angc = 2.0 * np.pi * kc.astype(np.float64) / HEAD_DIM
    eye = np.eye(FOURIER_GROUPS)
    chan = np.concatenate([np.kron(eye, np.cos(angc)), np.kron(eye, np.sin(angc))], axis=1).astype(np.float32)
    return pos, chan


def _chan_dft_kernel(z_ref, c_ref, o_ref, *, t):
    ab = _dot(z_ref[...].astype(BF16), c_ref[...].astype(BF16))
    o_ref[0:t, :] = ab[:, :FOURIER_WIDTH].astype(BF16)
    o_ref[t:2 * t, :] = ab[:, FOURIER_WIDTH:].astype(BF16)


def _pos_dft_kernel(p_ref, ab_ref, o_ref, p_scr, *, scale):
    @pl.when(pl.program_id(1) == 0)
    def _():
        p_scr[...] = p_ref[...].astype(BF16)

    o_ref[...] = (_dot(p_scr[...], ab_ref[...]) * scale).astype(BF16)


def _fourier(z, t, bn, row0=0):
    pos, chan = _dft_tables(t)
    off = row0 // t
    ab = pl.pallas_call(
        functools.partial(_chan_dft_kernel, t=t),
        out_shape=jax.ShapeDtypeStruct((bn, 2 * t, FOURIER_WIDTH), BF16),
        grid=(bn,),
        in_specs=[pl.BlockSpec((t, FOURIER_WIDTH), lambda b: (off + b, 0)),
                  pl.BlockSpec((FOURIER_WIDTH, 2 * FOURIER_WIDTH), lambda b: (0, 0))],
        out_specs=pl.BlockSpec((None, 2 * t, FOURIER_WIDTH), lambda b: (b, 0, 0)),
        compiler_params=_params("parallel"),
        name="chan_dft",
    )(z, jnp.asarray(chan))
    tq = min(t, 512)
    return pl.pallas_call(
        functools.partial(_pos_dft_kernel, scale=1.0 / math.sqrt(t * HEAD_DIM)),
        out_shape=jax.ShapeDtypeStruct((bn * t, FOURIER_WIDTH), BF16),
        grid=(t // tq, bn),
        in_specs=[pl.BlockSpec((tq, 2 * t), lambda i, b: (i, 0)),
                  pl.BlockSpec((None, 2 * t, FOURIER_WIDTH), lambda i, b: (b, 0, 0))],
        out_specs=pl.BlockSpec((tq, FOURIER_WIDTH), lambda i, b: (b * (t // tq) + i, 0)),
        scratch_shapes=[pltpu.VMEM((tq, 2 * t), BF16)],
        compiler_params=_params("parallel", "arbitrary"),
        name="pos_dft",
    )(jnp.asarray(pos), ab)


def _gla_segment(q_ref, k_ref, v_ref, r_ref, g_ref, y_ref, gw_ref, gb_ref, gn_ref,
                 b_scr, o_scr, s_scr, n_rows):
    c_len = GLA_CHUNK
    n = n_rows // c_len
    pair_k = 2 * GLA_DK
    pair_v = 2 * GLA_DV

    for d in range(2):
        zg = g_ref[:, d * GLA_GATE_RANK:(d + 1) * GLA_GATE_RANK]
        logit = _dot(zg.astype(BF16), gw_ref[d].astype(BF16)) + gb_ref[d]
        b_scr[d, 0:n_rows, :] = (jnp.minimum(logit, 0.0) - jnp.log(1.0 + jnp.exp(-jnp.abs(logit)))) * (1.0 / GLA_TAU)

    lane = lax.broadcasted_iota(jnp.int32, (c_len, pair_k), 1)
    col = lax.broadcasted_iota(jnp.int32, (c_len, pair_v), 1)
    ti = lax.broadcasted_iota(jnp.int32, (c_len, c_len), 0)
    tj = lax.broadcasted_iota(jnp.int32, (c_len, c_len), 1)
    blk = (lax.broadcasted_iota(jnp.int32, (pair_v, pair_k), 0) // GLA_DV
           == lax.broadcasted_iota(jnp.int32, (pair_v, pair_k), 1) // GLA_DK)

    group = math.gcd(GLA_GROUP, n)
    keep2 = [jnp.concatenate([m, m], axis=0) for m in (tj <= ti, tj >= ti)]
    ti2 = lax.broadcasted_iota(jnp.int32, (c_len, 2 * c_len), 0)
    tj2 = lax.broadcasted_iota(jnp.int32, (c_len, 2 * c_len), 1) & (c_len - 1)
    tri2 = [jnp.where(m, 1.0, 0.0).astype(BF16) for m in (tj2 <= ti2, tj2 >= ti2)]

    def body(i, carry):
        units = []
        for d in range(2):
            for g in range(group):
                c = i * group + g if d == 0 else n - 1 - (i * group + g)
                rows = pl.ds(pl.multiple_of(c * c_len, c_len), c_len)
                lg = b_scr[d, rows, :]
                hi = lg.astype(BF16)
                rest = (lg - hi.astype(F32)).astype(BF16)
                units.append(dict(d=d, rows=rows, b=_dot(tri2[d], jnp.concatenate([hi, rest], axis=0))))
        for u in units:
            d, rows, b = u["d"], u["rows"], u["b"]
            bl = b[c_len - 1:c_len, :] if d == 0 else b[0:1, :]
            k = k_ref[rows, :].astype(F32)
            v16 = v_ref[rows, :].astype(BF16)
            qd = q_ref[rows, :].astype(F32) * (GLA_DK ** -0.5) * jnp.exp(b)
            kd = (k * jnp.exp(-b)).astype(BF16)
            kdec = (k * jnp.exp(bl - b)).astype(BF16)
            q2 = jnp.concatenate([jnp.where((lane // GLA_DK) == hh, qd, 0.0) for hh in range(2)], axis=0)
            att = _dot_nt(q2.astype(BF16), kd)
            upd = _dot_tn(v16, kdec)
            u.update(qd=qd.astype(BF16), v16=v16, att=att, upd=upd, dec=jnp.exp(bl))
        for d in range(2):
            s = s_scr[d]
            for u in units:
                if u["d"] == d:
                    u["o"] = _dot_nt(u["qd"], s.astype(BF16))
                    s = jnp.where(blk, s * u["dec"] + u["upd"], 0.0)
            s_scr[d] = s
        for u in units:
            att = jnp.where(keep2[u["d"]], u["att"], 0.0).astype(BF16)
            pv = _dot(att, u["v16"])
            intra = jnp.where((col // GLA_DV) == 0, pv[0:c_len, :], pv[c_len:, :])
            o_scr[u["d"], u["rows"], :] = u["o"] + intra
        return carry

    lax.fori_loop(0, n // group, body, 0)

    gn = gn_ref[...]
    for hh in range(2):
        sl = slice(hh * GLA_DV, (hh + 1) * GLA_DV)
        o = o_scr[0, 0:n_rows, sl] + o_scr[1, 0:n_rows, sl]
        rr = r_ref[:, sl].astype(F32)
        y_ref[:, sl] = (_rms(o) * gn * (rr * _sigmoid(rr))).astype(y_ref.dtype)


def _gla_kernel(*refs, t_ctx, t_lat, n_cast):
    qc_ref, kc_ref, vc_ref, rc_ref, gc_ref, ql_ref, kl_ref, vl_ref, rl_ref, gl_ref, gw_ref, gb_ref, gn_ref = refs[:13]
    yc_ref, yl_ref = refs[13 + n_cast:15 + n_cast]
    b_scr, o_scr, s_scr = refs[-3:]
    _cast_blocks(refs[13:13 + n_cast], refs[15 + n_cast:15 + 2 * n_cast])
    s_scr[...] = jnp.zeros_like(s_scr)
    _gla_segment(qc_ref, kc_ref, vc_ref, rc_ref, gc_ref, yc_ref, gw_ref, gb_ref, gn_ref,
                 b_scr, o_scr, s_scr, t_ctx)
    _gla_segment(ql_ref, kl_ref, vl_ref, rl_ref, gl_ref, yl_ref, gw_ref, gb_ref, gn_ref,
                 b_scr, o_scr, s_scr, t_lat)


def _gla(z, zg, gate_w, gate_b, gla_g, bn, t_ctx, t_lat, ctx_row0, cast=()):
    pk, pv = 2 * GLA_DK, 2 * GLA_DV
    q0 = FOURIER_WIDTH // pk
    k0 = (FOURIER_WIDTH + GLA_HEADS * GLA_DK) // pk
    v0 = (FOURIER_WIDTH + 2 * GLA_HEADS * GLA_DK) // pv
    r0 = (FOURIER_WIDTH + 2 * GLA_HEADS * GLA_DK + GLA_HEADS * GLA_DV) // pv

    def seg_specs(t, row0):
        off = row0 // t
        return [pl.BlockSpec((t, pk), lambda b, p: (off + b, q0 + p)),
                pl.BlockSpec((t, pk), lambda b, p: (off + b, k0 + p)),
                pl.BlockSpec((t, pv), lambda b, p: (off + b, v0 + p)),
                pl.BlockSpec((t, pv), lambda b, p: (off + b, r0 + p)),
                pl.BlockSpec((t, 2 * GLA_GATE_RANK), lambda b, p: (off + b, 0))]

    wdt = GLA_HEADS * GLA_DV
    c_args, c_in, c_shape, c_out = _cast_riders(cast, lambda b, p: b, bn)
    return pl.pallas_call(
        functools.partial(_gla_kernel, t_ctx=t_ctx, t_lat=t_lat, n_cast=len(cast)),
        out_shape=[jax.ShapeDtypeStruct((bn * t_ctx, wdt), BF16), jax.ShapeDtypeStruct((bn * t_lat, wdt), BF16)] + c_shape,
        grid=(bn, GLA_HEADS // 2),
        in_specs=seg_specs(t_ctx, ctx_row0) + seg_specs(t_lat, 0) + [
            pl.BlockSpec((2, GLA_GATE_RANK, pk), lambda b, p: (0, 0, p)),
            pl.BlockSpec((2, 1, pk), lambda b, p: (0, 0, p)),
            pl.BlockSpec((1, GLA_DV), lambda b, p: (0, 0))] + c_in,
        out_specs=[pl.BlockSpec((t_ctx, pv), lambda b, p: (b, p)),
                   pl.BlockSpec((t_lat, pv), lambda b, p: (b, p))] + c_out,
        scratch_shapes=[pltpu.VMEM((2, t_lat, pk), F32), pltpu.VMEM((2, t_lat, pv), F32),
                        pltpu.VMEM((2, pv, pk), F32)],
        compiler_params=_params("arbitrary", "arbitrary"),
        name="gla",
    )(z, z, z, z, zg, z, z, z, z, zg, gate_w, gate_b.reshape(2, 1, GLA_HEADS * GLA_DK),
      gla_g.reshape(1, GLA_DV), *c_args)


def _conv_kernel(zb_ref, zc_ref, zx_ref, w_ref, b_ref, o_ref, *, t):
    u = zc_ref[...].astype(F32) * zx_ref[...].astype(F32)
    row = lax.broadcasted_iota(jnp.int32, u.shape, 0)
    prev = jnp.where(row >= 1, pltpu.roll(u, 1, 0), 0.0)
    nxt = jnp.where(row < t - 1, pltpu.roll(u, t - 1, 0), 0.0)
    y = prev * w_ref[0:1, :] + u * w_ref[1:2, :] + nxt * w_ref[2:3, :] + b_ref[...]
    o_ref[...] = (zb_ref[...].astype(F32) * y).astype(o_ref.dtype)


def _short_conv(z, conv_w, conv_b, bn, t):
    cw = CONV_WIDTH
    return pl.pallas_call(
        functools.partial(_conv_kernel, t=t),
        out_shape=jax.ShapeDtypeStruct((bn * t, cw), BF16),
        grid=(bn,),
        in_specs=[pl.BlockSpec((t, cw), lambda b: (b, 0)),
                  pl.BlockSpec((t, cw), lambda b: (b, 1)),
                  pl.BlockSpec((t, cw), lambda b: (b, 2)),
                  pl.BlockSpec((3, cw), lambda b: (0, 0)),
                  pl.BlockSpec((1, cw), lambda b: (0, 0))],
        out_specs=pl.BlockSpec((t, cw), lambda b: (b, 0)),
        compiler_params=_params("parallel"),
        name="short_conv",
    )(z, z, z, conv_w, conv_b.reshape(1, cw))


def _rope_tables(t):
    rows = t // GRID_W
    row = jnp.repeat(jnp.arange(rows), GRID_W).astype(F32)
    col = jnp.tile(jnp.arange(GRID_W), rows).astype(F32)
    n = ROPE_AXIS_DIM // 2
    inv = ROPE_THETA ** (-jnp.arange(n, dtype=F32) / n)
    ar, ac = row[:, None] * inv, col[:, None] * inv
    ang = jnp.concatenate([ar, ar, ac, ac], axis=-1)
    sign = jnp.tile(jnp.concatenate([-jnp.ones((n,), F32), jnp.ones((n,), F32)]), 2)
    cos = jnp.cos(ang)
    sin = jnp.sin(ang) * sign
    return jnp.tile(cos, (1, 2)), jnp.tile(sin, (1, 2))


def _rope(x, cos, sin):
    lane = lax.broadcasted_iota(jnp.int32, x.shape, 1)
    n = ROPE_AXIS_DIM // 2
    w = x.shape[1]
    partner = jnp.where((lane & (2 * n - 1)) < n, pltpu.roll(x, w - n, 1), pltpu.roll(x, n, 1))
    return x * cos + partner * sin


ATTN_ONES_ROWS = 16
ATTN_HEADS_PER_STEP = 2


def _attn_kernel(*refs, t_lat, lam_init, tq, n_cast):
    lam_ref, q_ref, kl_ref, vl_ref, kc_ref, vc_ref, cos_ref, sin_ref, dn_ref = refs[:9]
    o_ref = refs[9 + n_cast]
    k_scr, vt_scr, s0_scr, m0_scr, s1_scr, m1_scr = refs[-6:]
    _cast_blocks(refs[9:9 + n_cast], refs[10 + n_cast:10 + 2 * n_cast])
    hw = 2 * HEAD_DIM
    t_all = k_scr.shape[1]

    for hd in range(ATTN_HEADS_PER_STEP):
        cols = slice(hd * hw, (hd + 1) * hw)
        k_scr[hd, 0:t_lat, :] = _rope(kl_ref[:, cols].astype(F32), cos_ref[...], sin_ref[...]).astype(BF16)
        k_scr[hd, t_lat:, :] = kc_ref[:, cols].astype(BF16)
        vt_scr[hd, 0:hw, 0:t_lat] = vl_ref[:, cols].astype(F32).T.astype(BF16)
        vt_scr[hd, 0:hw, t_lat:] = vc_ref[:, cols].astype(F32).T.astype(BF16)
        vt_scr[hd, hw:, :] = jnp.ones((ATTN_ONES_ROWS, t_all), BF16)

    lv = lam_ref[...]
    lam = (jnp.exp(jnp.sum(lv[0:1] * lv[1:2], axis=-1, keepdims=True))
           - jnp.exp(jnp.sum(lv[2:3] * lv[3:4], axis=-1, keepdims=True)) + lam_init)
    lane = lax.broadcasted_iota(jnp.int32, (tq, hw), 1)

    n_blocks = t_lat // tq
    slots = ((s0_scr, m0_scr), (s1_scr, m1_scr))

    def scores(hd, i, slot):
        rows = pl.ds(pl.multiple_of(i * tq, tq), tq)
        q = (_rope(q_ref[rows, hd * hw:(hd + 1) * hw].astype(F32), cos_ref[rows, :], sin_ref[rows, :])
             * (HEAD_DIM ** -0.5 * math.log2(math.e)))
        s_scr, m_scr = slots[slot]
        for half in range(2):
            qb = jnp.where((lane // HEAD_DIM) == half, q, 0.0).astype(BF16)
            s = _dot_nt(k_scr[hd], qb)
            s_scr[half] = s
            m_scr[half] = jnp.max(s, axis=0, keepdims=True)

    def outputs(hd, i, slot):
        rows = pl.ds(pl.multiple_of(i * tq, tq), tq)
        s_scr, m_scr = slots[slot]
        outs = []
        for half in range(2):
            p = jnp.exp2(s_scr[half] - m_scr[half]).astype(BF16)
            acc = _dot(vt_scr[hd], p)
            outs.append(acc[0:hw, :] / acc[hw:hw + 1, :])
        o = (outs[0] - lam * outs[1]).T
        o_ref[rows, hd * hw:(hd + 1) * hw] = (_rms(o) * dn_ref[...] * (1.0 - lam_init)).astype(o_ref.dtype)

    assert n_blocks % 2 == 0
    scores(0, 0, 0)
    for hd in range(ATTN_HEADS_PER_STEP):
        def body(j, carry, hd=hd):
            scores(hd, 2 * j + 1, 1)
            outputs(hd, 2 * j, 0)
            scores(hd, 2 * j + 2, 0)
            outputs(hd, 2 * j + 1, 1)
            return carry

        lax.fori_loop(0, n_blocks // 2 - 1, body, 0)
        scores(hd, n_blocks - 1, 1)
        outputs(hd, n_blocks - 2, 0)
        if hd + 1 < ATTN_HEADS_PER_STEP:
            scores(hd + 1, 0, 0)
        outputs(hd, n_blocks - 1, 1)


def _diff_attn(z, lam_vecs, dnorm, lam_init, bn, t_lat, t_ctx, ctx_row0, cast=()):
    hw = 2 * HEAD_DIM
    wb = ATTN_HEADS_PER_STEP * hw
    q0 = 3 * CONV_WIDTH // wb
    k0 = q0 + DIFF_HEADS // ATTN_HEADS_PER_STEP
    v0 = k0 + DIFF_HEADS // ATTN_HEADS_PER_STEP
    tq = 256
    t_all = t_lat + t_ctx
    coff = ctx_row0 // t_ctx
    cos, sin = _rope_tables(t_lat)
    c_args, c_in, c_shape, c_out = _cast_riders(cast, lambda b, p: b, bn)
    outs = pl.pallas_call(
        functools.partial(_attn_kernel, t_lat=t_lat, lam_init=lam_init, tq=tq, n_cast=len(cast)),
        out_shape=[jax.ShapeDtypeStruct((bn * t_lat, DIFF_HEADS * DIFF_DV), BF16)] + c_shape,
        grid=(bn, DIFF_HEADS // ATTN_HEADS_PER_STEP),
        in_specs=[
            pl.BlockSpec((4, HEAD_DIM), lambda b, p: (0, 0)),
            pl.BlockSpec((t_lat, wb), lambda b, p: (b, q0 + p)),
            pl.BlockSpec((t_lat, wb), lambda b, p: (b, k0 + p)),
            pl.BlockSpec((t_lat, wb), lambda b, p: (b, v0 + p)),
            pl.BlockSpec((t_ctx, wb), lambda b, p: (coff + b, k0 + p)),
            pl.BlockSpec((t_ctx, wb), lambda b, p: (coff + b, v0 + p)),
            pl.BlockSpec((t_lat, hw), lambda b, p: (0, 0)),
            pl.BlockSpec((t_lat, hw), lambda b, p: (0, 0)),
            pl.BlockSpec((1, DIFF_DV), lambda b, p: (0, 0)),
        ] + c_in,
        out_specs=[pl.BlockSpec((t_lat, wb), lambda b, p: (b, p))] + c_out,
        scratch_shapes=[pltpu.VMEM((ATTN_HEADS_PER_STEP, t_all, hw), BF16),
                        pltpu.VMEM((ATTN_HEADS_PER_STEP, hw + ATTN_ONES_ROWS, t_all), BF16),
                        pltpu.VMEM((2, t_all, tq), F32), pltpu.VMEM((2, 1, tq), F32),
                        pltpu.VMEM((2, t_all, tq), F32), pltpu.VMEM((2, 1, tq), F32)],
        compiler_params=_params("arbitrary", "arbitrary"),
        name="diff_attn",
    )(lam_vecs, z, z, z, z, z, cos, sin, dnorm.reshape(1, DIFF_DV), *c_args)
    return outs if cast else outs[0]


def kernel(x, c, ctx, c_ctx, ada_w, ada_b, norm_ffn1, norm_mix, norm_ffn2, ffn1_w_in, ffn1_w_out, ffn2_w_in,
           ffn2_w_out, mix_w_out, even_w_in, gla_gate_w, gla_gate_b, gla_norm, odd_w_in, conv_w, conv_b,
           lambda_q1, lambda_k1, lambda_q2, lambda_k2, diff_norm, final_norm):
    assert DEPTH == 2
    bn, t_lat, d = x.shape
    t_ctx = ctx.shape[1]
    assert bn < MOD_ROWS
    ctx_row = bn

    cond = jnp.concatenate([c, c_ctx[None, :], jnp.zeros((MOD_ROWS - bn - 1, d), F32)], axis=0)
    mods = _modulation(cond, ada_w, ada_b)

    tm = 512
    n_lat = bn * t_lat
    lat_blocks = n_lat // tm
    lat_row = lambda i: i // (t_lat // tm)
    all_row = lambda i: jnp.where(i < lat_blocks, i // (t_lat // tm), ctx_row)
    g1 = norm_ffn1.reshape(DEPTH, 1, d)
    gm = norm_mix.reshape(DEPTH, 1, d)
    g2 = norm_ffn2.reshape(DEPTH, 1, d)
    w1i0, w1o0 = ffn1_w_in[0].astype(BF16), ffn1_w_out[0].astype(BF16)
    w_even = even_w_in[0].astype(BF16)
    cast_gla = ((ffn2_w_in, 0, bn), (ffn2_w_out, 0, bn), (mix_w_out, 0, bn))
    cast_ffn = ((ffn1_w_in, 1, FFN_CAST_STEPS), (ffn1_w_out, 1, FFN_CAST_STEPS // 2), (odd_w_in, 0, FFN_CAST_STEPS))
    cast_attn = ((ffn2_w_in, 1, bn), (ffn2_w_out, 1, bn), (mix_w_out, 1, bn))

    proj0 = (gm, w_even, 2 * GLA_GATE_RANK)
    h, z, zg = _ffn((x.reshape(n_lat, d), ctx.reshape(bn * t_ctx, d)), mods, 0, 0, all_row, g1, w1i0, w1o0,
                    tm=tm, n_first=lat_blocks, proj=proj0)
    yf_l = _fourier(z, t_lat, bn)
    yf_c = _fourier(z, t_ctx, bn, row0=n_lat)
    yg_c, yg_l, w2i0, w2o0, wmix0 = _gla(z, zg, gla_gate_w[0], gla_gate_b[0], gla_norm[0], bn, t_ctx, t_lat, n_lat,
                                         cast=cast_gla)
    h, w1i1, w1o1, w_odd = _ffn(h, mods, 0, 6, all_row, g2, w2i0, w2o0, tm=tm, n_first=lat_blocks,
                                mix=((yf_l, yf_c), (yg_l, yg_c), wmix0), cast=cast_ffn)

    h, z = _ffn(h, mods, 1, 0, all_row, g1, w1i1, w1o1, tm=tm, proj=(gm, w_odd, 0))
    lam_init = 0.8 - 0.6 * math.exp(-0.3 * 1)
    lam_vecs = jnp.stack([lambda_q1[0], lambda_k1[0], lambda_q2[0], lambda_k2[0]]).astype(F32)
    y_conv = _short_conv(z, conv_w[0], conv_b[0], bn, t_lat)
    y_att, w2i1, w2o1, wmix1 = _diff_attn(z, lam_vecs, diff_norm[0], lam_init, bn, t_lat, t_ctx, n_lat,
                                          cast=cast_attn)
    h = _ffn(h, mods, 1, 6, lat_row, g2, w2i1, w2o1, tm=tm, mix=(y_conv, y_att, wmix1),
             final_gain=final_norm.reshape(1, d), n_rows=n_lat)
    return h.reshape(bn, t_lat, d)
```

```python
import functools
import math

import numpy as np
import jax
import jax.numpy as jnp
from jax import lax
from jax.experimental import pallas as pl
from jax.experimental.pallas import tpu as pltpu

D_MODEL = 1024
DEPTH = 2
GRID_W = 64
HEAD_DIM = 64
N_MOD = 9
FFN_HIDDEN = 2816
NORM_EPS = 1e-6
FOURIER_GROUPS = 4
FOURIER_WIDTH = FOURIER_GROUPS * HEAD_DIM
GLA_HEADS = 6
GLA_DK = 64
GLA_DV = 128
GLA_GATE_RANK = 16
GLA_TAU = 16.0
GLA_CHUNK = 64
GLA_GROUP = 8
FFN_TILE = 256
FFN_CAST_STEPS = 32
PROJ_TILE = 512
CONV_WIDTH = 4 * HEAD_DIM
DIFF_HEADS = 6
DIFF_DV = 2 * HEAD_DIM
ROPE_THETA = 10000.0
ROPE_AXIS_DIM = HEAD_DIM // 2

EVEN_MAIN = FOURIER_WIDTH + 2 * GLA_HEADS * GLA_DK + 2 * GLA_HEADS * GLA_DV
ODD_IN = 3 * CONV_WIDTH + 3 * DIFF_HEADS * DIFF_DV

MOD_ROWS = 16
VMEM_LIMIT = 48 * 1024 * 1024

F32 = jnp.float32
BF16 = jnp.bfloat16


def _params(*sem):
    return pltpu.CompilerParams(dimension_semantics=sem, vmem_limit_bytes=VMEM_LIMIT)


def _sigmoid(x):
    return 1.0 / (1.0 + jnp.exp(-x))


def _dot(a, b):
    return jnp.dot(a, b, preferred_element_type=F32)


def _dot_nt(a, b):
    return lax.dot_general(a, b, (((1,), (1,)), ((), ())), preferred_element_type=F32)


def _dot_tn(a, b):
    return lax.dot_general(a, b, (((0,), (0,)), ((), ())), preferred_element_type=F32)


def _rms(x):
    return x * lax.rsqrt(jnp.mean(x * x, axis=-1, keepdims=True) + NORM_EPS)


def _mod_kernel(cond_ref, w_ref, b_ref, o_ref):
    c = cond_ref[...]
    s = c * _sigmoid(c)
    w = w_ref[...]
    w_hi = w.astype(BF16)
    w_lo = (w - w_hi.astype(F32)).astype(BF16)
    s_hi = s.astype(BF16)
    s_lo = (s - s_hi.astype(F32)).astype(BF16)
    o_ref[...] = _dot(s_hi, w_hi) + _dot(s_lo, w_hi) + _dot(s_hi, w_lo) + b_ref[...]


def _modulation(cond, ada_w, ada_b):
    n = N_MOD * D_MODEL
    tn = n // 4
    out = pl.pallas_call(
        _mod_kernel,
        out_shape=jax.ShapeDtypeStruct((DEPTH, MOD_ROWS, n), F32),
        grid=(DEPTH, n // tn),
        in_specs=[
            pl.BlockSpec((MOD_ROWS, D_MODEL), lambda l, j: (0, 0)),
            pl.BlockSpec((None, D_MODEL, tn), lambda l, j: (l, 0, j)),
            pl.BlockSpec((None, 1, tn), lambda l, j: (l, 0, j)),
        ],
        out_specs=pl.BlockSpec((None, MOD_ROWS, tn), lambda l, j: (l, 0, j)),
        compiler_params=_params("parallel", "parallel"),
        name="modulation",
    )(cond, ada_w, ada_b.reshape(DEPTH, 1, n))
    return out.reshape(DEPTH, MOD_ROWS, N_MOD, 1, D_MODEL)


def _mod_spec(layer, k, row_of_block):
    return pl.BlockSpec((None, None, None, 1, D_MODEL),
                        lambda i, *_: (layer, row_of_block(i), k, 0, 0))


def _gain_spec(layer):
    return pl.BlockSpec((None, 1, D_MODEL), lambda i, *_: (layer, 0, 0))


def _resident(block_shape, index_map):
    return pl.BlockSpec(block_shape, index_map, pipeline_mode=pl.Buffered(1))


def _cast_riders(sources, step_of, n_steps):
    args, in_specs, out_shape, out_specs = [], [], [], []
    for arr, lead, nb in sources:
        _, r, c = arr.shape
        rows, stride = r // nb, n_steps // nb
        assert rows * nb == r and stride * nb == n_steps and rows % 16 == 0
        idx = lambda *g, nb=nb, stride=stride: jnp.minimum(step_of(*g) // stride, nb - 1)
        args.append(arr)
        in_specs.append(pl.BlockSpec((None, rows, c), lambda *g, idx=idx, lead=lead: (lead, idx(*g), 0)))
        out_shape.append(jax.ShapeDtypeStruct((r, c), BF16))
        out_specs.append(pl.BlockSpec((rows, c), lambda *g, idx=idx: (idx(*g), 0)))
    return args, in_specs, out_shape, out_specs


def _cast_blocks(src_refs, dst_refs):
    for src, dst in zip(src_refs, dst_refs):
        dst[...] = src[...].astype(BF16)


def _ffn_kernel(*refs, with_mix, with_proj, with_gate, with_final, paired, n_first, n_cast):
    it = iter(refs)
    pairs = iter(paired)

    def rows_in():
        a = next(it)
        if not next(pairs):
            return a[...]
        b = next(it)
        return jnp.where(pl.program_id(0) < n_first, a[...], b[...])

    x = rows_in()
    if with_mix:
        ma, mb = rows_in(), rows_in()
        wm_ref, gm_ref = next(it), next(it)
    sh_ref, sc_ref, g_ref, gain_ref, wi_ref, wo_ref = (next(it) for _ in range(6))
    if with_proj:
        psh_ref, psc_ref, pgain_ref, wp_ref = (next(it) for _ in range(4))
    if with_final:
        fn_ref = next(it)
    cast_src = [next(it) for _ in range(n_cast)]
    o_ref = next(it)
    if with_proj:
        z_ref = next(it)
    if with_gate:
        zg_ref = next(it)
    cast_dst = [next(it) for _ in range(n_cast)]
    a_scr = next(it)
    _cast_blocks(cast_src, cast_dst)

    if with_mix:
        ka = ma.shape[1]
        x = x + gm_ref[...] * (_dot(ma, wm_ref[0:ka, :]) + _dot(mb, wm_ref[ka:, :]))
    xn = (_rms(x) * gain_ref[...] * (1.0 + sc_ref[...]) + sh_ref[...]).astype(BF16)
    for j in range(FFN_HIDDEN // FFN_TILE):
        lo = j * FFN_TILE
        g = _dot(xn, wi_ref[:, lo:lo + FFN_TILE])
        u = _dot(xn, wi_ref[:, FFN_HIDDEN + lo:FFN_HIDDEN + lo + FFN_TILE])
        a_scr[:, lo:lo + FFN_TILE] = (g * _sigmoid(g) * u).astype(BF16)
    out = x + (0.5 * g_ref[...]) * _dot(a_scr[...], wo_ref[...])
    if with_final:
        out = _rms(out) * fn_ref[...]
    o_ref[...] = out
    if with_proj:
        xm = (_rms(out) * pgain_ref[...] * (1.0 + psc_ref[...]) + psh_ref[...]).astype(BF16)
        n_out = z_ref.shape[1]
        for lo in range(0, n_out, PROJ_TILE):
            z_ref[:, lo:lo + PROJ_TILE] = _dot(xm, wp_ref[:, lo:lo + PROJ_TILE]).astype(z_ref.dtype)
        if with_gate:
            zg_ref[...] = _dot(xm, wp_ref[:, n_out:])


def _ffn(h, mods, layer, mod_base, row_of_block, gain, w_in, w_out, *, tm, n_first=None, n_rows=None, mix=None,
         proj=None, final_gain=None, cast=()):
    m = n_rows or (sum(a.shape[0] for a in h) if isinstance(h, tuple) else h.shape[0])
    row = lambda i: (i, 0)
    args, specs, paired = [], [], []

    def add_rows(a):
        paired.append(isinstance(a, tuple))
        if paired[-1]:
            first, second = a
            assert first.shape[0] == n_first * tm
            args.extend([first, second])
            specs.extend([pl.BlockSpec((tm, first.shape[1]), lambda i: (jnp.minimum(i, n_first - 1), 0)),
                          pl.BlockSpec((tm, second.shape[1]), lambda i: (jnp.maximum(i - n_first, 0), 0))])
        else:
            args.append(a)
            specs.append(pl.BlockSpec((tm, a.shape[1]), row))

    add_rows(h)
    if mix is not None:
        ma, mb, w_mix = mix
        add_rows(ma)
        add_rows(mb)
        args += [w_mix, mods]
        specs += [_resident(w_mix.shape, lambda i: (0, 0)), _mod_spec(layer, 5, row_of_block)]
    args += [mods, mods, mods, gain, w_in, w_out]
    specs += [
        _mod_spec(layer, mod_base, row_of_block),
        _mod_spec(layer, mod_base + 1, row_of_block),
        _mod_spec(layer, mod_base + 2, row_of_block),
        _gain_spec(layer),
        _resident((D_MODEL, 2 * FFN_HIDDEN), lambda i: (0, 0)),
        _resident((FFN_HIDDEN, D_MODEL), lambda i: (0, 0)),
    ]
    out_shape = [jax.ShapeDtypeStruct((m, D_MODEL), F32)]
    out_specs = [pl.BlockSpec((tm, D_MODEL), row)]
    n_gate = 0
    if proj is not None:
        pgain, wp, n_gate = proj
        n_out = wp.shape[1] - n_gate
        assert n_out % PROJ_TILE == 0
        args += [mods, mods, pgain, wp]
        specs += [_mod_spec(layer, 3, row_of_block), _mod_spec(layer, 4, row_of_block), _gain_spec(layer),
                  _resident(wp.shape, lambda i: (0, 0))]
        out_shape.append(jax.ShapeDtypeStruct((m, n_out), BF16))
        out_specs.append(pl.BlockSpec((tm, n_out), row))
        if n_gate:
            out_shape.append(jax.ShapeDtypeStruct((m, n_gate), F32))
            out_specs.append(pl.BlockSpec((tm, n_gate), row))
    if final_gain is not None:
        args.append(final_gain)
        specs.append(pl.BlockSpec((1, D_MODEL), lambda i: (0, 0)))
    c_args, c_in, c_shape, c_out = _cast_riders(cast, lambda i: i, FFN_CAST_STEPS)
    assert not cast or m // tm >= FFN_CAST_STEPS
    args += c_args
    specs += c_in
    out_shape += c_shape
    out_specs += c_out
    outs = pl.pallas_call(
        functools.partial(_ffn_kernel, with_mix=mix is not None, with_proj=proj is not None, with_gate=n_gate > 0,
                          with_final=final_gain is not None, paired=tuple(paired), n_first=n_first,
                          n_cast=len(cast)),
        out_shape=out_shape,
        grid=(m // tm,),
        in_specs=specs,
        out_specs=out_specs,
        scratch_shapes=[pltpu.VMEM((tm, FFN_HIDDEN), BF16)],
        compiler_params=_params("arbitrary" if cast else "parallel"),
        name="ffn",
    )(*args)
    return outs if len(outs) > 1 else outs[0]


def _dft_tables(t, tq):
    k = (np.arange(tq, dtype=np.int64)[:, None] * np.arange(t, dtype=np.int64)[None, :]) % t
    ang = 2.0 * np.pi * k.astype(np.float64) / t
    pos = np.concatenate([np.cos(ang), -np.sin(ang)], axis=1).astype(np.float32)
    kc = (np.arange(HEAD_DIM)[:, None] * np.arange(HEAD_DIM)[None, :]) % HEAD_DIM
    angc = 2.0 * np.pi * kc.astype(np.float64) / HEAD_DIM
    eye = np.eye(FOURIER_GROUPS)
    chan = np.concatenate([np.kron(eye, np.cos(angc)), np.kron(eye, np.sin(angc))], axis=1).astype(np.float32)
    return pos, chan


def _dft_kernel(p_ref, z_ref, c_ref, o_ref, p_scr, ab_scr, *, scale, quarter_turns):
    i, b = pl.program_id(0), pl.program_id(1)
    t = p_ref.shape[1] // 2

    @pl.when(b == 0)
    def _():
        col = lax.broadcasted_iota(jnp.int32, (1, t), 1)
        k = (i * quarter_turns * col) & 3
        ca = jnp.where(k == 0, 1.0, jnp.where(k == 2, -1.0, 0.0))
        sa = jnp.where(k == 1, 1.0, jnp.where(k == 3, -1.0, 0.0))
        c0, n0 = p_ref[:, 0:t], p_ref[:, t:]
        p_scr[:, 0:t] = (ca * c0 + sa * n0).astype(BF16)
        p_scr[:, t:] = (ca * n0 - sa * c0).astype(BF16)

    @pl.when(i == 0)
    def _():
        ab = _dot(z_ref[...], c_ref[...].astype(BF16))
        ab_scr[b, 0:t, :] = ab[:, :FOURIER_WIDTH].astype(BF16)
        ab_scr[b, t:, :] = ab[:, FOURIER_WIDTH:].astype(BF16)

    o_ref[...] = (_dot(p_scr[...], ab_scr[b]) * scale).astype(BF16)


def _fourier(z, t, bn, row0=0):
    tq = min(t, 512)
    assert (4 * tq) % t == 0
    pos, chan = _dft_tables(t, tq)
    off = row0 // t
    return pl.pallas_call(
        functools.partial(_dft_kernel, scale=1.0 / math.sqrt(t * HEAD_DIM), quarter_turns=4 * tq // t),
        out_shape=jax.ShapeDtypeStruct((bn * t, FOURIER_WIDTH), BF16),
        grid=(t // tq, bn),
        in_specs=[_resident((tq, 2 * t), lambda i, b: (0, 0)),
                  pl.BlockSpec((t, FOURIER_WIDTH), lambda i, b: (off + b, 0)),
                  pl.BlockSpec((FOURIER_WIDTH, 2 * FOURIER_WIDTH), lambda i, b: (0, 0))],
        out_specs=pl.BlockSpec((tq, FOURIER_WIDTH), lambda i, b: (b * (t // tq) + i, 0)),
        scratch_shapes=[pltpu.VMEM((tq, 2 * t), BF16), pltpu.VMEM((bn, 2 * t, FOURIER_WIDTH), BF16)],
        compiler_params=_params("arbitrary", "arbitrary"),
        name="dft",
    )(jnp.asarray(pos), z, jnp.asarray(chan))


def _gla_segment(q_ref, k_ref, v_ref, r_ref, g_ref, y_ref, gw_ref, gb_ref, gn_ref,
                 b_scr, o_scr, s_scr, n_rows):
    c_len = GLA_CHUNK
    n = n_rows // c_len
    pair_k = 2 * GLA_DK
    pair_v = 2 * GLA_DV

    for d in range(2):
        zg = g_ref[:, d * GLA_GATE_RANK:(d + 1) * GLA_GATE_RANK]
        logit = _dot(zg.astype(BF16), gw_ref[d].astype(BF16)) + gb_ref[d]
        b_scr[d, 0:n_rows, :] = (jnp.minimum(logit, 0.0) - jnp.log(1.0 + jnp.exp(-jnp.abs(logit)))) * (1.0 / GLA_TAU)

    lane = lax.broadcasted_iota(jnp.int32, (c_len, pair_k), 1)
    col = lax.broadcasted_iota(jnp.int32, (c_len, pair_v), 1)
    ti = lax.broadcasted_iota(jnp.int32, (c_len, c_len), 0)
    tj = lax.broadcasted_iota(jnp.int32, (c_len, c_len), 1)
    blk = (lax.broadcasted_iota(jnp.int32, (pair_v, pair_k), 0) // GLA_DV
           == lax.broadcasted_iota(jnp.int32, (pair_v, pair_k), 1) // GLA_DK)

    group = math.gcd(GLA_GROUP, n)
    keep2 = [jnp.concatenate([m, m], axis=0) for m in (tj <= ti, tj >= ti)]
    ti2 = lax.broadcasted_iota(jnp.int32, (c_len, 2 * c_len), 0)
    tj2 = lax.broadcasted_iota(jnp.int32, (c_len, 2 * c_len), 1) & (c_len - 1)
    tri2 = [jnp.where(m, 1.0, 0.0).astype(BF16) for m in (tj2 <= ti2, tj2 >= ti2)]

    def body(i, carry):
        units = []
        for d in range(2):
            for g in range(group):
                c = i * group + g if d == 0 else n - 1 - (i * group + g)
                rows = pl.ds(pl.multiple_of(c * c_len, c_len), c_len)
                lg = b_scr[d, rows, :]
                hi = lg.astype(BF16)
                rest = (lg - hi.astype(F32)).astype(BF16)
                units.append(dict(d=d, rows=rows, b=_dot(tri2[d], jnp.concatenate([hi, rest], axis=0))))
        for u in units:
            d, rows, b = u["d"], u["rows"], u["b"]
            bl = b[c_len - 1:c_len, :] if d == 0 else b[0:1, :]
            k = k_ref[rows, :].astype(F32)
            v16 = v_ref[rows, :].astype(BF16)
            qd = q_ref[rows, :].astype(F32) * (GLA_DK ** -0.5) * jnp.exp(b)
            kd = (k * jnp.exp(-b)).astype(BF16)
            kdec = (k * jnp.exp(bl - b)).astype(BF16)
            q2 = jnp.concatenate([jnp.where((lane // GLA_DK) == hh, qd, 0.0) for hh in range(2)], axis=0)
            att = _dot_nt(q2.astype(BF16), kd)
            upd = _dot_tn(v16, kdec)
            u.update(qd=qd.astype(BF16), v16=v16, att=att, upd=upd, dec=jnp.exp(bl))
        for d in range(2):
            s = s_scr[d]
            for u in units:
                if u["d"] == d:
                    u["o"] = _dot_nt(u["qd"], s.astype(BF16))
                    s = jnp.where(blk, s * u["dec"] + u["upd"], 0.0)
            s_scr[d] = s
        for u in units:
            att = jnp.where(keep2[u["d"]], u["att"], 0.0).astype(BF16)
            pv = _dot(att, u["v16"])
            intra = jnp.where((col // GLA_DV) == 0, pv[0:c_len, :], pv[c_len:, :])
            o_scr[u["d"], u["rows"], :] = u["o"] + intra
        return carry

    lax.fori_loop(0, n // group, body, 0)

    gn = gn_ref[...]
    for hh in range(2):
        sl = slice(hh * GLA_DV, (hh + 1) * GLA_DV)
        o = o_scr[0, 0:n_rows, sl] + o_scr[1, 0:n_rows, sl]
        rr = r_ref[:, sl].astype(F32)
        y_ref[:, sl] = (_rms(o) * gn * (rr * _sigmoid(rr))).astype(y_ref.dtype)


def _gla_kernel(*refs, t_ctx, t_lat, n_cast):
    qc_ref, kc_ref, vc_ref, rc_ref, gc_ref, ql_ref, kl_ref, vl_ref, rl_ref, gl_ref, gw_ref, gb_ref, gn_ref = refs[:13]
    yc_ref, yl_ref = refs[13 + n_cast:15 + n_cast]
    b_scr, o_scr, s_scr = refs[-3:]
    _cast_blocks(refs[13:13 + n_cast], refs[15 + n_cast:15 + 2 * n_cast])
    s_scr[...] = jnp.zeros_like(s_scr)
    _gla_segment(qc_ref, kc_ref, vc_ref, rc_ref, gc_ref, yc_ref, gw_ref, gb_ref, gn_ref,
                 b_scr, o_scr, s_scr, t_ctx)
    _gla_segment(ql_ref, kl_ref, vl_ref, rl_ref, gl_ref, yl_ref, gw_ref, gb_ref, gn_ref,
                 b_scr, o_scr, s_scr, t_lat)


def _gla(z, zg, gate_w, gate_b, gla_g, bn, t_ctx, t_lat, ctx_row0, cast=()):
    pk, pv = 2 * GLA_DK, 2 * GLA_DV
    q0 = FOURIER_WIDTH // pk
    k0 = (FOURIER_WIDTH + GLA_HEADS * GLA_DK) // pk
    v0 = (FOURIER_WIDTH + 2 * GLA_HEADS * GLA_DK) // pv
    r0 = (FOURIER_WIDTH + 2 * GLA_HEADS * GLA_DK + GLA_HEADS * GLA_DV) // pv

    def seg_specs(t, row0):
        off = row0 // t
        return [pl.BlockSpec((t, pk), lambda b, p: (off + b, q0 + p)),
                pl.BlockSpec((t, pk), lambda b, p: (off + b, k0 + p)),
                pl.BlockSpec((t, pv), lambda b, p: (off + b, v0 + p)),
                pl.BlockSpec((t, pv), lambda b, p: (off + b, r0 + p)),
                pl.BlockSpec((t, 2 * GLA_GATE_RANK), lambda b, p: (off + b, 0))]

    wdt = GLA_HEADS * GLA_DV
    c_args, c_in, c_shape, c_out = _cast_riders(cast, lambda b, p: b, bn)
    return pl.pallas_call(
        functools.partial(_gla_kernel, t_ctx=t_ctx, t_lat=t_lat, n_cast=len(cast)),
        out_shape=[jax.ShapeDtypeStruct((bn * t_ctx, wdt), BF16), jax.ShapeDtypeStruct((bn * t_lat, wdt), BF16)] + c_shape,
        grid=(bn, GLA_HEADS // 2),
        in_specs=seg_specs(t_ctx, ctx_row0) + seg_specs(t_lat, 0) + [
            pl.BlockSpec((2, GLA_GATE_RANK, pk), lambda b, p: (0, 0, p)),
            pl.BlockSpec((2, 1, pk), lambda b, p: (0, 0, p)),
            pl.BlockSpec((1, GLA_DV), lambda b, p: (0, 0))] + c_in,
        out_specs=[pl.BlockSpec((t_ctx, pv), lambda b, p: (b, p)),
                   pl.BlockSpec((t_lat, pv), lambda b, p: (b, p))] + c_out,
        scratch_shapes=[pltpu.VMEM((2, t_lat, pk), F32), pltpu.VMEM((2, t_lat, pv), F32),
                        pltpu.VMEM((2, pv, pk), F32)],
        compiler_params=_params("arbitrary", "arbitrary"),
        name="gla",
    )(z, z, z, z, zg, z, z, z, z, zg, gate_w, gate_b.reshape(2, 1, GLA_HEADS * GLA_DK),
      gla_g.reshape(1, GLA_DV), *c_args)


def _conv_kernel(zb_ref, zc_ref, zx_ref, w_ref, b_ref, o_ref, *, t):
    u = zc_ref[...].astype(F32) * zx_ref[...].astype(F32)
    row = lax.broadcasted_iota(jnp.int32, u.shape, 0)
    prev = jnp.where(row >= 1, pltpu.roll(u, 1, 0), 0.0)
    nxt = jnp.where(row < t - 1, pltpu.roll(u, t - 1, 0), 0.0)
    y = prev * w_ref[0:1, :] + u * w_ref[1:2, :] + nxt * w_ref[2:3, :] + b_ref[...]
    o_ref[...] = (zb_ref[...].astype(F32) * y).astype(o_ref.dtype)


def _short_conv(z, conv_w, conv_b, bn, t):
    cw = CONV_WIDTH
    return pl.pallas_call(
        functools.partial(_conv_kernel, t=t),
        out_shape=jax.ShapeDtypeStruct((bn * t, cw), BF16),
        grid=(bn,),
        in_specs=[pl.BlockSpec((t, cw), lambda b: (b, 0)),
                  pl.BlockSpec((t, cw), lambda b: (b, 1)),
                  pl.BlockSpec((t, cw), lambda b: (b, 2)),
                  pl.BlockSpec((3, cw), lambda b: (0, 0)),
                  pl.BlockSpec((1, cw), lambda b: (0, 0))],
        out_specs=pl.BlockSpec((t, cw), lambda b: (b, 0)),
        compiler_params=_params("parallel"),
        name="short_conv",
    )(z, z, z, conv_w, conv_b.reshape(1, cw))


def _rope_tables(t):
    rows = t // GRID_W
    row = jnp.repeat(jnp.arange(rows), GRID_W).astype(F32)
    col = jnp.tile(jnp.arange(GRID_W), rows).astype(F32)
    n = ROPE_AXIS_DIM // 2
    inv = ROPE_THETA ** (-jnp.arange(n, dtype=F32) / n)
    ar, ac = row[:, None] * inv, col[:, None] * inv
    ang = jnp.concatenate([ar, ar, ac, ac], axis=-1)
    sign = jnp.tile(jnp.concatenate([-jnp.ones((n,), F32), jnp.ones((n,), F32)]), 2)
    cos = jnp.cos(ang)
    sin = jnp.sin(ang) * sign
    return jnp.tile(cos, (1, 2)), jnp.tile(sin, (1, 2))


def _rope(x, cos, sin):
    lane = lax.broadcasted_iota(jnp.int32, x.shape, 1)
    n = ROPE_AXIS_DIM // 2
    w = x.shape[1]
    partner = jnp.where((lane & (2 * n - 1)) < n, pltpu.roll(x, w - n, 1), pltpu.roll(x, n, 1))
    return x * cos + partner * sin


ATTN_ONES_ROWS = 16
ATTN_HEADS_PER_STEP = 2


def _attn_kernel(*refs, t_lat, lam_init, tq, n_cast):
    lam_ref, q_ref, kl_ref, vl_ref, kc_ref, vc_ref, cos_ref, sin_ref, dn_ref = refs[:9]
    o_ref = refs[9 + n_cast]
    k_scr, vt_scr, s0_scr, m0_scr, s1_scr, m1_scr = refs[-6:]
    _cast_blocks(refs[9:9 + n_cast], refs[10 + n_cast:10 + 2 * n_cast])
    hw = 2 * HEAD_DIM
    t_all = k_scr.shape[1]

    for hd in range(ATTN_HEADS_PER_STEP):
        cols = slice(hd * hw, (hd + 1) * hw)
        k_scr[hd, 0:t_lat, :] = _rope(kl_ref[:, cols].astype(F32), cos_ref[...], sin_ref[...]).astype(BF16)
        k_scr[hd, t_lat:, :] = kc_ref[:, cols].astype(BF16)
        vt_scr[hd, 0:hw, 0:t_lat] = vl_ref[:, cols].astype(F32).T.astype(BF16)
        vt_scr[hd, 0:hw, t_lat:] = vc_ref[:, cols].astype(F32).T.astype(BF16)
        vt_scr[hd, hw:, :] = jnp.ones((ATTN_ONES_ROWS, t_all), BF16)

    lv = lam_ref[...]
    lam = (jnp.exp(jnp.sum(lv[0:1] * lv[1:2], axis=-1, keepdims=True))
           - jnp.exp(jnp.sum(lv[2:3] * lv[3:4], axis=-1, keepdims=True)) + lam_init)
    lane = lax.broadcasted_iota(jnp.int32, (tq, hw), 1)

    n_blocks = t_lat // tq
    slots = ((s0_scr, m0_scr), (s1_scr, m1_scr))

    def scores(hd, i, slot):
        rows = pl.ds(pl.multiple_of(i * tq, tq), tq)
        q = (_rope(q_ref[rows, hd * hw:(hd + 1) * hw].astype(F32), cos_ref[rows, :], sin_ref[rows, :])
             * (HEAD_DIM ** -0.5 * math.log2(math.e)))
        s_scr, m_scr = slots[slot]
        for half in range(2):
            qb = jnp.where((lane // HEAD_DIM) == half, q, 0.0).astype(BF16)
            s = _dot_nt(k_scr[hd], qb)
            s_scr[half] = s
            m_scr[half] = jnp.max(s, axis=0, keepdims=True)

    def outputs(hd, i, slot):
        rows = pl.ds(pl.multiple_of(i * tq, tq), tq)
        s_scr, m_scr = slots[slot]
        outs = []
        for half in range(2):
            p = jnp.exp2(s_scr[half] - m_scr[half]).astype(BF16)
            acc = _dot(vt_scr[hd], p)
            outs.append(acc[0:hw, :] / acc[hw:hw + 1, :])
        o = (outs[0] - lam * outs[1]).T
        o_ref[rows, hd * hw:(hd + 1) * hw] = (_rms(o) * dn_ref[...] * (1.0 - lam_init)).astype(o_ref.dtype)

    assert n_blocks % 2 == 0
    scores(0, 0, 0)
    for hd in range(ATTN_HEADS_PER_STEP):
        def body(j, carry, hd=hd):
            scores(hd, 2 * j + 1, 1)
            outputs(hd, 2 * j, 0)
            scores(hd, 2 * j + 2, 0)
            outputs(hd, 2 * j + 1, 1)
            return carry

        lax.fori_loop(0, n_blocks // 2 - 1, body, 0)
        scores(hd, n_blocks - 1, 1)
        outputs(hd, n_blocks - 2, 0)
        if hd + 1 < ATTN_HEADS_PER_STEP:
            scores(hd + 1, 0, 0)
        outputs(hd, n_blocks - 1, 1)


def _diff_attn(z, lam_vecs, dnorm, lam_init, bn, t_lat, t_ctx, ctx_row0, cast=()):
    hw = 2 * HEAD_DIM
    wb = ATTN_HEADS_PER_STEP * hw
    q0 = 3 * CONV_WIDTH // wb
    k0 = q0 + DIFF_HEADS // ATTN_HEADS_PER_STEP
    v0 = k0 + DIFF_HEADS // ATTN_HEADS_PER_STEP
    tq = 256
    t_all = t_lat + t_ctx
    coff = ctx_row0 // t_ctx
    cos, sin = _rope_tables(t_lat)
    c_args, c_in, c_shape, c_out = _cast_riders(cast, lambda b, p: b, bn)
    outs = pl.pallas_call(
        functools.partial(_attn_kernel, t_lat=t_lat, lam_init=lam_init, tq=tq, n_cast=len(cast)),
        out_shape=[jax.ShapeDtypeStruct((bn * t_lat, DIFF_HEADS * DIFF_DV), BF16)] + c_shape,
        grid=(bn, DIFF_HEADS // ATTN_HEADS_PER_STEP),
        in_specs=[
            pl.BlockSpec((4, HEAD_DIM), lambda b, p: (0, 0)),
            pl.BlockSpec((t_lat, wb), lambda b, p: (b, q0 + p)),
            pl.BlockSpec((t_lat, wb), lambda b, p: (b, k0 + p)),
            pl.BlockSpec((t_lat, wb), lambda b, p: (b, v0 + p)),
            pl.BlockSpec((t_ctx, wb), lambda b, p: (coff + b, k0 + p)),
            pl.BlockSpec((t_ctx, wb), lambda b, p: (coff + b, v0 + p)),
            pl.BlockSpec((t_lat, hw), lambda b, p: (0, 0)),
            pl.BlockSpec((t_lat, hw), lambda b, p: (0, 0)),
            pl.BlockSpec((1, DIFF_DV), lambda b, p: (0, 0)),
        ] + c_in,
        out_specs=[pl.BlockSpec((t_lat, wb), lambda b, p: (b, p))] + c_out,
        scratch_shapes=[pltpu.VMEM((ATTN_HEADS_PER_STEP, t_all, hw), BF16),
                        pltpu.VMEM((ATTN_HEADS_PER_STEP, hw + ATTN_ONES_ROWS, t_all), BF16),
                        pltpu.VMEM((2, t_all, tq), F32), pltpu.VMEM((2, 1, tq), F32),
                        pltpu.VMEM((2, t_all, tq), F32), pltpu.VMEM((2, 1, tq), F32)],
        compiler_params=_params("arbitrary", "arbitrary"),
        name="diff_attn",
    )(lam_vecs, z, z, z, z, z, cos, sin, dnorm.reshape(1, DIFF_DV), *c_args)
    return outs if cast else outs[0]


def kernel(x, c, ctx, c_ctx, ada_w, ada_b, norm_ffn1, norm_mix, norm_ffn2, ffn1_w_in, ffn1_w_out, ffn2_w_in,
           ffn2_w_out, mix_w_out, even_w_in, gla_gate_w, gla_gate_b, gla_norm, odd_w_in, conv_w, conv_b,
           lambda_q1, lambda_k1, lambda_q2, lambda_k2, diff_norm, final_norm):
    assert DEPTH == 2
    bn, t_lat, d = x.shape
    t_ctx = ctx.shape[1]
    assert bn < MOD_ROWS
    ctx_row = bn

    cond = jnp.concatenate([c, c_ctx[None, :], jnp.zeros((MOD_ROWS - bn - 1, d), F32)], axis=0)
    mods = _modulation(cond, ada_w, ada_b)

    tm = 512
    n_lat = bn * t_lat
    lat_blocks = n_lat // tm
    lat_row = lambda i: i // (t_lat // tm)
    all_row = lambda i: jnp.where(i < lat_blocks, i // (t_lat // tm), ctx_row)
    g1 = norm_ffn1.reshape(DEPTH, 1, d)
    gm = norm_mix.reshape(DEPTH, 1, d)
    g2 = norm_ffn2.reshape(DEPTH, 1, d)
    w1i0, w1o0 = ffn1_w_in[0].astype(BF16), ffn1_w_out[0].astype(BF16)
    w_even = even_w_in[0].astype(BF16)
    cast_gla = ((ffn2_w_in, 0, bn), (ffn2_w_out, 0, bn), (mix_w_out, 0, bn))
    cast_ffn = ((ffn1_w_in, 1, FFN_CAST_STEPS), (ffn1_w_out, 1, FFN_CAST_STEPS // 2), (odd_w_in, 0, FFN_CAST_STEPS))
    cast_attn = ((ffn2_w_in, 1, bn), (ffn2_w_out, 1, bn), (mix_w_out, 1, bn))

    proj0 = (gm, w_even, 2 * GLA_GATE_RANK)
    h, z, zg = _ffn((x.reshape(n_lat, d), ctx.reshape(bn * t_ctx, d)), mods, 0, 0, all_row, g1, w1i0, w1o0,
                    tm=tm, n_first=lat_blocks, proj=proj0)
    yf_l = _fourier(z, t_lat, bn)
    yf_c = _fourier(z, t_ctx, bn, row0=n_lat)
    yg_c, yg_l, w2i0, w2o0, wmix0 = _gla(z, zg, gla_gate_w[0], gla_gate_b[0], gla_norm[0], bn, t_ctx, t_lat, n_lat,
                                         cast=cast_gla)
    h, w1i1, w1o1, w_odd = _ffn(h, mods, 0, 6, all_row, g2, w2i0, w2o0, tm=tm, n_first=lat_blocks,
                                mix=((yf_l, yf_c), (yg_l, yg_c), wmix0), cast=cast_ffn)

    h, z = _ffn(h, mods, 1, 0, all_row, g1, w1i1, w1o1, tm=tm, proj=(gm, w_odd, 0))
    lam_init = 0.8 - 0.6 * math.exp(-0.3 * 1)
    lam_vecs = jnp.stack([lambda_q1[0], lambda_k1[0], lambda_q2[0], lambda_k2[0]]).astype(F32)
    y_conv = _short_conv(z, conv_w[0], conv_b[0], bn, t_lat)
    y_att, w2i1, w2o1, wmix1 = _diff_attn(z, lam_vecs, diff_norm[0], lam_init, bn, t_lat, t_ctx, n_lat,
                                          cast=cast_attn)
    h = _ffn(h, mods, 1, 6, lat_row, g2, w2i1, w2o1, tm=tm, mix=(y_conv, y_att, wmix1),
             final_gain=final_norm.reshape(1, d), n_rows=n_lat)
    return h.reshape(bn, t_lat, d)
```

```python
import functools
import math

import numpy as np
import jax
import jax.numpy as jnp
from jax import lax
from jax.experimental import pallas as pl
from jax.experimental.pallas import tpu as pltpu

D_MODEL = 1024
DEPTH = 2
GRID_W = 64
HEAD_DIM = 64
N_MOD = 9
FFN_HIDDEN = 2816
NORM_EPS = 1e-6
FOURIER_GROUPS = 4
FOURIER_WIDTH = FOURIER_GROUPS * HEAD_DIM
GLA_HEADS = 6
GLA_DK = 64
GLA_DV = 128
GLA_GATE_RANK = 16
GLA_TAU = 16.0
GLA_CHUNK = 64
GLA_GROUP = 8
FFN_TILE = 256
FFN_CAST_STEPS = 32
PROJ_TILE = 512
CONV_WIDTH = 4 * HEAD_DIM
DIFF_HEADS = 6
DIFF_DV = 2 * HEAD_DIM
ROPE_THETA = 10000.0
ROPE_AXIS_DIM = HEAD_DIM // 2

EVEN_MAIN = FOURIER_WIDTH + 2 * GLA_HEADS * GLA_DK + 2 * GLA_HEADS * GLA_DV
ODD_IN = 3 * CONV_WIDTH + 3 * DIFF_HEADS * DIFF_DV

MOD_ROWS = 16
MOD_TILES = 4
VMEM_LIMIT = 48 * 1024 * 1024

F32 = jnp.float32
BF16 = jnp.bfloat16


def _params(*sem):
    return pltpu.CompilerParams(dimension_semantics=sem, vmem_limit_bytes=VMEM_LIMIT)


def _sigmoid(x):
    return 1.0 / (1.0 + jnp.exp(-x))


def _dot(a, b):
    return jnp.dot(a, b, preferred_element_type=F32)


def _dot_nt(a, b):
    return lax.dot_general(a, b, (((1,), (1,)), ((), ())), preferred_element_type=F32)


def _dot_tn(a, b):
    return lax.dot_general(a, b, (((0,), (0,)), ((), ())), preferred_element_type=F32)


def _rms(x):
    return x * lax.rsqrt(jnp.mean(x * x, axis=-1, keepdims=True) + NORM_EPS)


def _mod_kernel(*refs, n_cast):
    cond_ref, w_ref, b_ref = refs[:3]
    o_ref = refs[3 + n_cast]
    _cast_blocks(refs[3:3 + n_cast], refs[4 + n_cast:])
    c = cond_ref[...]
    s = c * _sigmoid(c)
    w = w_ref[...]
    w_hi = w.astype(BF16)
    w_lo = (w - w_hi.astype(F32)).astype(BF16)
    s_hi = s.astype(BF16)
    s_lo = (s - s_hi.astype(F32)).astype(BF16)
    o_ref[...] = _dot(s_hi, w_hi) + _dot(s_lo, w_hi) + _dot(s_hi, w_lo) + b_ref[...]


def _modulation(cond, ada_w, ada_b, cast=()):
    n = N_MOD * D_MODEL
    tn = n // MOD_TILES
    c_args, c_in, c_shape, c_out = _cast_riders(cast, lambda l, j: l * MOD_TILES + j, DEPTH * MOD_TILES)
    outs = pl.pallas_call(
        functools.partial(_mod_kernel, n_cast=len(cast)),
        out_shape=[jax.ShapeDtypeStruct((DEPTH, MOD_ROWS, n), F32)] + c_shape,
        grid=(DEPTH, MOD_TILES),
        in_specs=[
            pl.BlockSpec((MOD_ROWS, D_MODEL), lambda l, j: (0, 0)),
            pl.BlockSpec((None, D_MODEL, tn), lambda l, j: (l, 0, j)),
            pl.BlockSpec((None, 1, tn), lambda l, j: (l, 0, j)),
        ] + c_in,
        out_specs=[pl.BlockSpec((None, MOD_ROWS, tn), lambda l, j: (l, 0, j))] + c_out,
        compiler_params=_params("arbitrary", "arbitrary"),
        name="modulation",
    )(cond, ada_w, ada_b.reshape(DEPTH, 1, n), *c_args)
    return (outs[0].reshape(DEPTH, MOD_ROWS, N_MOD, 1, D_MODEL), *outs[1:])


def _mod_spec(layer, k, row_of_block):
    return pl.BlockSpec((None, None, None, 1, D_MODEL),
                        lambda i, *_: (layer, row_of_block(i), k, 0, 0))


def _gain_spec(layer):
    return pl.BlockSpec((None, 1, D_MODEL), lambda i, *_: (layer, 0, 0))


def _resident(block_shape, index_map):
    return pl.BlockSpec(block_shape, index_map, pipeline_mode=pl.Buffered(1))


def _cast_riders(sources, step_of, n_steps):
    args, in_specs, out_shape, out_specs = [], [], [], []
    for arr, lead, nb in sources:
        _, r, c = arr.shape
        rows, stride = r // nb, n_steps // nb
        assert rows * nb == r and stride * nb == n_steps and rows % 16 == 0
        idx = lambda *g, nb=nb, stride=stride: jnp.minimum(step_of(*g) // stride, nb - 1)
        args.append(arr)
        in_specs.append(pl.BlockSpec((None, rows, c), lambda *g, idx=idx, lead=lead: (lead, idx(*g), 0)))
        out_shape.append(jax.ShapeDtypeStruct((r, c), BF16))
        out_specs.append(pl.BlockSpec((rows, c), lambda *g, idx=idx: (idx(*g), 0)))
    return args, in_specs, out_shape, out_specs


def _cast_blocks(src_refs, dst_refs):
    for src, dst in zip(src_refs, dst_refs):
        dst[...] = src[...].astype(BF16)


def _ffn_kernel(*refs, with_mix, with_proj, with_gate, with_final, paired, n_first, n_cast):
    it = iter(refs)
    pairs = iter(paired)

    def rows_in():
        a = next(it)
        if not next(pairs):
            return a[...]
        b = next(it)
        return jnp.where(pl.program_id(0) < n_first, a[...], b[...])

    x = rows_in()
    if with_mix:
        ma, mb = rows_in(), rows_in()
        wm_ref, gm_ref = next(it), next(it)
    sh_ref, sc_ref, g_ref, gain_ref, wi_ref, wo_ref = (next(it) for _ in range(6))
    if with_proj:
        psh_ref, psc_ref, pgain_ref, wp_ref = (next(it) for _ in range(4))
    if with_final:
        fn_ref = next(it)
    cast_src = [next(it) for _ in range(n_cast)]
    o_ref = next(it)
    if with_proj:
        z_ref = next(it)
    if with_gate:
        zg_ref = next(it)
    cast_dst = [next(it) for _ in range(n_cast)]
    a_scr = next(it)
    _cast_blocks(cast_src, cast_dst)

    if with_mix:
        ka = ma.shape[1]
        x = x + gm_ref[...] * (_dot(ma, wm_ref[0:ka, :]) + _dot(mb, wm_ref[ka:, :]))
    xn = (_rms(x) * gain_ref[...] * (1.0 + sc_ref[...]) + sh_ref[...]).astype(BF16)
    for j in range(FFN_HIDDEN // FFN_TILE):
        lo = j * FFN_TILE
        g = _dot(xn, wi_ref[:, lo:lo + FFN_TILE])
        u = _dot(xn, wi_ref[:, FFN_HIDDEN + lo:FFN_HIDDEN + lo + FFN_TILE])
        a_scr[:, lo:lo + FFN_TILE] = (g * _sigmoid(g) * u).astype(BF16)
    out = x + (0.5 * g_ref[...]) * _dot(a_scr[...], wo_ref[...])
    if with_final:
        out = _rms(out) * fn_ref[...]
    o_ref[...] = out
    if with_proj:
        xm = (_rms(out) * pgain_ref[...] * (1.0 + psc_ref[...]) + psh_ref[...]).astype(BF16)
        n_out = z_ref.shape[1]
        for lo in range(0, n_out, PROJ_TILE):
            z_ref[:, lo:lo + PROJ_TILE] = _dot(xm, wp_ref[:, lo:lo + PROJ_TILE]).astype(z_ref.dtype)
        if with_gate:
            zg_ref[...] = _dot(xm, wp_ref[:, n_out:])


def _ffn(h, mods, layer, mod_base, row_of_block, gain, w_in, w_out, *, tm, n_first=None, n_rows=None, mix=None,
         proj=None, final_gain=None, cast=()):
    m = n_rows or (sum(a.shape[0] for a in h) if isinstance(h, tuple) else h.shape[0])
    row = lambda i: (i, 0)
    args, specs, paired = [], [], []

    def add_rows(a):
        paired.append(isinstance(a, tuple))
        if paired[-1]:
            first, second = a
            assert first.shape[0] == n_first * tm
            args.extend([first, second])
            specs.extend([pl.BlockSpec((tm, first.shape[1]), lambda i: (jnp.minimum(i, n_first - 1), 0)),
                          pl.BlockSpec((tm, second.shape[1]), lambda i: (jnp.maximum(i - n_first, 0), 0))])
        else:
            args.append(a)
            specs.append(pl.BlockSpec((tm, a.shape[1]), row))

    add_rows(h)
    if mix is not None:
        ma, mb, w_mix = mix
        add_rows(ma)
        add_rows(mb)
        args += [w_mix, mods]
        specs += [_resident(w_mix.shape, lambda i: (0, 0)), _mod_spec(layer, 5, row_of_block)]
    args += [mods, mods, mods, gain, w_in, w_out]
    specs += [
        _mod_spec(layer, mod_base, row_of_block),
        _mod_spec(layer, mod_base + 1, row_of_block),
        _mod_spec(layer, mod_base + 2, row_of_block),
        _gain_spec(layer),
        _resident((D_MODEL, 2 * FFN_HIDDEN), lambda i: (0, 0)),
        _resident((FFN_HIDDEN, D_MODEL), lambda i: (0, 0)),
    ]
    out_shape = [jax.ShapeDtypeStruct((m, D_MODEL), F32)]
    out_specs = [pl.BlockSpec((tm, D_MODEL), row)]
    n_gate = 0
    if proj is not None:
        pgain, wp, n_gate = proj
        n_out = wp.shape[1] - n_gate
        assert n_out % PROJ_TILE == 0
        args += [mods, mods, pgain, wp]
        specs += [_mod_spec(layer, 3, row_of_block), _mod_spec(layer, 4, row_of_block), _gain_spec(layer),
                  _resident(wp.shape, lambda i: (0, 0))]
        out_shape.append(jax.ShapeDtypeStruct((m, n_out), BF16))
        out_specs.append(pl.BlockSpec((tm, n_out), row))
        if n_gate:
            out_shape.append(jax.ShapeDtypeStruct((m, n_gate), F32))
            out_specs.append(pl.BlockSpec((tm, n_gate), row))
    if final_gain is not None:
        args.append(final_gain)
        specs.append(pl.BlockSpec((1, D_MODEL), lambda i: (0, 0)))
    c_args, c_in, c_shape, c_out = _cast_riders(cast, lambda i: i, FFN_CAST_STEPS)
    assert not cast or m // tm >= FFN_CAST_STEPS
    args += c_args
    specs += c_in
    out_shape += c_shape
    out_specs += c_out
    outs = pl.pallas_call(
        functools.partial(_ffn_kernel, with_mix=mix is not None, with_proj=proj is not None, with_gate=n_gate > 0,
                          with_final=final_gain is not None, paired=tuple(paired), n_first=n_first,
                          n_cast=len(cast)),
        out_shape=out_shape,
        grid=(m // tm,),
        in_specs=specs,
        out_specs=out_specs,
        scratch_shapes=[pltpu.VMEM((tm, FFN_HIDDEN), BF16)],
        compiler_params=_params("arbitrary" if cast else "parallel"),
        name="ffn",
    )(*args)
    return outs if len(outs) > 1 else outs[0]


def _dft_tables(t, tq):
    k = (np.arange(tq, dtype=np.int64)[:, None] * np.arange(t, dtype=np.int64)[None, :]) % t
    ang = 2.0 * np.pi * k.astype(np.float64) / t
    pos = np.concatenate([np.cos(ang), -np.sin(ang)], axis=1).astype(np.float32)
    kc = (np.arange(HEAD_DIM)[:, None] * np.arange(HEAD_DIM)[None, :]) % HEAD_DIM
    angc = 2.0 * np.pi * kc.astype(np.float64) / HEAD_DIM
    eye = np.eye(FOURIER_GROUPS)
    chan = np.concatenate([np.kron(eye, np.cos(angc)), np.kron(eye, np.sin(angc))], axis=1).astype(np.float32)
    return pos, chan


def _dft_kernel(p_ref, z_ref, c_ref, o_ref, p_scr, ab_scr, *, scale, quarter_turns):
    i, b = pl.program_id(0), pl.program_id(1)
    t = p_ref.shape[1] // 2

    @pl.when(b == 0)
    def _():
        col = lax.broadcasted_iota(jnp.int32, (1, t), 1)
        k = (i * quarter_turns * col) & 3
        ca = jnp.where(k == 0, 1.0, jnp.where(k == 2, -1.0, 0.0))
        sa = jnp.where(k == 1, 1.0, jnp.where(k == 3, -1.0, 0.0))
        c0, n0 = p_ref[:, 0:t], p_ref[:, t:]
        p_scr[:, 0:t] = (ca * c0 + sa * n0).astype(BF16)
        p_scr[:, t:] = (ca * n0 - sa * c0).astype(BF16)

    @pl.when(i == 0)
    def _():
        ab = _dot(z_ref[...], c_ref[...].astype(BF16))
        ab_scr[b, 0:t, :] = ab[:, :FOURIER_WIDTH].astype(BF16)
        ab_scr[b, t:, :] = ab[:, FOURIER_WIDTH:].astype(BF16)

    o_ref[...] = (_dot(p_scr[...], ab_scr[b]) * scale).astype(BF16)


def _fourier(z, t, bn, row0=0):
    tq = min(t, 512)
    assert (4 * tq) % t == 0
    pos, chan = _dft_tables(t, tq)
    off = row0 // t
    return pl.pallas_call(
        functools.partial(_dft_kernel, scale=1.0 / math.sqrt(t * HEAD_DIM), quarter_turns=4 * tq // t),
        out_shape=jax.ShapeDtypeStruct((bn * t, FOURIER_WIDTH), BF16),
        grid=(t // tq, bn),
        in_specs=[_resident((tq, 2 * t), lambda i, b: (0, 0)),
                  pl.BlockSpec((t, FOURIER_WIDTH), lambda i, b: (off + b, 0)),
                  pl.BlockSpec((FOURIER_WIDTH, 2 * FOURIER_WIDTH), lambda i, b: (0, 0))],
        out_specs=pl.BlockSpec((tq, FOURIER_WIDTH), lambda i, b: (b * (t // tq) + i, 0)),
        scratch_shapes=[pltpu.VMEM((tq, 2 * t), BF16), pltpu.VMEM((bn, 2 * t, FOURIER_WIDTH), BF16)],
        compiler_params=_params("arbitrary", "arbitrary"),
        name="dft",
    )(jnp.asarray(pos), z, jnp.asarray(chan))


def _gla_segment(q_ref, k_ref, v_ref, r_ref, g_ref, y_ref, gw_ref, gb_ref, gn_ref,
                 b_scr, o_scr, s_scr, n_rows):
    c_len = GLA_CHUNK
    n = n_rows // c_len
    pair_k = 2 * GLA_DK
    pair_v = 2 * GLA_DV

    for d in range(2):
        zg = g_ref[:, d * GLA_GATE_RANK:(d + 1) * GLA_GATE_RANK]
        logit = _dot(zg.astype(BF16), gw_ref[d].astype(BF16)) + gb_ref[d]
        b_scr[d, 0:n_rows, :] = (jnp.minimum(logit, 0.0) - jnp.log(1.0 + jnp.exp(-jnp.abs(logit)))) * (1.0 / GLA_TAU)

    lane = lax.broadcasted_iota(jnp.int32, (c_len, pair_k), 1)
    col = lax.broadcasted_iota(jnp.int32, (c_len, pair_v), 1)
    ti = lax.broadcasted_iota(jnp.int32, (c_len, c_len), 0)
    tj = lax.broadcasted_iota(jnp.int32, (c_len, c_len), 1)
    blk = (lax.broadcasted_iota(jnp.int32, (pair_v, pair_k), 0) // GLA_DV
           == lax.broadcasted_iota(jnp.int32, (pair_v, pair_k), 1) // GLA_DK)

    group = math.gcd(GLA_GROUP, n)
    keep2 = [jnp.concatenate([m, m], axis=0) for m in (tj <= ti, tj >= ti)]
    ti2 = lax.broadcasted_iota(jnp.int32, (c_len, 2 * c_len), 0)
    tj2 = lax.broadcasted_iota(jnp.int32, (c_len, 2 * c_len), 1) & (c_len - 1)
    tri2 = [jnp.where(m, 1.0, 0.0).astype(BF16) for m in (tj2 <= ti2, tj2 >= ti2)]

    def body(i, carry):
        units = []
        for d in range(2):
            for g in range(group):
                c = i * group + g if d == 0 else n - 1 - (i * group + g)
                rows = pl.ds(pl.multiple_of(c * c_len, c_len), c_len)
                lg = b_scr[d, rows, :]
                hi = lg.astype(BF16)
                rest = (lg - hi.astype(F32)).astype(BF16)
                units.append(dict(d=d, rows=rows, b=_dot(tri2[d], jnp.concatenate([hi, rest], axis=0))))
        for u in units:
            d, rows, b = u["d"], u["rows"], u["b"]
            bl = b[c_len - 1:c_len, :] if d == 0 else b[0:1, :]
            k = k_ref[rows, :].astype(F32)
            v16 = v_ref[rows, :].astype(BF16)
            qd = q_ref[rows, :].astype(F32) * (GLA_DK ** -0.5) * jnp.exp(b)
            kd = (k * jnp.exp(-b)).astype(BF16)
            kdec = (k * jnp.exp(bl - b)).astype(BF16)
            q2 = jnp.concatenate([jnp.where((lane // GLA_DK) == hh, qd, 0.0) for hh in range(2)], axis=0)
            att = _dot_nt(q2.astype(BF16), kd)
            upd = _dot_tn(v16, kdec)
            u.update(qd=qd.astype(BF16), v16=v16, att=att, upd=upd, dec=jnp.exp(bl))
        for d in range(2):
            s = s_scr[d]
            for u in units:
                if u["d"] == d:
                    u["o"] = _dot_nt(u["qd"], s.astype(BF16))
                    s = jnp.where(blk, s * u["dec"] + u["upd"], 0.0)
            s_scr[d] = s
        for u in units:
            att = jnp.where(keep2[u["d"]], u["att"], 0.0).astype(BF16)
            pv = _dot(att, u["v16"])
            intra = jnp.where((col // GLA_DV) == 0, pv[0:c_len, :], pv[c_len:, :])
            o_scr[u["d"], u["rows"], :] = u["o"] + intra
        return carry

    lax.fori_loop(0, n // group, body, 0)

    gn = gn_ref[...]
    for hh in range(2):
        sl = slice(hh * GLA_DV, (hh + 1) * GLA_DV)
        o = o_scr[0, 0:n_rows, sl] + o_scr[1, 0:n_rows, sl]
        rr = r_ref[:, sl].astype(F32)
        y_ref[:, sl] = (_rms(o) * gn * (rr * _sigmoid(rr))).astype(y_ref.dtype)


def _gla_kernel(*refs, t_ctx, t_lat, n_cast):
    qc_ref, kc_ref, vc_ref, rc_ref, gc_ref, ql_ref, kl_ref, vl_ref, rl_ref, gl_ref, gw_ref, gb_ref, gn_ref = refs[:13]
    yc_ref, yl_ref = refs[13 + n_cast:15 + n_cast]
    b_scr, o_scr, s_scr = refs[-3:]
    _cast_blocks(refs[13:13 + n_cast], refs[15 + n_cast:15 + 2 * n_cast])
    s_scr[...] = jnp.zeros_like(s_scr)
    _gla_segment(qc_ref, kc_ref, vc_ref, rc_ref, gc_ref, yc_ref, gw_ref, gb_ref, gn_ref,
                 b_scr, o_scr, s_scr, t_ctx)
    _gla_segment(ql_ref, kl_ref, vl_ref, rl_ref, gl_ref, yl_ref, gw_ref, gb_ref, gn_ref,
                 b_scr, o_scr, s_scr, t_lat)


def _gla(z, zg, gate_w, gate_b, gla_g, bn, t_ctx, t_lat, ctx_row0, cast=()):
    pk, pv = 2 * GLA_DK, 2 * GLA_DV
    q0 = FOURIER_WIDTH // pk
    k0 = (FOURIER_WIDTH + GLA_HEADS * GLA_DK) // pk
    v0 = (FOURIER_WIDTH + 2 * GLA_HEADS * GLA_DK) // pv
    r0 = (FOURIER_WIDTH + 2 * GLA_HEADS * GLA_DK + GLA_HEADS * GLA_DV) // pv

    def seg_specs(t, row0):
        off = row0 // t
        return [pl.BlockSpec((t, pk), lambda b, p: (off + b, q0 + p)),
                pl.BlockSpec((t, pk), lambda b, p: (off + b, k0 + p)),
                pl.BlockSpec((t, pv), lambda b, p: (off + b, v0 + p)),
                pl.BlockSpec((t, pv), lambda b, p: (off + b, r0 + p)),
                pl.BlockSpec((t, 2 * GLA_GATE_RANK), lambda b, p: (off + b, 0))]

    wdt = GLA_HEADS * GLA_DV
    c_args, c_in, c_shape, c_out = _cast_riders(cast, lambda b, p: b, bn)
    return pl.pallas_call(
        functools.partial(_gla_kernel, t_ctx=t_ctx, t_lat=t_lat, n_cast=len(cast)),
        out_shape=[jax.ShapeDtypeStruct((bn * t_ctx, wdt), BF16), jax.ShapeDtypeStruct((bn * t_lat, wdt), BF16)] + c_shape,
        grid=(bn, GLA_HEADS // 2),
        in_specs=seg_specs(t_ctx, ctx_row0) + seg_specs(t_lat, 0) + [
            pl.BlockSpec((2, GLA_GATE_RANK, pk), lambda b, p: (0, 0, p)),
            pl.BlockSpec((2, 1, pk), lambda b, p: (0, 0, p)),
            pl.BlockSpec((1, GLA_DV), lambda b, p: (0, 0))] + c_in,
        out_specs=[pl.BlockSpec((t_ctx, pv), lambda b, p: (b, p)),
                   pl.BlockSpec((t_lat, pv), lambda b, p: (b, p))] + c_out,
        scratch_shapes=[pltpu.VMEM((2, t_lat, pk), F32), pltpu.VMEM((2, t_lat, pv), F32),
                        pltpu.VMEM((2, pv, pk), F32)],
        compiler_params=_params("arbitrary", "arbitrary"),
        name="gla",
    )(z, z, z, z, zg, z, z, z, z, zg, gate_w, gate_b.reshape(2, 1, GLA_HEADS * GLA_DK),
      gla_g.reshape(1, GLA_DV), *c_args)


def _short_conv_block(zb_ref, zc_ref, zx_ref, w_ref, b_ref, o_ref):
    u = zc_ref[...].astype(F32) * zx_ref[...].astype(F32)
    t = u.shape[0]
    row = lax.broadcasted_iota(jnp.int32, u.shape, 0)
    prev = jnp.where(row >= 1, pltpu.roll(u, 1, 0), 0.0)
    nxt = jnp.where(row < t - 1, pltpu.roll(u, t - 1, 0), 0.0)
    y = prev * w_ref[0:1, :] + u * w_ref[1:2, :] + nxt * w_ref[2:3, :] + b_ref[...]
    o_ref[...] = (zb_ref[...].astype(F32) * y).astype(o_ref.dtype)


def _rope_tables(t):
    rows = t // GRID_W
    row = jnp.repeat(jnp.arange(rows), GRID_W).astype(F32)
    col = jnp.tile(jnp.arange(GRID_W), rows).astype(F32)
    n = ROPE_AXIS_DIM // 2
    inv = ROPE_THETA ** (-jnp.arange(n, dtype=F32) / n)
    ar, ac = row[:, None] * inv, col[:, None] * inv
    ang = jnp.concatenate([ar, ar, ac, ac], axis=-1)
    sign = jnp.tile(jnp.concatenate([-jnp.ones((n,), F32), jnp.ones((n,), F32)]), 2)
    cos = jnp.cos(ang)
    sin = jnp.sin(ang) * sign
    return jnp.tile(cos, (1, 2)), jnp.tile(sin, (1, 2))


def _rope(x, cos, sin):
    lane = lax.broadcasted_iota(jnp.int32, x.shape, 1)
    n = ROPE_AXIS_DIM // 2
    w = x.shape[1]
    partner = jnp.where((lane & (2 * n - 1)) < n, pltpu.roll(x, w - n, 1), pltpu.roll(x, n, 1))
    return x * cos + partner * sin


ATTN_ONES_ROWS = 16
ATTN_HEADS_PER_STEP = 2


def _attn_kernel(*refs, t_lat, lam_init, tq, n_cast):
    lam_ref, q_ref, kl_ref, vl_ref, kc_ref, vc_ref, cos_ref, sin_ref, dn_ref = refs[:9]
    conv_in = refs[9:14]
    n_in = 14 + n_cast
    o_ref, conv_ref = refs[n_in:n_in + 2]
    k_scr, vt_scr, s0_scr, m0_scr, s1_scr, m1_scr = refs[-6:]
    _cast_blocks(refs[14:n_in], refs[n_in + 2:n_in + 2 + n_cast])

    @pl.when(pl.program_id(1) == 0)
    def _():
        _short_conv_block(*conv_in, conv_ref)

    hw = 2 * HEAD_DIM
    t_all = k_scr.shape[1]

    for hd in range(ATTN_HEADS_PER_STEP):
        cols = slice(hd * hw, (hd + 1) * hw)
        k_scr[hd, 0:t_lat, :] = _rope(kl_ref[:, cols].astype(F32), cos_ref[...], sin_ref[...]).astype(BF16)
        k_scr[hd, t_lat:, :] = kc_ref[:, cols].astype(BF16)
        vt_scr[hd, 0:hw, 0:t_lat] = vl_ref[:, cols].astype(F32).T.astype(BF16)
        vt_scr[hd, 0:hw, t_lat:] = vc_ref[:, cols].astype(F32).T.astype(BF16)
        vt_scr[hd, hw:, :] = jnp.ones((ATTN_ONES_ROWS, t_all), BF16)

    lv = lam_ref[...]
    lam = (jnp.exp(jnp.sum(lv[0:1] * lv[1:2], axis=-1, keepdims=True))
           - jnp.exp(jnp.sum(lv[2:3] * lv[3:4], axis=-1, keepdims=True)) + lam_init)
    lane = lax.broadcasted_iota(jnp.int32, (tq, hw), 1)

    n_blocks = t_lat // tq
    slots = ((s0_scr, m0_scr), (s1_scr, m1_scr))

    def scores(hd, i, slot):
        rows = pl.ds(pl.multiple_of(i * tq, tq), tq)
        q = (_rope(q_ref[rows, hd * hw:(hd + 1) * hw].astype(F32), cos_ref[rows, :], sin_ref[rows, :])
             * (HEAD_DIM ** -0.5 * math.log2(math.e)))
        s_scr, m_scr = slots[slot]
        for half in range(2):
            qb = jnp.where((lane // HEAD_DIM) == half, q, 0.0).astype(BF16)
            s = _dot_nt(k_scr[hd], qb)
            s_scr[half] = s
            m_scr[half] = jnp.max(s, axis=0, keepdims=True)

    def outputs(hd, i, slot):
        rows = pl.ds(pl.multiple_of(i * tq, tq), tq)
        s_scr, m_scr = slots[slot]
        outs = []
        for half in range(2):
            p = jnp.exp2(s_scr[half] - m_scr[half]).astype(BF16)
            acc = _dot(vt_scr[hd], p)
            outs.append(acc[0:hw, :] / acc[hw:hw + 1, :])
        o = (outs[0] - lam * outs[1]).T
        o_ref[rows, hd * hw:(hd + 1) * hw] = (_rms(o) * dn_ref[...] * (1.0 - lam_init)).astype(o_ref.dtype)

    assert n_blocks % 2 == 0
    scores(0, 0, 0)
    for hd in range(ATTN_HEADS_PER_STEP):
        def body(j, carry, hd=hd):
            scores(hd, 2 * j + 1, 1)
            outputs(hd, 2 * j, 0)
            scores(hd, 2 * j + 2, 0)
            outputs(hd, 2 * j + 1, 1)
            return carry

        lax.fori_loop(0, n_blocks // 2 - 1, body, 0)
        scores(hd, n_blocks - 1, 1)
        outputs(hd, n_blocks - 2, 0)
        if hd + 1 < ATTN_HEADS_PER_STEP:
            scores(hd + 1, 0, 0)
        outputs(hd, n_blocks - 1, 1)


def _odd_mixers(z, lam_vecs, dnorm, lam_init, conv_w, conv_b, bn, t_lat, t_ctx, ctx_row0, cast=()):
    hw = 2 * HEAD_DIM
    wb = ATTN_HEADS_PER_STEP * hw
    q0 = 3 * CONV_WIDTH // wb
    k0 = q0 + DIFF_HEADS // ATTN_HEADS_PER_STEP
    v0 = k0 + DIFF_HEADS // ATTN_HEADS_PER_STEP
    tq = 256
    t_all = t_lat + t_ctx
    coff = ctx_row0 // t_ctx
    cos, sin = _rope_tables(t_lat)
    c_args, c_in, c_shape, c_out = _cast_riders(cast, lambda b, p: b, bn)
    outs = pl.pallas_call(
        functools.partial(_attn_kernel, t_lat=t_lat, lam_init=lam_init, tq=tq, n_cast=len(cast)),
        out_shape=[jax.ShapeDtypeStruct((bn * t_lat, DIFF_HEADS * DIFF_DV), BF16),
                   jax.ShapeDtypeStruct((bn * t_lat, CONV_WIDTH), BF16)] + c_shape,
        grid=(bn, DIFF_HEADS // ATTN_HEADS_PER_STEP),
        in_specs=[
            pl.BlockSpec((4, HEAD_DIM), lambda b, p: (0, 0)),
            pl.BlockSpec((t_lat, wb), lambda b, p: (b, q0 + p)),
            pl.BlockSpec((t_lat, wb), lambda b, p: (b, k0 + p)),
            pl.BlockSpec((t_lat, wb), lambda b, p: (b, v0 + p)),
            pl.BlockSpec((t_ctx, wb), lambda b, p: (coff + b, k0 + p)),
            pl.BlockSpec((t_ctx, wb), lambda b, p: (coff + b, v0 + p)),
            pl.BlockSpec((t_lat, hw), lambda b, p: (0, 0)),
            pl.BlockSpec((t_lat, hw), lambda b, p: (0, 0)),
            pl.BlockSpec((1, DIFF_DV), lambda b, p: (0, 0)),
            pl.BlockSpec((t_lat, CONV_WIDTH), lambda b, p: (b, 0)),
            pl.BlockSpec((t_lat, CONV_WIDTH), lambda b, p: (b, 1)),
            pl.BlockSpec((t_lat, CONV_WIDTH), lambda b, p: (b, 2)),
            pl.BlockSpec((3, CONV_WIDTH), lambda b, p: (0, 0)),
            pl.BlockSpec((1, CONV_WIDTH), lambda b, p: (0, 0)),
        ] + c_in,
        out_specs=[pl.BlockSpec((t_lat, wb), lambda b, p: (b, p)),
                   pl.BlockSpec((t_lat, CONV_WIDTH), lambda b, p: (b, 0))] + c_out,
        scratch_shapes=[pltpu.VMEM((ATTN_HEADS_PER_STEP, t_all, hw), BF16),
                        pltpu.VMEM((ATTN_HEADS_PER_STEP, hw + ATTN_ONES_ROWS, t_all), BF16),
                        pltpu.VMEM((2, t_all, tq), F32), pltpu.VMEM((2, 1, tq), F32),
                        pltpu.VMEM((2, t_all, tq), F32), pltpu.VMEM((2, 1, tq), F32)],
        compiler_params=_params("arbitrary", "arbitrary"),
        name="diff_attn",
    )(lam_vecs, z, z, z, z, z, cos, sin, dnorm.reshape(1, DIFF_DV), z, z, z, conv_w,
      conv_b.reshape(1, CONV_WIDTH), *c_args)
    return outs


def kernel(x, c, ctx, c_ctx, ada_w, ada_b, norm_ffn1, norm_mix, norm_ffn2, ffn1_w_in, ffn1_w_out, ffn2_w_in,
           ffn2_w_out, mix_w_out, even_w_in, gla_gate_w, gla_gate_b, gla_norm, odd_w_in, conv_w, conv_b,
           lambda_q1, lambda_k1, lambda_q2, lambda_k2, diff_norm, final_norm):
    assert DEPTH == 2
    bn, t_lat, d = x.shape
    t_ctx = ctx.shape[1]
    assert bn < MOD_ROWS
    ctx_row = bn

    cond = jnp.concatenate([c, c_ctx[None, :], jnp.zeros((MOD_ROWS - bn - 1, d), F32)], axis=0)
    n_mod_steps = DEPTH * MOD_TILES
    mods, w1i0, w1o0, w_even = _modulation(
        cond, ada_w, ada_b, cast=((ffn1_w_in, 0, n_mod_steps), (ffn1_w_out, 0, n_mod_steps),
                                  (even_w_in, 0, n_mod_steps)))

    tm = 512
    n_lat = bn * t_lat
    lat_blocks = n_lat // tm
    lat_row = lambda i: i // (t_lat // tm)
    all_row = lambda i: jnp.where(i < lat_blocks, i // (t_lat // tm), ctx_row)
    g1 = norm_ffn1.reshape(DEPTH, 1, d)
    gm = norm_mix.reshape(DEPTH, 1, d)
    g2 = norm_ffn2.reshape(DEPTH, 1, d)
    cast_gla = ((ffn2_w_in, 0, bn), (ffn2_w_out, 0, bn), (mix_w_out, 0, bn))
    cast_ffn = ((ffn1_w_in, 1, FFN_CAST_STEPS), (ffn1_w_out, 1, FFN_CAST_STEPS // 2), (odd_w_in, 0, FFN_CAST_STEPS))
    cast_attn = ((ffn2_w_in, 1, bn), (ffn2_w_out, 1, bn), (mix_w_out, 1, bn))

    proj0 = (gm, w_even, 2 * GLA_GATE_RANK)
    h, z, zg = _ffn((x.reshape(n_lat, d), ctx.reshape(bn * t_ctx, d)), mods, 0, 0, all_row, g1, w1i0, w1o0,
                    tm=tm, n_first=lat_blocks, proj=proj0)
    yf_l = _fourier(z, t_lat, bn)
    yf_c = _fourier(z, t_ctx, bn, row0=n_lat)
    yg_c, yg_l, w2i0, w2o0, wmix0 = _gla(z, zg, gla_gate_w[0], gla_gate_b[0], gla_norm[0], bn, t_ctx, t_lat, n_lat,
                                         cast=cast_gla)
    h, w1i1, w1o1, w_odd = _ffn(h, mods, 0, 6, all_row, g2, w2i0, w2o0, tm=tm, n_first=lat_blocks,
                                mix=((yf_l, yf_c), (yg_l, yg_c), wmix0), cast=cast_ffn)

    h, z = _ffn(h, mods, 1, 0, all_row, g1, w1i1, w1o1, tm=tm, proj=(gm, w_odd, 0))
    lam_init = 0.8 - 0.6 * math.exp(-0.3 * 1)
    lam_vecs = jnp.stack([lambda_q1[0], lambda_k1[0], lambda_q2[0], lambda_k2[0]]).astype(F32)
    y_att, y_conv, w2i1, w2o1, wmix1 = _odd_mixers(z, lam_vecs, diff_norm[0], lam_init, conv_w[0], conv_b[0], bn,
                                                   t_lat, t_ctx, n_lat, cast=cast_attn)
    h = _ffn(h, mods, 1, 6, lat_row, g2, w2i1, w2o1, tm=tm, mix=(y_conv, y_att, wmix1),
             final_gain=final_norm.reshape(1, d), n_rows=n_lat)
    return h.reshape(bn, t_lat, d)
```

```python
import functools
import math

import numpy as np
import jax
import jax.numpy as jnp
from jax import lax
from jax.experimental import pallas as pl
from jax.experimental.pallas import tpu as pltpu

D_MODEL = 1024
DEPTH = 2
GRID_W = 64
HEAD_DIM = 64
N_MOD = 9
FFN_HIDDEN = 2816
NORM_EPS = 1e-6
FOURIER_GROUPS = 4
FOURIER_WIDTH = FOURIER_GROUPS * HEAD_DIM
GLA_HEADS = 6
GLA_DK = 64
GLA_DV = 128
GLA_GATE_RANK = 16
GLA_TAU = 16.0
GLA_CHUNK = 64
GLA_GROUP = 8
FFN_TILE = 256
FFN_CAST_STEPS = 32
PROJ_TILE = 512
CONV_WIDTH = 4 * HEAD_DIM
DIFF_HEADS = 6
DIFF_DV = 2 * HEAD_DIM
ROPE_THETA = 10000.0
ROPE_AXIS_DIM = HEAD_DIM // 2

EVEN_MAIN = FOURIER_WIDTH + 2 * GLA_HEADS * GLA_DK + 2 * GLA_HEADS * GLA_DV
ODD_IN = 3 * CONV_WIDTH + 3 * DIFF_HEADS * DIFF_DV

MOD_ROWS = 16
MOD_TILES = 4
VMEM_LIMIT = 48 * 1024 * 1024

F32 = jnp.float32
BF16 = jnp.bfloat16


def _params(*sem):
    return pltpu.CompilerParams(dimension_semantics=sem, vmem_limit_bytes=VMEM_LIMIT)


def _sigmoid(x):
    return 1.0 / (1.0 + jnp.exp(-x))


def _dot(a, b):
    return jnp.dot(a, b, preferred_element_type=F32)


def _dot_nt(a, b):
    return lax.dot_general(a, b, (((1,), (1,)), ((), ())), preferred_element_type=F32)


def _dot_tn(a, b):
    return lax.dot_general(a, b, (((0,), (0,)), ((), ())), preferred_element_type=F32)


def _rms(x):
    return x * lax.rsqrt(jnp.mean(x * x, axis=-1, keepdims=True) + NORM_EPS)


def _mod_kernel(*refs, n_cast):
    cond_ref, w_ref, b_ref = refs[:3]
    o_ref = refs[3 + n_cast]
    _cast_blocks(refs[3:3 + n_cast], refs[4 + n_cast:])
    c = cond_ref[...]
    s = c * _sigmoid(c)
    w = w_ref[...]
    w_hi = w.astype(BF16)
    w_lo = (w - w_hi.astype(F32)).astype(BF16)
    s_hi = s.astype(BF16)
    s_lo = (s - s_hi.astype(F32)).astype(BF16)
    o_ref[...] = _dot(s_hi, w_hi) + _dot(s_lo, w_hi) + _dot(s_hi, w_lo) + b_ref[...]


def _modulation(cond, ada_w, ada_b, cast=()):
    n = N_MOD * D_MODEL
    tn = n // MOD_TILES
    c_args, c_in, c_shape, c_out = _cast_riders(cast, lambda l, j: l * MOD_TILES + j, DEPTH * MOD_TILES)
    outs = pl.pallas_call(
        functools.partial(_mod_kernel, n_cast=len(cast)),
        out_shape=[jax.ShapeDtypeStruct((DEPTH, MOD_ROWS, n), F32)] + c_shape,
        grid=(DEPTH, MOD_TILES),
        in_specs=[
            pl.BlockSpec((MOD_ROWS, D_MODEL), lambda l, j: (0, 0)),
            pl.BlockSpec((None, D_MODEL, tn), lambda l, j: (l, 0, j)),
            pl.BlockSpec((None, 1, tn), lambda l, j: (l, 0, j)),
        ] + c_in,
        out_specs=[pl.BlockSpec((None, MOD_ROWS, tn), lambda l, j: (l, 0, j))] + c_out,
        compiler_params=_params("arbitrary", "arbitrary"),
        name="modulation",
    )(cond, ada_w, ada_b.reshape(DEPTH, 1, n), *c_args)
    return (outs[0].reshape(DEPTH, MOD_ROWS, N_MOD, 1, D_MODEL), *outs[1:])


def _mod_spec(layer, k, row_of_block):
    return pl.BlockSpec((None, None, None, 1, D_MODEL),
                        lambda i, *_: (layer, row_of_block(i), k, 0, 0))


def _gain_spec(layer):
    return pl.BlockSpec((None, 1, D_MODEL), lambda i, *_: (layer, 0, 0))


def _resident(block_shape, index_map):
    return pl.BlockSpec(block_shape, index_map, pipeline_mode=pl.Buffered(1))


def _cast_riders(sources, step_of, n_steps):
    args, in_specs, out_shape, out_specs = [], [], [], []
    for arr, lead, nb in sources:
        _, r, c = arr.shape
        rows, stride = r // nb, n_steps // nb
        assert rows * nb == r and stride * nb == n_steps and rows % 16 == 0
        idx = lambda *g, nb=nb, stride=stride: jnp.minimum(step_of(*g) // stride, nb - 1)
        args.append(arr)
        in_specs.append(pl.BlockSpec((None, rows, c), lambda *g, idx=idx, lead=lead: (lead, idx(*g), 0)))
        out_shape.append(jax.ShapeDtypeStruct((r, c), BF16))
        out_specs.append(pl.BlockSpec((rows, c), lambda *g, idx=idx: (idx(*g), 0)))
    return args, in_specs, out_shape, out_specs


def _cast_blocks(src_refs, dst_refs):
    for src, dst in zip(src_refs, dst_refs):
        dst[...] = src[...].astype(BF16)


def _ffn_kernel(*refs, with_mix, with_proj, with_gate, with_final, paired, n_first, n_cast, proj_nt):
    it = iter(refs)
    pairs = iter(paired)

    def rows_in():
        a = next(it)
        if not next(pairs):
            return a[...]
        b = next(it)
        return jnp.where(pl.program_id(0) < n_first, a[...], b[...])

    x = rows_in()
    if with_mix:
        ma, mb = rows_in(), rows_in()
        wm_ref, gm_ref = next(it), next(it)
    sh_ref, sc_ref, g_ref, gain_ref, wi_ref, wo_ref = (next(it) for _ in range(6))
    if with_proj:
        psh_ref, psc_ref, pgain_ref, wp_ref = (next(it) for _ in range(4))
    if with_final:
        fn_ref = next(it)
    cast_src = [next(it) for _ in range(n_cast)]
    o_ref = next(it)
    if with_proj:
        z_ref = next(it)
    if with_gate:
        zg_ref = next(it)
    cast_dst = [next(it) for _ in range(n_cast)]
    a_scr = next(it)
    _cast_blocks(cast_src, cast_dst)

    if with_mix:
        ka = ma.shape[1]
        x = x + gm_ref[...] * (_dot(ma, wm_ref[0:ka, :]) + _dot(mb, wm_ref[ka:, :]))
    xn = (_rms(x) * gain_ref[...] * (1.0 + sc_ref[...]) + sh_ref[...]).astype(BF16)
    for j in range(FFN_HIDDEN // FFN_TILE):
        lo = j * FFN_TILE
        g = _dot(xn, wi_ref[:, lo:lo + FFN_TILE])
        u = _dot(xn, wi_ref[:, FFN_HIDDEN + lo:FFN_HIDDEN + lo + FFN_TILE])
        a_scr[:, lo:lo + FFN_TILE] = (g * _sigmoid(g) * u).astype(BF16)
    out = x + (0.5 * g_ref[...]) * _dot(a_scr[...], wo_ref[...])
    if with_final:
        out = _rms(out) * fn_ref[...]
    o_ref[...] = out
    if with_proj:
        xm = (_rms(out) * pgain_ref[...] * (1.0 + psc_ref[...]) + psh_ref[...]).astype(BF16)
        n_out = z_ref.shape[1]
        project = (lambda lo, hi: _dot_nt(xm, wp_ref[lo:hi, :])) if proj_nt else (lambda lo, hi: _dot(xm, wp_ref[:, lo:hi]))
        for lo in range(0, n_out, PROJ_TILE):
            z_ref[:, lo:lo + PROJ_TILE] = project(lo, lo + PROJ_TILE).astype(z_ref.dtype)
        if with_gate:
            zg_ref[...] = project(n_out, n_out + zg_ref.shape[1])


def _ffn(h, mods, layer, mod_base, row_of_block, gain, w_in, w_out, *, tm, n_first=None, n_rows=None, mix=None,
         proj=None, final_gain=None, cast=()):
    m = n_rows or (sum(a.shape[0] for a in h) if isinstance(h, tuple) else h.shape[0])
    row = lambda i: (i, 0)
    args, specs, paired = [], [], []

    def add_rows(a):
        paired.append(isinstance(a, tuple))
        if paired[-1]:
            first, second = a
            assert first.shape[0] == n_first * tm
            args.extend([first, second])
            specs.extend([pl.BlockSpec((tm, first.shape[1]), lambda i: (jnp.minimum(i, n_first - 1), 0)),
                          pl.BlockSpec((tm, second.shape[1]), lambda i: (jnp.maximum(i - n_first, 0), 0))])
        else:
            args.append(a)
            specs.append(pl.BlockSpec((tm, a.shape[1]), row))

    add_rows(h)
    if mix is not None:
        ma, mb, w_mix = mix
        add_rows(ma)
        add_rows(mb)
        args += [w_mix, mods]
        specs += [_resident(w_mix.shape, lambda i: (0, 0)), _mod_spec(layer, 5, row_of_block)]
    args += [mods, mods, mods, gain, w_in, w_out]
    specs += [
        _mod_spec(layer, mod_base, row_of_block),
        _mod_spec(layer, mod_base + 1, row_of_block),
        _mod_spec(layer, mod_base + 2, row_of_block),
        _gain_spec(layer),
        _resident((D_MODEL, 2 * FFN_HIDDEN), lambda i: (0, 0)),
        _resident((FFN_HIDDEN, D_MODEL), lambda i: (0, 0)),
    ]
    out_shape = [jax.ShapeDtypeStruct((m, D_MODEL), F32)]
    out_specs = [pl.BlockSpec((tm, D_MODEL), row)]
    n_gate, proj_nt = 0, False
    if proj is not None:
        pgain, wp, n_gate, proj_nt = proj
        n_out = wp.shape[0 if proj_nt else 1] - n_gate
        assert n_out % PROJ_TILE == 0
        args += [mods, mods, pgain, wp]
        specs += [_mod_spec(layer, 3, row_of_block), _mod_spec(layer, 4, row_of_block), _gain_spec(layer),
                  _resident(wp.shape, lambda i: (0, 0))]
        out_shape.append(jax.ShapeDtypeStruct((m, n_out), BF16))
        out_specs.append(pl.BlockSpec((tm, n_out), row))
        if n_gate:
            out_shape.append(jax.ShapeDtypeStruct((m, n_gate), F32))
            out_specs.append(pl.BlockSpec((tm, n_gate), row))
    if final_gain is not None:
        args.append(final_gain)
        specs.append(pl.BlockSpec((1, D_MODEL), lambda i: (0, 0)))
    c_args, c_in, c_shape, c_out = _cast_riders(cast, lambda i: i, FFN_CAST_STEPS)
    assert not cast or m // tm >= FFN_CAST_STEPS
    args += c_args
    specs += c_in
    out_shape += c_shape
    out_specs += c_out
    outs = pl.pallas_call(
        functools.partial(_ffn_kernel, with_mix=mix is not None, with_proj=proj is not None, with_gate=n_gate > 0,
                          with_final=final_gain is not None, paired=tuple(paired), n_first=n_first,
                          n_cast=len(cast), proj_nt=proj_nt),
        out_shape=out_shape,
        grid=(m // tm,),
        in_specs=specs,
        out_specs=out_specs,
        scratch_shapes=[pltpu.VMEM((tm, FFN_HIDDEN), BF16)],
        compiler_params=_params("arbitrary" if cast else "parallel"),
        name="ffn",
    )(*args)
    return outs if len(outs) > 1 else outs[0]


def _dft_tables(t, tq):
    k = (np.arange(tq, dtype=np.int64)[:, None] * np.arange(t, dtype=np.int64)[None, :]) % t
    ang = 2.0 * np.pi * k.astype(np.float64) / t
    pos = np.concatenate([np.cos(ang), -np.sin(ang)], axis=1).astype(np.float32)
    kc = (np.arange(HEAD_DIM)[:, None] * np.arange(HEAD_DIM)[None, :]) % HEAD_DIM
    angc = 2.0 * np.pi * kc.astype(np.float64) / HEAD_DIM
    eye = np.eye(FOURIER_GROUPS)
    chan = np.concatenate([np.kron(eye, np.cos(angc)), np.kron(eye, np.sin(angc))], axis=1).astype(np.float32)
    return pos, chan


def _dft_kernel(p_ref, z_ref, c_ref, o_ref, p_scr, ab_scr, *, scale, quarter_turns):
    i, b = pl.program_id(0), pl.program_id(1)
    t = p_ref.shape[1] // 2

    @pl.when(b == 0)
    def _():
        col = lax.broadcasted_iota(jnp.int32, (1, t), 1)
        k = (i * quarter_turns * col) & 3
        ca = jnp.where(k == 0, 1.0, jnp.where(k == 2, -1.0, 0.0))
        sa = jnp.where(k == 1, 1.0, jnp.where(k == 3, -1.0, 0.0))
        c0, n0 = p_ref[:, 0:t], p_ref[:, t:]
        p_scr[:, 0:t] = (ca * c0 + sa * n0).astype(BF16)
        p_scr[:, t:] = (ca * n0 - sa * c0).astype(BF16)

    @pl.when(i == 0)
    def _():
        ab = _dot(z_ref[...], c_ref[...].astype(BF16))
        ab_scr[b, 0:t, :] = ab[:, :FOURIER_WIDTH].astype(BF16)
        ab_scr[b, t:, :] = ab[:, FOURIER_WIDTH:].astype(BF16)

    o_ref[...] = (_dot(p_scr[...], ab_scr[b]) * scale).astype(BF16)


def _fourier(z, t, bn, row0=0):
    tq = min(t, 512)
    assert (4 * tq) % t == 0
    pos, chan = _dft_tables(t, tq)
    off = row0 // t
    return pl.pallas_call(
        functools.partial(_dft_kernel, scale=1.0 / math.sqrt(t * HEAD_DIM), quarter_turns=4 * tq // t),
        out_shape=jax.ShapeDtypeStruct((bn * t, FOURIER_WIDTH), BF16),
        grid=(t // tq, bn),
        in_specs=[_resident((tq, 2 * t), lambda i, b: (0, 0)),
                  pl.BlockSpec((t, FOURIER_WIDTH), lambda i, b: (off + b, 0)),
                  pl.BlockSpec((FOURIER_WIDTH, 2 * FOURIER_WIDTH), lambda i, b: (0, 0))],
        out_specs=pl.BlockSpec((tq, FOURIER_WIDTH), lambda i, b: (b * (t // tq) + i, 0)),
        scratch_shapes=[pltpu.VMEM((tq, 2 * t), BF16), pltpu.VMEM((bn, 2 * t, FOURIER_WIDTH), BF16)],
        compiler_params=_params("arbitrary", "arbitrary"),
        name="dft",
    )(jnp.asarray(pos), z, jnp.asarray(chan))


def _gla_segment(q_ref, k_ref, v_ref, r_ref, g_ref, y_ref, gw_ref, gb_ref, gn_ref,
                 b_scr, o_scr, s_scr, n_rows):
    c_len = GLA_CHUNK
    n = n_rows // c_len
    pair_k = 2 * GLA_DK
    pair_v = 2 * GLA_DV

    for d in range(2):
        zg = g_ref[:, d * GLA_GATE_RANK:(d + 1) * GLA_GATE_RANK]
        b_scr[d, 0:n_rows, :] = _dot(zg.astype(BF16), gw_ref[d].astype(BF16)) + gb_ref[d]

    lane = lax.broadcasted_iota(jnp.int32, (c_len, pair_k), 1)
    col = lax.broadcasted_iota(jnp.int32, (c_len, pair_v), 1)
    ti = lax.broadcasted_iota(jnp.int32, (c_len, c_len), 0)
    tj = lax.broadcasted_iota(jnp.int32, (c_len, c_len), 1)
    blk = (lax.broadcasted_iota(jnp.int32, (pair_v, pair_k), 0) // GLA_DV
           == lax.broadcasted_iota(jnp.int32, (pair_v, pair_k), 1) // GLA_DK)

    group = math.gcd(GLA_GROUP, n)
    n_it = n // group
    keep2 = [jnp.concatenate([m, m], axis=0) for m in (tj <= ti, tj >= ti)]
    ti2 = lax.broadcasted_iota(jnp.int32, (c_len, 2 * c_len), 0)
    tj2 = lax.broadcasted_iota(jnp.int32, (c_len, 2 * c_len), 1) & (c_len - 1)
    tri2 = [jnp.where(m, 1.0, 0.0).astype(BF16) for m in (tj2 <= ti2, tj2 >= ti2)]
    gn = gn_ref[...]

    def finish(rows, o):
        for hh in range(2):
            sl = slice(hh * GLA_DV, (hh + 1) * GLA_DV)
            rr = r_ref[rows, sl].astype(F32)
            y_ref[rows, sl] = (_rms(o[:, sl]) * gn * (rr * _sigmoid(rr))).astype(y_ref.dtype)

    def chunk_rows(c):
        start = c * c_len
        return pl.ds(start if isinstance(start, int) else pl.multiple_of(start, c_len), c_len)

    def body(i, carry, phase):
        units = []
        for d in range(2):
            for g in range(group):
                c = i * group + g if d == 0 else n - 1 - (i * group + g)
                rows = chunk_rows(c)
                logit = b_scr[d, rows, :]
                lg = (jnp.minimum(logit, 0.0) - jnp.log(1.0 + jnp.exp(-jnp.abs(logit)))) * (1.0 / GLA_TAU)
                hi = lg.astype(BF16)
                rest = (lg - hi.astype(F32)).astype(BF16)
                units.append(dict(d=d, rows=rows, b=_dot(tri2[d], jnp.concatenate([hi, rest], axis=0))))
        for u in units:
            d, rows, b = u["d"], u["rows"], u["b"]
            bl = b[c_len - 1:c_len, :] if d == 0 else b[0:1, :]
            k = k_ref[rows, :].astype(F32)
            v16 = v_ref[rows, :].astype(BF16)
            qd = q_ref[rows, :].astype(F32) * (GLA_DK ** -0.5) * jnp.exp(b)
            kd = (k * jnp.exp(-b)).astype(BF16)
            kdec = (k * jnp.exp(bl - b)).astype(BF16)
            q2 = jnp.concatenate([jnp.where((lane // GLA_DK) == hh, qd, 0.0) for hh in range(2)], axis=0)
            att = _dot_nt(q2.astype(BF16), kd)
            upd = _dot_tn(v16, kdec)
            u.update(qd=qd.astype(BF16), v16=v16, att=att, upd=upd, dec=jnp.exp(bl))
        for d in range(2):
            s = s_scr[d]
            for u in units:
                if u["d"] == d:
                    u["o"] = _dot_nt(u["qd"], s.astype(BF16))
                    s = jnp.where(blk, s * u["dec"] + u["upd"], 0.0)
            s_scr[d] = s
        for u in units:
            att = jnp.where(keep2[u["d"]], u["att"], 0.0).astype(BF16)
            pv = _dot(att, u["v16"])
            u["o"] = u["o"] + jnp.where((col // GLA_DV) == 0, pv[0:c_len, :], pv[c_len:, :])
            if phase == "first":
                o_scr[u["d"], u["rows"], :] = u["o"]
            elif phase == "second":
                finish(u["rows"], u["o"] + o_scr[1 - u["d"], u["rows"], :])
        if phase == "only":
            for g in range(group):
                finish(units[g]["rows"], units[g]["o"] + units[group + n - 1 - g]["o"])
        return carry

    if n_it == 1:
        body(0, 0, "only")
    else:
        assert n_it % 2 == 0
        lax.fori_loop(0, n_it // 2, functools.partial(body, phase="first"), 0)
        lax.fori_loop(n_it // 2, n_it, functools.partial(body, phase="second"), 0)


def _gla_kernel(*refs, t_ctx, t_lat, n_cast):
    qc_ref, kc_ref, vc_ref, rc_ref, gc_ref, ql_ref, kl_ref, vl_ref, rl_ref, gl_ref, gw_ref, gb_ref, gn_ref = refs[:13]
    yc_ref, yl_ref = refs[13 + n_cast:15 + n_cast]
    b_scr, o_scr, s_scr = refs[-3:]
    _cast_blocks(refs[13:13 + n_cast], refs[15 + n_cast:15 + 2 * n_cast])
    s_scr[...] = jnp.zeros_like(s_scr)
    _gla_segment(qc_ref, kc_ref, vc_ref, rc_ref, gc_ref, yc_ref, gw_ref, gb_ref, gn_ref,
                 b_scr, o_scr, s_scr, t_ctx)
    _gla_segment(ql_ref, kl_ref, vl_ref, rl_ref, gl_ref, yl_ref, gw_ref, gb_ref, gn_ref,
                 b_scr, o_scr, s_scr, t_lat)


def _gla(z, zg, gate_w, gate_b, gla_g, bn, t_ctx, t_lat, ctx_row0, cast=()):
    pk, pv = 2 * GLA_DK, 2 * GLA_DV
    q0 = FOURIER_WIDTH // pk
    k0 = (FOURIER_WIDTH + GLA_HEADS * GLA_DK) // pk
    v0 = (FOURIER_WIDTH + 2 * GLA_HEADS * GLA_DK) // pv
    r0 = (FOURIER_WIDTH + 2 * GLA_HEADS * GLA_DK + GLA_HEADS * GLA_DV) // pv

    def seg_specs(t, row0):
        off = row0 // t
        return [pl.BlockSpec((t, pk), lambda b, p: (off + b, q0 + p)),
                pl.BlockSpec((t, pk), lambda b, p: (off + b, k0 + p)),
                pl.BlockSpec((t, pv), lambda b, p: (off + b, v0 + p)),
                pl.BlockSpec((t, pv), lambda b, p: (off + b, r0 + p)),
                pl.BlockSpec((t, 2 * GLA_GATE_RANK), lambda b, p: (off + b, 0))]

    wdt = GLA_HEADS * GLA_DV
    c_args, c_in, c_shape, c_out = _cast_riders(cast, lambda b, p: b, bn)
    return pl.pallas_call(
        functools.partial(_gla_kernel, t_ctx=t_ctx, t_lat=t_lat, n_cast=len(cast)),
        out_shape=[jax.ShapeDtypeStruct((bn * t_ctx, wdt), BF16), jax.ShapeDtypeStruct((bn * t_lat, wdt), BF16)] + c_shape,
        grid=(bn, GLA_HEADS // 2),
        in_specs=seg_specs(t_ctx, ctx_row0) + seg_specs(t_lat, 0) + [
            pl.BlockSpec((2, GLA_GATE_RANK, pk), lambda b, p: (0, 0, p)),
            pl.BlockSpec((2, 1, pk), lambda b, p: (0, 0, p)),
            pl.BlockSpec((1, GLA_DV), lambda b, p: (0, 0))] + c_in,
        out_specs=[pl.BlockSpec((t_ctx, pv), lambda b, p: (b, p)),
                   pl.BlockSpec((t_lat, pv), lambda b, p: (b, p))] + c_out,
        scratch_shapes=[pltpu.VMEM((2, t_lat, pk), F32), pltpu.VMEM((2, t_lat, pv), F32),
                        pltpu.VMEM((2, pv, pk), F32)],
        compiler_params=_params("arbitrary", "arbitrary"),
        name="gla",
    )(z, z, z, z, zg, z, z, z, z, zg, gate_w, gate_b.reshape(2, 1, GLA_HEADS * GLA_DK),
      gla_g.reshape(1, GLA_DV), *c_args)


def _short_conv_block(zb_ref, zc_ref, zx_ref, w_ref, b_ref, o_ref):
    u = zc_ref[...].astype(F32) * zx_ref[...].astype(F32)
    t = u.shape[0]
    row = lax.broadcasted_iota(jnp.int32, u.shape, 0)
    prev = jnp.where(row >= 1, pltpu.roll(u, 1, 0), 0.0)
    nxt = jnp.where(row < t - 1, pltpu.roll(u, t - 1, 0), 0.0)
    y = prev * w_ref[0:1, :] + u * w_ref[1:2, :] + nxt * w_ref[2:3, :] + b_ref[...]
    o_ref[...] = (zb_ref[...].astype(F32) * y).astype(o_ref.dtype)


def _rope_tables(t):
    rows = t // GRID_W
    row = jnp.repeat(jnp.arange(rows), GRID_W).astype(F32)
    col = jnp.tile(jnp.arange(GRID_W), rows).astype(F32)
    n = ROPE_AXIS_DIM // 2
    inv = ROPE_THETA ** (-jnp.arange(n, dtype=F32) / n)
    ar, ac = row[:, None] * inv, col[:, None] * inv
    ang = jnp.concatenate([ar, ar, ac, ac], axis=-1)
    sign = jnp.tile(jnp.concatenate([-jnp.ones((n,), F32), jnp.ones((n,), F32)]), 2)
    cos = jnp.cos(ang)
    sin = jnp.sin(ang) * sign
    return jnp.tile(cos, (1, 2)), jnp.tile(sin, (1, 2))


def _rope(x, cos, sin):
    lane = lax.broadcasted_iota(jnp.int32, x.shape, 1)
    n = ROPE_AXIS_DIM // 2
    w = x.shape[1]
    partner = jnp.where((lane & (2 * n - 1)) < n, pltpu.roll(x, w - n, 1), pltpu.roll(x, n, 1))
    return x * cos + partner * sin


ATTN_ONES_ROWS = 16
ATTN_HEADS_PER_STEP = 2


def _attn_kernel(*refs, t_lat, lam_init, tq, n_cast):
    lam_ref, q_ref, kl_ref, vl_ref, kc_ref, vc_ref, cos_ref, sin_ref, dn_ref = refs[:9]
    conv_in = refs[9:14]
    n_in = 14 + n_cast
    o_ref, conv_ref = refs[n_in:n_in + 2]
    k_scr, vt_scr, s0_scr, m0_scr, s1_scr, m1_scr = refs[-6:]
    _cast_blocks(refs[14:n_in], refs[n_in + 2:n_in + 2 + n_cast])

    @pl.when(pl.program_id(1) == 0)
    def _():
        _short_conv_block(*conv_in, conv_ref)

    hw = 2 * HEAD_DIM
    t_all = k_scr.shape[1]

    for hd in range(ATTN_HEADS_PER_STEP):
        cols = slice(hd * hw, (hd + 1) * hw)
        k_scr[hd, 0:t_lat, :] = _rope(kl_ref[:, cols].astype(F32), cos_ref[...], sin_ref[...]).astype(BF16)
        k_scr[hd, t_lat:, :] = kc_ref[:, cols].astype(BF16)
        vt_scr[hd, 0:hw, 0:t_lat] = vl_ref[:, cols].astype(F32).T.astype(BF16)
        vt_scr[hd, 0:hw, t_lat:] = vc_ref[:, cols].astype(F32).T.astype(BF16)
        vt_scr[hd, hw:, :] = jnp.ones((ATTN_ONES_ROWS, t_all), BF16)

    lv = lam_ref[...]
    lam = (jnp.exp(jnp.sum(lv[0:1] * lv[1:2], axis=-1, keepdims=True))
           - jnp.exp(jnp.sum(lv[2:3] * lv[3:4], axis=-1, keepdims=True)) + lam_init)
    lane = lax.broadcasted_iota(jnp.int32, (tq, hw), 1)

    n_blocks = t_lat // tq
    slots = ((s0_scr, m0_scr), (s1_scr, m1_scr))

    def scores(hd, i, slot):
        rows = pl.ds(pl.multiple_of(i * tq, tq), tq)
        q = (_rope(q_ref[rows, hd * hw:(hd + 1) * hw].astype(F32), cos_ref[rows, :], sin_ref[rows, :])
             * (HEAD_DIM ** -0.5 * math.log2(math.e)))
        s_scr, m_scr = slots[slot]
        for half in range(2):
            qb = jnp.where((lane // HEAD_DIM) == half, q, 0.0).astype(BF16)
            s = _dot_nt(k_scr[hd], qb)
            s_scr[half] = s
            m_scr[half] = jnp.max(s, axis=0, keepdims=True)

    def outputs(hd, i, slot):
        rows = pl.ds(pl.multiple_of(i * tq, tq), tq)
        s_scr, m_scr = slots[slot]
        outs = []
        for half in range(2):
            p = jnp.exp2(s_scr[half] - m_scr[half]).astype(BF16)
            acc = _dot(vt_scr[hd], p)
            outs.append(acc[0:hw, :] / acc[hw:hw + 1, :])
        o = (outs[0] - lam * outs[1]).T
        o_ref[rows, hd * hw:(hd + 1) * hw] = (_rms(o) * dn_ref[...] * (1.0 - lam_init)).astype(o_ref.dtype)

    assert n_blocks % 2 == 0
    scores(0, 0, 0)
    for hd in range(ATTN_HEADS_PER_STEP):
        def body(j, carry, hd=hd):
            scores(hd, 2 * j + 1, 1)
            outputs(hd, 2 * j, 0)
            scores(hd, 2 * j + 2, 0)
            outputs(hd, 2 * j + 1, 1)
            return carry

        lax.fori_loop(0, n_blocks // 2 - 1, body, 0)
        scores(hd, n_blocks - 1, 1)
        outputs(hd, n_blocks - 2, 0)
        if hd + 1 < ATTN_HEADS_PER_STEP:
            scores(hd + 1, 0, 0)
        outputs(hd, n_blocks - 1, 1)


def _odd_mixers(z, lam_vecs, dnorm, lam_init, conv_w, conv_b, bn, t_lat, t_ctx, ctx_row0, cast=()):
    hw = 2 * HEAD_DIM
    wb = ATTN_HEADS_PER_STEP * hw
    q0 = 3 * CONV_WIDTH // wb
    k0 = q0 + DIFF_HEADS // ATTN_HEADS_PER_STEP
    v0 = k0 + DIFF_HEADS // ATTN_HEADS_PER_STEP
    tq = 256
    t_all = t_lat + t_ctx
    coff = ctx_row0 // t_ctx
    cos, sin = _rope_tables(t_lat)
    c_args, c_in, c_shape, c_out = _cast_riders(cast, lambda b, p: b, bn)
    outs = pl.pallas_call(
        functools.partial(_attn_kernel, t_lat=t_lat, lam_init=lam_init, tq=tq, n_cast=len(cast)),
        out_shape=[jax.ShapeDtypeStruct((bn * t_lat, DIFF_HEADS * DIFF_DV), BF16),
                   jax.ShapeDtypeStruct((bn * t_lat, CONV_WIDTH), BF16)] + c_shape,
        grid=(bn, DIFF_HEADS // ATTN_HEADS_PER_STEP),
        in_specs=[
            pl.BlockSpec((4, HEAD_DIM), lambda b, p: (0, 0)),
            pl.BlockSpec((t_lat, wb), lambda b, p: (b, q0 + p)),
            pl.BlockSpec((t_lat, wb), lambda b, p: (b, k0 + p)),
            pl.BlockSpec((t_lat, wb), lambda b, p: (b, v0 + p)),
            pl.BlockSpec((t_ctx, wb), lambda b, p: (coff + b, k0 + p)),
            pl.BlockSpec((t_ctx, wb), lambda b, p: (coff + b, v0 + p)),
            pl.BlockSpec((t_lat, hw), lambda b, p: (0, 0)),
            pl.BlockSpec((t_lat, hw), lambda b, p: (0, 0)),
            pl.BlockSpec((1, DIFF_DV), lambda b, p: (0, 0)),
            pl.BlockSpec((t_lat, CONV_WIDTH), lambda b, p: (b, 0)),
            pl.BlockSpec((t_lat, CONV_WIDTH), lambda b, p: (b, 1)),
            pl.BlockSpec((t_lat, CONV_WIDTH), lambda b, p: (b, 2)),
            pl.BlockSpec((3, CONV_WIDTH), lambda b, p: (0, 0)),
            pl.BlockSpec((1, CONV_WIDTH), lambda b, p: (0, 0)),
        ] + c_in,
        out_specs=[pl.BlockSpec((t_lat, wb), lambda b, p: (b, p)),
                   pl.BlockSpec((t_lat, CONV_WIDTH), lambda b, p: (b, 0))] + c_out,
        scratch_shapes=[pltpu.VMEM((ATTN_HEADS_PER_STEP, t_all, hw), BF16),
                        pltpu.VMEM((ATTN_HEADS_PER_STEP, hw + ATTN_ONES_ROWS, t_all), BF16),
                        pltpu.VMEM((2, t_all, tq), F32), pltpu.VMEM((2, 1, tq), F32),
                        pltpu.VMEM((2, t_all, tq), F32), pltpu.VMEM((2, 1, tq), F32)],
        compiler_params=_params("arbitrary", "arbitrary"),
        name="diff_attn",
    )(lam_vecs, z, z, z, z, z, cos, sin, dnorm.reshape(1, DIFF_DV), z, z, z, conv_w,
      conv_b.reshape(1, CONV_WIDTH), *c_args)
    return outs


def kernel(x, c, ctx, c_ctx, ada_w, ada_b, norm_ffn1, norm_mix, norm_ffn2, ffn1_w_in, ffn1_w_out, ffn2_w_in,
           ffn2_w_out, mix_w_out, even_w_in, gla_gate_w, gla_gate_b, gla_norm, odd_w_in, conv_w, conv_b,
           lambda_q1, lambda_k1, lambda_q2, lambda_k2, diff_norm, final_norm):
    assert DEPTH == 2
    bn, t_lat, d = x.shape
    t_ctx = ctx.shape[1]
    assert bn < MOD_ROWS
    ctx_row = bn

    cond = jnp.concatenate([c, c_ctx[None, :], jnp.zeros((MOD_ROWS - bn - 1, d), F32)], axis=0)
    n_mod_steps = DEPTH * MOD_TILES
    mods, w1i0, w1o0 = _modulation(cond, ada_w, ada_b,
                                   cast=((ffn1_w_in, 0, n_mod_steps), (ffn1_w_out, 0, n_mod_steps)))
    w_even_t = jnp.swapaxes(even_w_in[0], 0, 1).astype(BF16)

    tm = 512
    n_lat = bn * t_lat
    lat_blocks = n_lat // tm
    lat_row = lambda i: i // (t_lat // tm)
    all_row = lambda i: jnp.where(i < lat_blocks, i // (t_lat // tm), ctx_row)
    g1 = norm_ffn1.reshape(DEPTH, 1, d)
    gm = norm_mix.reshape(DEPTH, 1, d)
    g2 = norm_ffn2.reshape(DEPTH, 1, d)
    cast_gla = ((ffn2_w_in, 0, bn), (ffn2_w_out, 0, bn), (mix_w_out, 0, bn))
    cast_ffn = ((ffn1_w_in, 1, FFN_CAST_STEPS), (ffn1_w_out, 1, FFN_CAST_STEPS // 2), (odd_w_in, 0, FFN_CAST_STEPS))
    cast_attn = ((ffn2_w_in, 1, bn), (ffn2_w_out, 1, bn), (mix_w_out, 1, bn))

    proj0 = (gm, w_even_t, 2 * GLA_GATE_RANK, True)
    h, z, zg = _ffn((x.reshape(n_lat, d), ctx.reshape(bn * t_ctx, d)), mods, 0, 0, all_row, g1, w1i0, w1o0,
                    tm=tm, n_first=lat_blocks, proj=proj0)
    yf_l = _fourier(z, t_lat, bn)
    yf_c = _fourier(z, t_ctx, bn, row0=n_lat)
    yg_c, yg_l, w2i0, w2o0, wmix0 = _gla(z, zg, gla_gate_w[0], gla_gate_b[0], gla_norm[0], bn, t_ctx, t_lat, n_lat,
                                         cast=cast_gla)
    h, w1i1, w1o1, w_odd = _ffn(h, mods, 0, 6, all_row, g2, w2i0, w2o0, tm=tm, n_first=lat_blocks,
                                mix=((yf_l, yf_c), (yg_l, yg_c), wmix0), cast=cast_ffn)

    h, z = _ffn(h, mods, 1, 0, all_row, g1, w1i1, w1o1, tm=tm, proj=(gm, w_odd, 0, False))
    lam_init = 0.8 - 0.6 * math.exp(-0.3 * 1)
    lam_vecs = jnp.stack([lambda_q1[0], lambda_k1[0], lambda_q2[0], lambda_k2[0]]).astype(F32)
    y_att, y_conv, w2i1, w2o1, wmix1 = _odd_mixers(z, lam_vecs, diff_norm[0], lam_init, conv_w[0], conv_b[0], bn,
                                                   t_lat, t_ctx, n_lat, cast=cast_attn)
    h = _ffn(h, mods, 1, 6, lat_row, g2, w2i1, w2o1, tm=tm, mix=(y_conv, y_att, wmix1),
             final_gain=final_norm.reshape(1, d), n_rows=n_lat)
    return h.reshape(bn, t_lat, d)
```

```python
import functools
import math

import numpy as np
import jax
import jax.numpy as jnp
from jax import lax
from jax.experimental import pallas as pl
from jax.experimental.pallas import tpu as pltpu

D_MODEL = 1024
DEPTH = 2
GRID_W = 64
HEAD_DIM = 64
N_MOD = 9
FFN_HIDDEN = 2816
NORM_EPS = 1e-6
FOURIER_GROUPS = 4
FOURIER_WIDTH = FOURIER_GROUPS * HEAD_DIM
GLA_HEADS = 6
GLA_DK = 64
GLA_DV = 128
GLA_GATE_RANK = 16
GLA_TAU = 16.0
GLA_CHUNK = 64
GLA_GROUP = 8
FFN_TILE = 256
FFN_CAST_STEPS = 32
PROJ_TILE = 512
CONV_WIDTH = 4 * HEAD_DIM
DIFF_HEADS = 6
DIFF_DV = 2 * HEAD_DIM
ROPE_THETA = 10000.0
ROPE_AXIS_DIM = HEAD_DIM // 2

MOD_ROWS = 16
MOD_TILES = 4
VMEM_LIMIT = 48 * 1024 * 1024

F32 = jnp.float32
BF16 = jnp.bfloat16


def _params(*sem):
    return pltpu.CompilerParams(dimension_semantics=sem, vmem_limit_bytes=VMEM_LIMIT)


def _sigmoid(x):
    return 1.0 / (1.0 + jnp.exp(-x))


def _dot(a, b):
    return jnp.dot(a, b, preferred_element_type=F32)


def _dot_nt(a, b):
    return lax.dot_general(a, b, (((1,), (1,)), ((), ())), preferred_element_type=F32)


def _dot_tn(a, b):
    return lax.dot_general(a, b, (((0,), (0,)), ((), ())), preferred_element_type=F32)


def _rms(x):
    return x * lax.rsqrt(jnp.mean(x * x, axis=-1, keepdims=True) + NORM_EPS)


def _mod_kernel(*refs, n_cast):
    cond_ref, w_ref, b_ref = refs[:3]
    o_ref = refs[3 + n_cast]
    _cast_blocks(refs[3:3 + n_cast], refs[4 + n_cast:])
    c = cond_ref[...]
    s = c * _sigmoid(c)
    w = w_ref[...]
    w_hi = w.astype(BF16)
    w_lo = (w - w_hi.astype(F32)).astype(BF16)
    s_hi = s.astype(BF16)
    s_lo = (s - s_hi.astype(F32)).astype(BF16)
    o_ref[...] = _dot(s_hi, w_hi) + _dot(s_lo, w_hi) + _dot(s_hi, w_lo) + b_ref[...]


def _modulation(cond, ada_w, ada_b, cast=()):
    n = N_MOD * D_MODEL
    tn = n // MOD_TILES
    c_args, c_in, c_shape, c_out = _cast_riders(cast, lambda l, j: l * MOD_TILES + j, DEPTH * MOD_TILES)
    outs = pl.pallas_call(
        functools.partial(_mod_kernel, n_cast=len(cast)),
        out_shape=[jax.ShapeDtypeStruct((DEPTH, MOD_ROWS, n), F32)] + c_shape,
        grid=(DEPTH, MOD_TILES),
        in_specs=[
            pl.BlockSpec((MOD_ROWS, D_MODEL), lambda l, j: (0, 0)),
            pl.BlockSpec((None, D_MODEL, tn), lambda l, j: (l, 0, j)),
            pl.BlockSpec((None, 1, tn), lambda l, j: (l, 0, j)),
        ] + c_in,
        out_specs=[pl.BlockSpec((None, MOD_ROWS, tn), lambda l, j: (l, 0, j))] + c_out,
        compiler_params=_params("arbitrary", "arbitrary"),
        name="modulation",
    )(cond, ada_w, ada_b.reshape(DEPTH, 1, n), *c_args)
    return (outs[0].reshape(DEPTH, MOD_ROWS, N_MOD, 1, D_MODEL), *outs[1:])


def _mod_spec(layer, k, row_of_block):
    return pl.BlockSpec((None, None, None, 1, D_MODEL),
                        lambda i, *_: (layer, row_of_block(i), k, 0, 0))


def _gain_spec(layer):
    return pl.BlockSpec((None, 1, D_MODEL), lambda i, *_: (layer, 0, 0))


def _resident(block_shape, index_map):
    return pl.BlockSpec(block_shape, index_map, pipeline_mode=pl.Buffered(1))


def _cast_riders(sources, step_of, n_steps):
    args, in_specs, out_shape, out_specs = [], [], [], []
    for arr, lead, nb in sources:
        _, r, c = arr.shape
        rows, stride = r // nb, n_steps // nb
        assert rows * nb == r and stride * nb == n_steps and rows % 16 == 0
        idx = lambda *g, nb=nb, stride=stride: jnp.minimum(step_of(*g) // stride, nb - 1)
        args.append(arr)
        in_specs.append(pl.BlockSpec((None, rows, c), lambda *g, idx=idx, lead=lead: (lead, idx(*g), 0)))
        out_shape.append(jax.ShapeDtypeStruct((r, c), BF16))
        out_specs.append(pl.BlockSpec((rows, c), lambda *g, idx=idx: (idx(*g), 0)))
    return args, in_specs, out_shape, out_specs


def _cast_blocks(src_refs, dst_refs):
    for src, dst in zip(src_refs, dst_refs):
        dst[...] = src[...].astype(BF16)


def _ffn_kernel(*refs, with_mix, with_proj, with_gate, with_final, paired, n_first, n_cast, proj_nt):
    it = iter(refs)
    pairs = iter(paired)

    def rows_in():
        a = next(it)
        if not next(pairs):
            return a[...]
        b = next(it)
        return jnp.where(pl.program_id(0) < n_first, a[...], b[...])

    x = rows_in()
    if with_mix:
        ma, mb = rows_in(), rows_in()
        wm_ref, gm_ref = next(it), next(it)
    sh_ref, sc_ref, g_ref, gain_ref, wi_ref, wo_ref = (next(it) for _ in range(6))
    if with_proj:
        psh_ref, psc_ref, pgain_ref, wp_ref = (next(it) for _ in range(4))
    if with_final:
        fn_ref = next(it)
    cast_src = [next(it) for _ in range(n_cast)]
    o_ref = next(it)
    if with_proj:
        z_ref = next(it)
    if with_gate:
        zg_ref = next(it)
    cast_dst = [next(it) for _ in range(n_cast)]
    a_scr = next(it)
    _cast_blocks(cast_src, cast_dst)

    if with_mix:
        ka = ma.shape[1]
        x = x + gm_ref[...] * (_dot(ma, wm_ref[0:ka, :]) + _dot(mb, wm_ref[ka:, :]))
    xn = (_rms(x) * gain_ref[...] * (1.0 + sc_ref[...]) + sh_ref[...]).astype(BF16)
    for j in range(FFN_HIDDEN // FFN_TILE):
        lo = j * FFN_TILE
        g = _dot(xn, wi_ref[:, lo:lo + FFN_TILE])
        u = _dot(xn, wi_ref[:, FFN_HIDDEN + lo:FFN_HIDDEN + lo + FFN_TILE])
        a_scr[:, lo:lo + FFN_TILE] = (g * _sigmoid(g) * u).astype(BF16)
    out = x + (0.5 * g_ref[...]) * _dot(a_scr[...], wo_ref[...])
    if with_final:
        out = _rms(out) * fn_ref[...]
    o_ref[...] = out
    if with_proj:
        xm = (_rms(out) * pgain_ref[...] * (1.0 + psc_ref[...]) + psh_ref[...]).astype(BF16)
        n_out = z_ref.shape[1]
        project = (lambda lo, hi: _dot_nt(xm, wp_ref[lo:hi, :])) if proj_nt else (lambda lo, hi: _dot(xm, wp_ref[:, lo:hi]))
        for lo in range(0, n_out, PROJ_TILE):
            z_ref[:, lo:lo + PROJ_TILE] = project(lo, lo + PROJ_TILE).astype(z_ref.dtype)
        if with_gate:
            zg_ref[...] = project(n_out, n_out + zg_ref.shape[1])


def _ffn(h, mods, layer, mod_base, row_of_block, gain, w_in, w_out, *, tm, n_first=None, n_rows=None, mix=None,
         proj=None, final_gain=None, cast=()):
    m = n_rows or (sum(a.shape[0] for a in h) if isinstance(h, tuple) else h.shape[0])
    row = lambda i: (i, 0)
    args, specs, paired = [], [], []

    def add_rows(a):
        paired.append(isinstance(a, tuple))
        if paired[-1]:
            first, second = a
            assert first.shape[0] == n_first * tm
            args.extend([first, second])
            specs.extend([pl.BlockSpec((tm, first.shape[1]), lambda i: (jnp.minimum(i, n_first - 1), 0)),
                          pl.BlockSpec((tm, second.shape[1]), lambda i: (jnp.maximum(i - n_first, 0), 0))])
        else:
            args.append(a)
            specs.append(pl.BlockSpec((tm, a.shape[1]), row))

    add_rows(h)
    if mix is not None:
        ma, mb, w_mix = mix
        add_rows(ma)
        add_rows(mb)
        args += [w_mix, mods]
        specs += [_resident(w_mix.shape, lambda i: (0, 0)), _mod_spec(layer, 5, row_of_block)]
    args += [mods, mods, mods, gain, w_in, w_out]
    specs += [
        _mod_spec(layer, mod_base, row_of_block),
        _mod_spec(layer, mod_base + 1, row_of_block),
        _mod_spec(layer, mod_base + 2, row_of_block),
        _gain_spec(layer),
        _resident((D_MODEL, 2 * FFN_HIDDEN), lambda i: (0, 0)),
        _resident((FFN_HIDDEN, D_MODEL), lambda i: (0, 0)),
    ]
    out_shape = [jax.ShapeDtypeStruct((m, D_MODEL), F32)]
    out_specs = [pl.BlockSpec((tm, D_MODEL), row)]
    n_gate, proj_nt = 0, False
    if proj is not None:
        pgain, wp, n_gate, proj_nt = proj
        n_out = wp.shape[0 if proj_nt else 1] - n_gate
        assert n_out % PROJ_TILE == 0
        args += [mods, mods, pgain, wp]
        specs += [_mod_spec(layer, 3, row_of_block), _mod_spec(layer, 4, row_of_block), _gain_spec(layer),
                  _resident(wp.shape, lambda i: (0, 0))]
        out_shape.append(jax.ShapeDtypeStruct((m, n_out), BF16))
        out_specs.append(pl.BlockSpec((tm, n_out), row))
        if n_gate:
            out_shape.append(jax.ShapeDtypeStruct((m, n_gate), F32))
            out_specs.append(pl.BlockSpec((tm, n_gate), row))
    if final_gain is not None:
        args.append(final_gain)
        specs.append(pl.BlockSpec((1, D_MODEL), lambda i: (0, 0)))
    c_args, c_in, c_shape, c_out = _cast_riders(cast, lambda i: i, FFN_CAST_STEPS)
    assert not cast or m // tm >= FFN_CAST_STEPS
    args += c_args
    specs += c_in
    out_shape += c_shape
    out_specs += c_out
    outs = pl.pallas_call(
        functools.partial(_ffn_kernel, with_mix=mix is not None, with_proj=proj is not None, with_gate=n_gate > 0,
                          with_final=final_gain is not None, paired=tuple(paired), n_first=n_first,
                          n_cast=len(cast), proj_nt=proj_nt),
        out_shape=out_shape,
        grid=(m // tm,),
        in_specs=specs,
        out_specs=out_specs,
        scratch_shapes=[pltpu.VMEM((tm, FFN_HIDDEN), BF16)],
        compiler_params=_params("arbitrary" if cast else "parallel"),
        name="ffn",
    )(*args)
    return outs if len(outs) > 1 else outs[0]


def _dft_tables(t, tq):
    k = (np.arange(tq, dtype=np.int64)[:, None] * np.arange(t, dtype=np.int64)[None, :]) % t
    ang = 2.0 * np.pi * k.astype(np.float64) / t
    pos = np.concatenate([np.cos(ang), -np.sin(ang)], axis=1).astype(np.float32)
    kc = (np.arange(HEAD_DIM)[:, None] * np.arange(HEAD_DIM)[None, :]) % HEAD_DIM
    angc = 2.0 * np.pi * kc.astype(np.float64) / HEAD_DIM
    eye = np.eye(FOURIER_GROUPS)
    chan = np.concatenate([np.kron(eye, np.cos(angc)), np.kron(eye, np.sin(angc))], axis=1).astype(np.float32)
    return pos, chan


def _dft_kernel(p_ref, z_ref, c_ref, o_ref, p_scr, ab_scr, *, scale, quarter_turns):
    i, b = pl.program_id(0), pl.program_id(1)
    t = p_ref.shape[1] // 2

    @pl.when(b == 0)
    def _():
        col = lax.broadcasted_iota(jnp.int32, (1, t), 1)
        k = (i * quarter_turns * col) & 3
        ca = jnp.where(k == 0, 1.0, jnp.where(k == 2, -1.0, 0.0))
        sa = jnp.where(k == 1, 1.0, jnp.where(k == 3, -1.0, 0.0))
        c0, n0 = p_ref[:, 0:t], p_ref[:, t:]
        p_scr[:, 0:t] = (ca * c0 + sa * n0).astype(BF16)
        p_scr[:, t:] = (ca * n0 - sa * c0).astype(BF16)

    @pl.when(i == 0)
    def _():
        ab = _dot(z_ref[...], c_ref[...].astype(BF16))
        ab_scr[b, 0:t, :] = ab[:, :FOURIER_WIDTH].astype(BF16)
        ab_scr[b, t:, :] = ab[:, FOURIER_WIDTH:].astype(BF16)

    o_ref[...] = (_dot(p_scr[...], ab_scr[b]) * scale).astype(BF16)


def _fourier(z, t, bn, row0=0):
    tq = min(t, 512)
    assert (4 * tq) % t == 0
    pos, chan = _dft_tables(t, tq)
    off = row0 // t
    return pl.pallas_call(
        functools.partial(_dft_kernel, scale=1.0 / math.sqrt(t * HEAD_DIM), quarter_turns=4 * tq // t),
        out_shape=jax.ShapeDtypeStruct((bn * t, FOURIER_WIDTH), BF16),
        grid=(t // tq, bn),
        in_specs=[_resident((tq, 2 * t), lambda i, b: (0, 0)),
                  pl.BlockSpec((t, FOURIER_WIDTH), lambda i, b: (off + b, 0)),
                  pl.BlockSpec((FOURIER_WIDTH, 2 * FOURIER_WIDTH), lambda i, b: (0, 0))],
        out_specs=pl.BlockSpec((tq, FOURIER_WIDTH), lambda i, b: (b * (t // tq) + i, 0)),
        scratch_shapes=[pltpu.VMEM((tq, 2 * t), BF16), pltpu.VMEM((bn, 2 * t, FOURIER_WIDTH), BF16)],
        compiler_params=_params("arbitrary", "arbitrary"),
        name="dft",
    )(jnp.asarray(pos), z, jnp.asarray(chan))


def _gla_segment(q_ref, k_ref, v_ref, r_ref, g_ref, y_ref, gw_ref, gb_ref, gn_ref,
                 b_scr, o_scr, s_scr, n_rows):
    c_len = GLA_CHUNK
    n = n_rows // c_len
    pair_k = 2 * GLA_DK
    pair_v = 2 * GLA_DV

    for d in range(2):
        zg = g_ref[:, d * GLA_GATE_RANK:(d + 1) * GLA_GATE_RANK]
        b_scr[d, 0:n_rows, :] = _dot(zg.astype(BF16), gw_ref[d].astype(BF16)) + gb_ref[d]

    lane = lax.broadcasted_iota(jnp.int32, (c_len, pair_k), 1)
    col = lax.broadcasted_iota(jnp.int32, (c_len, pair_v), 1)
    ti = lax.broadcasted_iota(jnp.int32, (c_len, c_len), 0)
    tj = lax.broadcasted_iota(jnp.int32, (c_len, c_len), 1)
    blk = (lax.broadcasted_iota(jnp.int32, (pair_v, pair_k), 0) // GLA_DV
           == lax.broadcasted_iota(jnp.int32, (pair_v, pair_k), 1) // GLA_DK)

    group = math.gcd(GLA_GROUP, n)
    n_it = n // group
    keep2 = [jnp.concatenate([m, m], axis=0) for m in (tj <= ti, tj >= ti)]
    ti2 = lax.broadcasted_iota(jnp.int32, (c_len, 2 * c_len), 0)
    tj2 = lax.broadcasted_iota(jnp.int32, (c_len, 2 * c_len), 1) & (c_len - 1)
    tri2 = [jnp.where(m, 1.0, 0.0).astype(BF16) for m in (tj2 <= ti2, tj2 >= ti2)]
    gn = gn_ref[...]

    def finish(rows, o):
        for hh in range(2):
            sl = slice(hh * GLA_DV, (hh + 1) * GLA_DV)
            rr = r_ref[rows, sl].astype(F32)
            y_ref[rows, sl] = (_rms(o[:, sl]) * gn * (rr * _sigmoid(rr))).astype(y_ref.dtype)

    def chunk_rows(c):
        start = c * c_len
        return pl.ds(start if isinstance(start, int) else pl.multiple_of(start, c_len), c_len)

    def body(i, carry, phase):
        units = []
        for d in range(2):
            for g in range(group):
                c = i * group + g if d == 0 else n - 1 - (i * group + g)
                rows = chunk_rows(c)
                logit = b_scr[d, rows, :]
                lg = (jnp.minimum(logit, 0.0) - jnp.log(1.0 + jnp.exp(-jnp.abs(logit)))) * (1.0 / GLA_TAU)
                hi = lg.astype(BF16)
                rest = (lg - hi.astype(F32)).astype(BF16)
                units.append(dict(d=d, rows=rows, b=_dot(tri2[d], jnp.concatenate([hi, rest], axis=0))))
        for u in units:
            d, rows, b = u["d"], u["rows"], u["b"]
            bl = b[c_len - 1:c_len, :] if d == 0 else b[0:1, :]
            k = k_ref[rows, :].astype(F32)
            v16 = v_ref[rows, :].astype(BF16)
            qd = q_ref[rows, :].astype(F32) * (GLA_DK ** -0.5) * jnp.exp(b)
            kd = (k * jnp.exp(-b)).astype(BF16)
            kdec = (k * jnp.exp(bl - b)).astype(BF16)
            q2 = jnp.concatenate([jnp.where((lane // GLA_DK) == hh, qd, 0.0) for hh in range(2)], axis=0)
            att = _dot_nt(q2.astype(BF16), kd)
            upd = _dot_tn(v16, kdec)
            u.update(qd=qd.astype(BF16), v16=v16, att=att, upd=upd, dec=jnp.exp(bl))
        for d in range(2):
            s = s_scr[d]
            for u in units:
                if u["d"] == d:
                    u["o"] = _dot_nt(u["qd"], s.astype(BF16))
                    s = jnp.where(blk, s * u["dec"] + u["upd"], 0.0)
            s_scr[d] = s
        for u in units:
            att = jnp.where(keep2[u["d"]], u["att"], 0.0).astype(BF16)
            pv = _dot(att, u["v16"])
            u["o"] = u["o"] + jnp.where((col // GLA_DV) == 0, pv[0:c_len, :], pv[c_len:, :])
            if phase == "first":
                o_scr[u["d"], u["rows"], :] = u["o"]
            elif phase == "second":
                finish(u["rows"], u["o"] + o_scr[1 - u["d"], u["rows"], :])
        if phase == "only":
            for g in range(group):
                finish(units[g]["rows"], units[g]["o"] + units[group + n - 1 - g]["o"])
        return carry

    if n_it == 1:
        body(0, 0, "only")
    else:
        assert n_it % 2 == 0
        lax.fori_loop(0, n_it // 2, functools.partial(body, phase="first"), 0)
        lax.fori_loop(n_it // 2, n_it, functools.partial(body, phase="second"), 0)


def _gla_kernel(*refs, t_ctx, t_lat, n_cast):
    qc_ref, kc_ref, vc_ref, rc_ref, gc_ref, ql_ref, kl_ref, vl_ref, rl_ref, gl_ref, gw_ref, gb_ref, gn_ref = refs[:13]
    yc_ref, yl_ref = refs[13 + n_cast:15 + n_cast]
    b_scr, o_scr, s_scr = refs[-3:]

    @pl.when(pl.program_id(1) == 0)
    def _():
        _cast_blocks(refs[13:13 + n_cast], refs[15 + n_cast:15 + 2 * n_cast])

    s_scr[...] = jnp.zeros_like(s_scr)
    _gla_segment(qc_ref, kc_ref, vc_ref, rc_ref, gc_ref, yc_ref, gw_ref, gb_ref, gn_ref,
                 b_scr, o_scr, s_scr, t_ctx)
    _gla_segment(ql_ref, kl_ref, vl_ref, rl_ref, gl_ref, yl_ref, gw_ref, gb_ref, gn_ref,
                 b_scr, o_scr, s_scr, t_lat)


def _gla(z, zg, gate_w, gate_b, gla_g, bn, t_ctx, t_lat, ctx_row0, cast=()):
    pk, pv = 2 * GLA_DK, 2 * GLA_DV
    q0 = FOURIER_WIDTH // pk
    k0 = (FOURIER_WIDTH + GLA_HEADS * GLA_DK) // pk
    v0 = (FOURIER_WIDTH + 2 * GLA_HEADS * GLA_DK) // pv
    r0 = (FOURIER_WIDTH + 2 * GLA_HEADS * GLA_DK + GLA_HEADS * GLA_DV) // pv

    def seg_specs(t, row0):
        off = row0 // t
        return [pl.BlockSpec((t, pk), lambda b, p: (off + b, q0 + p)),
                pl.BlockSpec((t, pk), lambda b, p: (off + b, k0 + p)),
                pl.BlockSpec((t, pv), lambda b, p: (off + b, v0 + p)),
                pl.BlockSpec((t, pv), lambda b, p: (off + b, r0 + p)),
                pl.BlockSpec((t, 2 * GLA_GATE_RANK), lambda b, p: (off + b, 0))]

    wdt = GLA_HEADS * GLA_DV
    c_args, c_in, c_shape, c_out = _cast_riders(cast, lambda b, p: b, bn)
    return pl.pallas_call(
        functools.partial(_gla_kernel, t_ctx=t_ctx, t_lat=t_lat, n_cast=len(cast)),
        out_shape=[jax.ShapeDtypeStruct((bn * t_ctx, wdt), BF16), jax.ShapeDtypeStruct((bn * t_lat, wdt), BF16)] + c_shape,
        grid=(bn, GLA_HEADS // 2),
        in_specs=seg_specs(t_ctx, ctx_row0) + seg_specs(t_lat, 0) + [
            pl.BlockSpec((2, GLA_GATE_RANK, pk), lambda b, p: (0, 0, p)),
            pl.BlockSpec((2, 1, pk), lambda b, p: (0, 0, p)),
            pl.BlockSpec((1, GLA_DV), lambda b, p: (0, 0))] + c_in,
        out_specs=[pl.BlockSpec((t_ctx, pv), lambda b, p: (b, p)),
                   pl.BlockSpec((t_lat, pv), lambda b, p: (b, p))] + c_out,
        scratch_shapes=[pltpu.VMEM((2, t_lat, pk), F32), pltpu.VMEM((2, t_lat, pv), F32),
                        pltpu.VMEM((2, pv, pk), F32)],
        compiler_params=_params("arbitrary", "arbitrary"),
        name="gla",
    )(z, z, z, z, zg, z, z, z, z, zg, gate_w, gate_b.reshape(2, 1, GLA_HEADS * GLA_DK),
      gla_g.reshape(1, GLA_DV), *c_args)


def _short_conv_block(zb_ref, zc_ref, zx_ref, w_ref, b_ref, o_ref):
    u = zc_ref[...].astype(F32) * zx_ref[...].astype(F32)
    t = u.shape[0]
    row = lax.broadcasted_iota(jnp.int32, u.shape, 0)
    prev = jnp.where(row >= 1, pltpu.roll(u, 1, 0), 0.0)
    nxt = jnp.where(row < t - 1, pltpu.roll(u, t - 1, 0), 0.0)
    y = prev * w_ref[0:1, :] + u * w_ref[1:2, :] + nxt * w_ref[2:3, :] + b_ref[...]
    o_ref[...] = (zb_ref[...].astype(F32) * y).astype(o_ref.dtype)


def _rope_tables(t):
    rows = t // GRID_W
    row = jnp.repeat(jnp.arange(rows), GRID_W).astype(F32)
    col = jnp.tile(jnp.arange(GRID_W), rows).astype(F32)
    n = ROPE_AXIS_DIM // 2
    inv = ROPE_THETA ** (-jnp.arange(n, dtype=F32) / n)
    ar, ac = row[:, None] * inv, col[:, None] * inv
    ang = jnp.concatenate([ar, ar, ac, ac], axis=-1)
    sign = jnp.tile(jnp.concatenate([-jnp.ones((n,), F32), jnp.ones((n,), F32)]), 2)
    cos = jnp.cos(ang)
    sin = jnp.sin(ang) * sign
    return jnp.tile(cos, (1, 2)), jnp.tile(sin, (1, 2))


def _rope(x, cos, sin):
    lane = lax.broadcasted_iota(jnp.int32, x.shape, 1)
    n = ROPE_AXIS_DIM // 2
    w = x.shape[1]
    partner = jnp.where((lane & (2 * n - 1)) < n, pltpu.roll(x, w - n, 1), pltpu.roll(x, n, 1))
    return x * cos + partner * sin


ATTN_ONES_ROWS = 16
ATTN_HEADS_PER_STEP = 2


def _attn_kernel(*refs, t_lat, lam_init, tq, n_cast):
    lam_ref, q_ref, kl_ref, vl_ref, kc_ref, vc_ref, cos_ref, sin_ref, dn_ref = refs[:9]
    conv_in = refs[9:14]
    n_in = 14 + n_cast
    o_ref, conv_ref = refs[n_in:n_in + 2]
    k_scr, vt_scr, s0_scr, m0_scr, s1_scr, m1_scr = refs[-6:]

    @pl.when(pl.program_id(1) == 0)
    def _():
        _short_conv_block(*conv_in, conv_ref)
        _cast_blocks(refs[14:n_in], refs[n_in + 2:n_in + 2 + n_cast])

    hw = 2 * HEAD_DIM
    t_all = k_scr.shape[1]

    for hd in range(ATTN_HEADS_PER_STEP):
        cols = slice(hd * hw, (hd + 1) * hw)
        k_scr[hd, 0:t_lat, :] = _rope(kl_ref[:, cols].astype(F32), cos_ref[...], sin_ref[...]).astype(BF16)
        k_scr[hd, t_lat:, :] = kc_ref[:, cols].astype(BF16)
        vt_scr[hd, 0:hw, 0:t_lat] = vl_ref[:, cols].astype(F32).T.astype(BF16)
        vt_scr[hd, 0:hw, t_lat:] = vc_ref[:, cols].astype(F32).T.astype(BF16)
        vt_scr[hd, hw:, :] = jnp.ones((ATTN_ONES_ROWS, t_all), BF16)

    lv = lam_ref[...]
    lam = (jnp.exp(jnp.sum(lv[0:1] * lv[1:2], axis=-1, keepdims=True))
           - jnp.exp(jnp.sum(lv[2:3] * lv[3:4], axis=-1, keepdims=True)) + lam_init)
    lane = lax.broadcasted_iota(jnp.int32, (tq, hw), 1)

    n_blocks = t_lat // tq
    slots = ((s0_scr, m0_scr), (s1_scr, m1_scr))

    def scores(hd, i, slot):
        rows = pl.ds(pl.multiple_of(i * tq, tq), tq)
        q = (_rope(q_ref[rows, hd * hw:(hd + 1) * hw].astype(F32), cos_ref[rows, :], sin_ref[rows, :])
             * (HEAD_DIM ** -0.5 * math.log2(math.e)))
        s_scr, m_scr = slots[slot]
        for half in range(2):
            qb = jnp.where((lane // HEAD_DIM) == half, q, 0.0).astype(BF16)
            s = _dot_nt(k_scr[hd], qb)
            s_scr[half] = s
            m_scr[half] = jnp.max(s, axis=0, keepdims=True)

    def outputs(hd, i, slot):
        rows = pl.ds(pl.multiple_of(i * tq, tq), tq)
        s_scr, m_scr = slots[slot]
        outs = []
        for half in range(2):
            p = jnp.exp2(s_scr[half] - m_scr[half]).astype(BF16)
            acc = _dot(vt_scr[hd], p)
            outs.append(acc[0:hw, :] / acc[hw:hw + 1, :])
        o = (outs[0] - lam * outs[1]).T
        o_ref[rows, hd * hw:(hd + 1) * hw] = (_rms(o) * dn_ref[...] * (1.0 - lam_init)).astype(o_ref.dtype)

    assert n_blocks % 2 == 0
    scores(0, 0, 0)
    for hd in range(ATTN_HEADS_PER_STEP):
        def body(j, carry, hd=hd):
            scores(hd, 2 * j + 1, 1)
            outputs(hd, 2 * j, 0)
            scores(hd, 2 * j + 2, 0)
            outputs(hd, 2 * j + 1, 1)
            return carry

        lax.fori_loop(0, n_blocks // 2 - 1, body, 0)
        scores(hd, n_blocks - 1, 1)
        outputs(hd, n_blocks - 2, 0)
        if hd + 1 < ATTN_HEADS_PER_STEP:
            scores(hd + 1, 0, 0)
        outputs(hd, n_blocks - 1, 1)


def _odd_mixers(z, lam_vecs, dnorm, lam_init, conv_w, conv_b, bn, t_lat, t_ctx, ctx_row0, cast=()):
    hw = 2 * HEAD_DIM
    wb = ATTN_HEADS_PER_STEP * hw
    q0 = 3 * CONV_WIDTH // wb
    k0 = q0 + DIFF_HEADS // ATTN_HEADS_PER_STEP
    v0 = k0 + DIFF_HEADS // ATTN_HEADS_PER_STEP
    tq = 256
    t_all = t_lat + t_ctx
    coff = ctx_row0 // t_ctx
    cos, sin = _rope_tables(t_lat)
    c_args, c_in, c_shape, c_out = _cast_riders(cast, lambda b, p: b, bn)
    outs = pl.pallas_call(
        functools.partial(_attn_kernel, t_lat=t_lat, lam_init=lam_init, tq=tq, n_cast=len(cast)),
        out_shape=[jax.ShapeDtypeStruct((bn * t_lat, DIFF_HEADS * DIFF_DV), BF16),
                   jax.ShapeDtypeStruct((bn * t_lat, CONV_WIDTH), BF16)] + c_shape,
        grid=(bn, DIFF_HEADS // ATTN_HEADS_PER_STEP),
        in_specs=[
            pl.BlockSpec((4, HEAD_DIM), lambda b, p: (0, 0)),
            pl.BlockSpec((t_lat, wb), lambda b, p: (b, q0 + p)),
            pl.BlockSpec((t_lat, wb), lambda b, p: (b, k0 + p)),
            pl.BlockSpec((t_lat, wb), lambda b, p: (b, v0 + p)),
            pl.BlockSpec((t_ctx, wb), lambda b, p: (coff + b, k0 + p)),
            pl.BlockSpec((t_ctx, wb), lambda b, p: (coff + b, v0 + p)),
            pl.BlockSpec((t_lat, hw), lambda b, p: (0, 0)),
            pl.BlockSpec((t_lat, hw), lambda b, p: (0, 0)),
            pl.BlockSpec((1, DIFF_DV), lambda b, p: (0, 0)),
            pl.BlockSpec((t_lat, CONV_WIDTH), lambda b, p: (b, 0)),
            pl.BlockSpec((t_lat, CONV_WIDTH), lambda b, p: (b, 1)),
            pl.BlockSpec((t_lat, CONV_WIDTH), lambda b, p: (b, 2)),
            pl.BlockSpec((3, CONV_WIDTH), lambda b, p: (0, 0)),
            pl.BlockSpec((1, CONV_WIDTH), lambda b, p: (0, 0)),
        ] + c_in,
        out_specs=[pl.BlockSpec((t_lat, wb), lambda b, p: (b, p)),
                   pl.BlockSpec((t_lat, CONV_WIDTH), lambda b, p: (b, 0))] + c_out,
        scratch_shapes=[pltpu.VMEM((ATTN_HEADS_PER_STEP, t_all, hw), BF16),
                        pltpu.VMEM((ATTN_HEADS_PER_STEP, hw + ATTN_ONES_ROWS, t_all), BF16),
                        pltpu.VMEM((2, t_all, tq), F32), pltpu.VMEM((2, 1, tq), F32),
                        pltpu.VMEM((2, t_all, tq), F32), pltpu.VMEM((2, 1, tq), F32)],
        compiler_params=_params("arbitrary", "arbitrary"),
        name="diff_attn",
    )(lam_vecs, z, z, z, z, z, cos, sin, dnorm.reshape(1, DIFF_DV), z, z, z, conv_w,
      conv_b.reshape(1, CONV_WIDTH), *c_args)
    return outs


def kernel(x, c, ctx, c_ctx, ada_w, ada_b, norm_ffn1, norm_mix, norm_ffn2, ffn1_w_in, ffn1_w_out, ffn2_w_in,
           ffn2_w_out, mix_w_out, even_w_in, gla_gate_w, gla_gate_b, gla_norm, odd_w_in, conv_w, conv_b,
           lambda_q1, lambda_k1, lambda_q2, lambda_k2, diff_norm, final_norm):
    assert DEPTH == 2
    bn, t_lat, d = x.shape
    t_ctx = ctx.shape[1]
    assert bn < MOD_ROWS
    ctx_row = bn

    cond = jnp.concatenate([c, c_ctx[None, :], jnp.zeros((MOD_ROWS - bn - 1, d), F32)], axis=0)
    n_mod_steps = DEPTH * MOD_TILES
    mods, w1i0, w1o0 = _modulation(cond, ada_w, ada_b,
                                   cast=((ffn1_w_in, 0, n_mod_steps), (ffn1_w_out, 0, n_mod_steps)))
    w_even_t = jnp.swapaxes(even_w_in[0], 0, 1).astype(BF16)

    tm = 512
    n_lat = bn * t_lat
    lat_blocks = n_lat // tm
    lat_row = lambda i: i // (t_lat // tm)
    all_row = lambda i: jnp.where(i < lat_blocks, i // (t_lat // tm), ctx_row)
    g1 = norm_ffn1.reshape(DEPTH, 1, d)
    gm = norm_mix.reshape(DEPTH, 1, d)
    g2 = norm_ffn2.reshape(DEPTH, 1, d)
    cast_gla = ((ffn2_w_in, 0, bn), (ffn2_w_out, 0, bn), (mix_w_out, 0, bn))
    cast_ffn = ((ffn1_w_in, 1, FFN_CAST_STEPS), (ffn1_w_out, 1, FFN_CAST_STEPS // 2), (odd_w_in, 0, FFN_CAST_STEPS))
    cast_attn = ((ffn2_w_in, 1, bn), (ffn2_w_out, 1, bn), (mix_w_out, 1, bn))

    proj0 = (gm, w_even_t, 2 * GLA_GATE_RANK, True)
    h, z, zg = _ffn((x.reshape(n_lat, d), ctx.reshape(bn * t_ctx, d)), mods, 0, 0, all_row, g1, w1i0, w1o0,
                    tm=tm, n_first=lat_blocks, proj=proj0)
    yf_l = _fourier(z, t_lat, bn)
    yf_c = _fourier(z, t_ctx, bn, row0=n_lat)
    yg_c, yg_l, w2i0, w2o0, wmix0 = _gla(z, zg, gla_gate_w[0], gla_gate_b[0], gla_norm[0], bn, t_ctx, t_lat, n_lat,
                                         cast=cast_gla)
    h, w1i1, w1o1, w_odd = _ffn(h, mods, 0, 6, all_row, g2, w2i0, w2o0, tm=tm, n_first=lat_blocks,
                                mix=((yf_l, yf_c), (yg_l, yg_c), wmix0), cast=cast_ffn)

    h, z = _ffn(h, mods, 1, 0, all_row, g1, w1i1, w1o1, tm=tm, proj=(gm, w_odd, 0, False))
    lam_init = 0.8 - 0.6 * math.exp(-0.3 * 1)
    lam_vecs = jnp.stack([lambda_q1[0], lambda_k1[0], lambda_q2[0], lambda_k2[0]]).astype(F32)
    y_att, y_conv, w2i1, w2o1, wmix1 = _odd_mixers(z, lam_vecs, diff_norm[0], lam_init, conv_w[0], conv_b[0], bn,
                                                   t_lat, t_ctx, n_lat, cast=cast_attn)
    h = _ffn(h, mods, 1, 6, lat_row, g2, w2i1, w2o1, tm=tm, mix=(y_conv, y_att, wmix1),
             final_gain=final_norm.reshape(1, d), n_rows=n_lat)
    return h.reshape(bn, t_lat, d)
```

```python
import functools
import math

import numpy as np
import jax
import jax.numpy as jnp
from jax import lax
from jax.experimental import pallas as pl
from jax.experimental.pallas import tpu as pltpu

D_MODEL = 1024
DEPTH = 2
GRID_W = 64
HEAD_DIM = 64
N_MOD = 9
FFN_HIDDEN = 2816
NORM_EPS = 1e-6
FOURIER_GROUPS = 4
FOURIER_WIDTH = FOURIER_GROUPS * HEAD_DIM
GLA_HEADS = 6
GLA_DK = 64
GLA_DV = 128
GLA_GATE_RANK = 16
GLA_TAU = 16.0
GLA_CHUNK = 64
GLA_GROUP = 8
FFN_ROWS = 512
FFN_TILE = 256
FFN_CAST_STEPS = 32
PROJ_TILE = 512
DFT_ROWS = 512
ATTN_Q_BLOCK = 256
CONV_WIDTH = 4 * HEAD_DIM
DIFF_HEADS = 6
DIFF_DV = 2 * HEAD_DIM
ROPE_THETA = 10000.0
ROPE_AXIS_DIM = HEAD_DIM // 2

MOD_ROWS = 16
MOD_TILES = 4
VMEM_LIMIT = 48 * 1024 * 1024

F32 = jnp.float32
BF16 = jnp.bfloat16


def _params(*sem):
    return pltpu.CompilerParams(dimension_semantics=sem, vmem_limit_bytes=VMEM_LIMIT)


def _sigmoid(x):
    return 1.0 / (1.0 + jnp.exp(-x))


def _dot(a, b):
    return jnp.dot(a, b, preferred_element_type=F32)


def _dot_nt(a, b):
    return lax.dot_general(a, b, (((1,), (1,)), ((), ())), preferred_element_type=F32)


def _dot_tn(a, b):
    return lax.dot_general(a, b, (((0,), (0,)), ((), ())), preferred_element_type=F32)


def _rms(x):
    return x * lax.rsqrt(jnp.mean(x * x, axis=-1, keepdims=True) + NORM_EPS)


def _mod_kernel(*refs, n_cast):
    cond_ref, w_ref, b_ref = refs[:3]
    o_ref = refs[3 + n_cast]
    _cast_blocks(refs[3:3 + n_cast], refs[4 + n_cast:])
    c = cond_ref[...]
    s = c * _sigmoid(c)
    w = w_ref[...]
    w_hi = w.astype(BF16)
    w_lo = (w - w_hi.astype(F32)).astype(BF16)
    s_hi = s.astype(BF16)
    s_lo = (s - s_hi.astype(F32)).astype(BF16)
    o_ref[...] = _dot(s_hi, w_hi) + _dot(s_lo, w_hi) + _dot(s_hi, w_lo) + b_ref[...]


def _modulation(cond, ada_w, ada_b, cast=()):
    n = N_MOD * D_MODEL
    tn = n // MOD_TILES
    c_args, c_in, c_shape, c_out = _cast_riders(cast, lambda l, j: l * MOD_TILES + j, DEPTH * MOD_TILES)
    outs = pl.pallas_call(
        functools.partial(_mod_kernel, n_cast=len(cast)),
        out_shape=[jax.ShapeDtypeStruct((DEPTH, MOD_ROWS, n), F32)] + c_shape,
        grid=(DEPTH, MOD_TILES),
        in_specs=[
            pl.BlockSpec((MOD_ROWS, D_MODEL), lambda l, j: (0, 0)),
            pl.BlockSpec((None, D_MODEL, tn), lambda l, j: (l, 0, j)),
            pl.BlockSpec((None, 1, tn), lambda l, j: (l, 0, j)),
        ] + c_in,
        out_specs=[pl.BlockSpec((None, MOD_ROWS, tn), lambda l, j: (l, 0, j))] + c_out,
        compiler_params=_params("arbitrary", "arbitrary"),
        name="modulation",
    )(cond, ada_w, ada_b.reshape(DEPTH, 1, n), *c_args)
    return (outs[0].reshape(DEPTH, MOD_ROWS, N_MOD, 1, D_MODEL), *outs[1:])


def _mod_spec(layer, k, row_of_block):
    return pl.BlockSpec((None, None, None, 1, D_MODEL),
                        lambda i, *_: (layer, row_of_block(i), k, 0, 0))


def _gain_spec(layer):
    return pl.BlockSpec((None, 1, D_MODEL), lambda i, *_: (layer, 0, 0))


def _resident(block_shape, index_map):
    return pl.BlockSpec(block_shape, index_map, pipeline_mode=pl.Buffered(1))


def _cast_riders(sources, step_of, n_steps):
    args, in_specs, out_shape, out_specs = [], [], [], []
    for arr, lead, nb in sources:
        _, r, c = arr.shape
        rows, stride = r // nb, n_steps // nb
        assert rows * nb == r and stride * nb == n_steps and rows % 16 == 0
        idx = lambda *g, nb=nb, stride=stride: jnp.minimum(step_of(*g) // stride, nb - 1)
        args.append(arr)
        in_specs.append(pl.BlockSpec((None, rows, c), lambda *g, idx=idx, lead=lead: (lead, idx(*g), 0)))
        out_shape.append(jax.ShapeDtypeStruct((r, c), BF16))
        out_specs.append(pl.BlockSpec((rows, c), lambda *g, idx=idx: (idx(*g), 0)))
    return args, in_specs, out_shape, out_specs


def _cast_blocks(src_refs, dst_refs):
    for src, dst in zip(src_refs, dst_refs):
        dst[...] = src[...].astype(BF16)


def _ffn_kernel(*refs, with_mix, with_proj, with_gate, with_final, paired, n_first, n_cast, proj_nt):
    it = iter(refs)
    pairs = iter(paired)

    def rows_in():
        a = next(it)
        if not next(pairs):
            return a[...]
        b = next(it)
        return jnp.where(pl.program_id(0) < n_first, a[...], b[...])

    x = rows_in()
    if with_mix:
        ma, mb = rows_in(), rows_in()
        wm_ref, gm_ref = next(it), next(it)
    sh_ref, sc_ref, g_ref, gain_ref, wi_ref, wo_ref = (next(it) for _ in range(6))
    if with_proj:
        psh_ref, psc_ref, pgain_ref, wp_ref = (next(it) for _ in range(4))
    if with_final:
        fn_ref = next(it)
    cast_src = [next(it) for _ in range(n_cast)]
    o_ref = next(it)
    if with_proj:
        z_ref = next(it)
    if with_gate:
        zg_ref = next(it)
    cast_dst = [next(it) for _ in range(n_cast)]
    a_scr = next(it)
    _cast_blocks(cast_src, cast_dst)

    if with_mix:
        ka = ma.shape[1]
        x = x + gm_ref[...] * (_dot(ma, wm_ref[0:ka, :]) + _dot(mb, wm_ref[ka:, :]))
    xn = (_rms(x) * gain_ref[...] * (1.0 + sc_ref[...]) + sh_ref[...]).astype(BF16)
    for j in range(FFN_HIDDEN // FFN_TILE):
        lo = j * FFN_TILE
        g = _dot(xn, wi_ref[:, lo:lo + FFN_TILE])
        u = _dot(xn, wi_ref[:, FFN_HIDDEN + lo:FFN_HIDDEN + lo + FFN_TILE])
        a_scr[:, lo:lo + FFN_TILE] = (g * _sigmoid(g) * u).astype(BF16)
    out = x + (0.5 * g_ref[...]) * _dot(a_scr[...], wo_ref[...])
    if with_final:
        out = _rms(out) * fn_ref[...]
    o_ref[...] = out
    if with_proj:
        xm = (_rms(out) * pgain_ref[...] * (1.0 + psc_ref[...]) + psh_ref[...]).astype(BF16)
        n_out = z_ref.shape[1]
        project = (lambda lo, hi: _dot_nt(xm, wp_ref[lo:hi, :])) if proj_nt else (lambda lo, hi: _dot(xm, wp_ref[:, lo:hi]))
        for lo in range(0, n_out, PROJ_TILE):
            z_ref[:, lo:lo + PROJ_TILE] = project(lo, lo + PROJ_TILE).astype(z_ref.dtype)
        if with_gate:
            zg_ref[...] = project(n_out, n_out + zg_ref.shape[1])


def _ffn(h, mods, layer, mod_base, row_of_block, gain, w_in, w_out, *, tm, n_first=None, n_rows=None, mix=None,
         proj=None, final_gain=None, cast=()):
    m = n_rows or (sum(a.shape[0] for a in h) if isinstance(h, tuple) else h.shape[0])
    row = lambda i: (i, 0)
    args, specs, paired = [], [], []

    def add_rows(a):
        paired.append(isinstance(a, tuple))
        if paired[-1]:
            first, second = a
            assert first.shape[0] == n_first * tm
            args.extend([first, second])
            specs.extend([pl.BlockSpec((tm, first.shape[1]), lambda i: (jnp.minimum(i, n_first - 1), 0)),
                          pl.BlockSpec((tm, second.shape[1]), lambda i: (jnp.maximum(i - n_first, 0), 0))])
        else:
            args.append(a)
            specs.append(pl.BlockSpec((tm, a.shape[1]), row))

    add_rows(h)
    if mix is not None:
        ma, mb, w_mix = mix
        add_rows(ma)
        add_rows(mb)
        args += [w_mix, mods]
        specs += [_resident(w_mix.shape, lambda i: (0, 0)), _mod_spec(layer, 5, row_of_block)]
    args += [mods, mods, mods, gain, w_in, w_out]
    specs += [
        _mod_spec(layer, mod_base, row_of_block),
        _mod_spec(layer, mod_base + 1, row_of_block),
        _mod_spec(layer, mod_base + 2, row_of_block),
        _gain_spec(layer),
        _resident((D_MODEL, 2 * FFN_HIDDEN), lambda i: (0, 0)),
        _resident((FFN_HIDDEN, D_MODEL), lambda i: (0, 0)),
    ]
    out_shape = [jax.ShapeDtypeStruct((m, D_MODEL), F32)]
    out_specs = [pl.BlockSpec((tm, D_MODEL), row)]
    n_gate, proj_nt = 0, False
    if proj is not None:
        pgain, wp, n_gate, proj_nt = proj
        n_out = wp.shape[0 if proj_nt else 1] - n_gate
        assert n_out % PROJ_TILE == 0
        args += [mods, mods, pgain, wp]
        specs += [_mod_spec(layer, 3, row_of_block), _mod_spec(layer, 4, row_of_block), _gain_spec(layer),
                  _resident(wp.shape, lambda i: (0, 0))]
        out_shape.append(jax.ShapeDtypeStruct((m, n_out), BF16))
        out_specs.append(pl.BlockSpec((tm, n_out), row))
        if n_gate:
            out_shape.append(jax.ShapeDtypeStruct((m, n_gate), F32))
            out_specs.append(pl.BlockSpec((tm, n_gate), row))
    if final_gain is not None:
        args.append(final_gain)
        specs.append(pl.BlockSpec((1, D_MODEL), lambda i: (0, 0)))
    c_args, c_in, c_shape, c_out = _cast_riders(cast, lambda i: i, FFN_CAST_STEPS)
    assert not cast or m // tm >= FFN_CAST_STEPS
    args += c_args
    specs += c_in
    out_shape += c_shape
    out_specs += c_out
    outs = pl.pallas_call(
        functools.partial(_ffn_kernel, with_mix=mix is not None, with_proj=proj is not None, with_gate=n_gate > 0,
                          with_final=final_gain is not None, paired=tuple(paired), n_first=n_first,
                          n_cast=len(cast), proj_nt=proj_nt),
        out_shape=out_shape,
        grid=(m // tm,),
        in_specs=specs,
        out_specs=out_specs,
        scratch_shapes=[pltpu.VMEM((tm, FFN_HIDDEN), BF16)],
        compiler_params=_params("arbitrary" if cast else "parallel"),
        name="ffn",
    )(*args)
    return outs if len(outs) > 1 else outs[0]


def _dft_tables(t, tq):
    k = (np.arange(tq, dtype=np.int64)[:, None] * np.arange(t, dtype=np.int64)[None, :]) % t
    ang = 2.0 * np.pi * k.astype(np.float64) / t
    pos = np.concatenate([np.cos(ang), -np.sin(ang)], axis=1).astype(np.float32)
    kc = (np.arange(HEAD_DIM)[:, None] * np.arange(HEAD_DIM)[None, :]) % HEAD_DIM
    angc = 2.0 * np.pi * kc.astype(np.float64) / HEAD_DIM
    eye = np.eye(FOURIER_GROUPS)
    chan = np.concatenate([np.kron(eye, np.cos(angc)), np.kron(eye, np.sin(angc))], axis=1).astype(np.float32)
    return pos, chan


def _dft_kernel(p_ref, z_ref, c_ref, o_ref, p_scr, ab_scr, *, scale, quarter_turns):
    i, b = pl.program_id(0), pl.program_id(1)
    t = p_ref.shape[1] // 2

    @pl.when(b == 0)
    def _():
        col = lax.broadcasted_iota(jnp.int32, (1, t), 1)
        k = (i * quarter_turns * col) & 3
        ca = jnp.where(k == 0, 1.0, jnp.where(k == 2, -1.0, 0.0))
        sa = jnp.where(k == 1, 1.0, jnp.where(k == 3, -1.0, 0.0))
        c0, n0 = p_ref[:, 0:t], p_ref[:, t:]
        p_scr[:, 0:t] = (ca * c0 + sa * n0).astype(BF16)
        p_scr[:, t:] = (ca * n0 - sa * c0).astype(BF16)

    @pl.when(i == 0)
    def _():
        ab = _dot(z_ref[...], c_ref[...].astype(BF16))
        ab_scr[b, 0:t, :] = ab[:, :FOURIER_WIDTH].astype(BF16)
        ab_scr[b, t:, :] = ab[:, FOURIER_WIDTH:].astype(BF16)

    o_ref[...] = (_dot(p_scr[...], ab_scr[b]) * scale).astype(BF16)


def _fourier(z, t, bn, row0=0):
    tq = min(t, DFT_ROWS)
    assert (4 * tq) % t == 0
    pos, chan = _dft_tables(t, tq)
    off = row0 // t
    return pl.pallas_call(
        functools.partial(_dft_kernel, scale=1.0 / math.sqrt(t * HEAD_DIM), quarter_turns=4 * tq // t),
        out_shape=jax.ShapeDtypeStruct((bn * t, FOURIER_WIDTH), BF16),
        grid=(t // tq, bn),
        in_specs=[_resident((tq, 2 * t), lambda i, b: (0, 0)),
                  pl.BlockSpec((t, FOURIER_WIDTH), lambda i, b: (off + b, 0)),
                  pl.BlockSpec((FOURIER_WIDTH, 2 * FOURIER_WIDTH), lambda i, b: (0, 0))],
        out_specs=pl.BlockSpec((tq, FOURIER_WIDTH), lambda i, b: (b * (t // tq) + i, 0)),
        scratch_shapes=[pltpu.VMEM((tq, 2 * t), BF16), pltpu.VMEM((bn, 2 * t, FOURIER_WIDTH), BF16)],
        compiler_params=_params("arbitrary", "arbitrary"),
        name="dft",
    )(jnp.asarray(pos), z, jnp.asarray(chan))


def _gla_segment(q_ref, k_ref, v_ref, r_ref, g_ref, y_ref, gw_ref, gb_ref, gn_ref,
                 b_scr, o_scr, s_scr, n_rows):
    c_len = GLA_CHUNK
    n = n_rows // c_len
    pair_k = 2 * GLA_DK
    pair_v = 2 * GLA_DV

    for d in range(2):
        zg = g_ref[:, d * GLA_GATE_RANK:(d + 1) * GLA_GATE_RANK]
        b_scr[d, 0:n_rows, :] = _dot(zg.astype(BF16), gw_ref[d].astype(BF16)) + gb_ref[d]

    lane = lax.broadcasted_iota(jnp.int32, (c_len, pair_k), 1)
    col = lax.broadcasted_iota(jnp.int32, (c_len, pair_v), 1)
    ti = lax.broadcasted_iota(jnp.int32, (c_len, c_len), 0)
    tj = lax.broadcasted_iota(jnp.int32, (c_len, c_len), 1)

    group = math.gcd(GLA_GROUP, n)
    n_it = n // group
    keep2 = [jnp.concatenate([m, m], axis=0) for m in (tj <= ti, tj >= ti)]
    ti2 = lax.broadcasted_iota(jnp.int32, (c_len, 2 * c_len), 0)
    tj2 = lax.broadcasted_iota(jnp.int32, (c_len, 2 * c_len), 1) & (c_len - 1)
    tri2 = [jnp.where(m, 1.0, 0.0).astype(BF16) for m in (tj2 <= ti2, tj2 >= ti2)]
    gn = gn_ref[...]

    def finish(rows, o):
        for hh in range(2):
            sl = slice(hh * GLA_DV, (hh + 1) * GLA_DV)
            rr = r_ref[rows, sl].astype(F32)
            y_ref[rows, sl] = (_rms(o[:, sl]) * gn * (rr * _sigmoid(rr))).astype(y_ref.dtype)

    def chunk_rows(c):
        start = c * c_len
        return pl.ds(start if isinstance(start, int) else pl.multiple_of(start, c_len), c_len)

    def body(i, carry, phase):
        units = []
        for d in range(2):
            for g in range(group):
                c = i * group + g if d == 0 else n - 1 - (i * group + g)
                rows = chunk_rows(c)
                logit = b_scr[d, rows, :]
                lg = (jnp.minimum(logit, 0.0) - jnp.log(1.0 + jnp.exp(-jnp.abs(logit)))) * (1.0 / GLA_TAU)
                hi = lg.astype(BF16)
                rest = (lg - hi.astype(F32)).astype(BF16)
                units.append(dict(d=d, rows=rows, b=_dot(tri2[d], jnp.concatenate([hi, rest], axis=0))))
        for u in units:
            d, rows, b = u["d"], u["rows"], u["b"]
            bl = b[c_len - 1:c_len, :] if d == 0 else b[0:1, :]
            k = k_ref[rows, :].astype(F32)
            v16 = v_ref[rows, :].astype(BF16)
            qd = q_ref[rows, :].astype(F32) * (GLA_DK ** -0.5) * jnp.exp(b)
            kd = (k * jnp.exp(-b)).astype(BF16)
            kdec = (k * jnp.exp(bl - b)).astype(BF16)
            q2 = jnp.concatenate([jnp.where((lane // GLA_DK) == hh, qd, 0.0) for hh in range(2)],
                                 axis=0).astype(BF16)
            att = _dot_nt(q2, kd)
            upd = _dot_tn(v16, kdec)
            u.update(q2=q2, v16=v16, att=att, upd=upd, dec=jnp.exp(bl))
        for d in range(2):
            s = s_scr[d]
            for u in units:
                if u["d"] == d:
                    u["o"] = _dot_nt(u["q2"], s.astype(BF16))
                    s = s * u["dec"] + u["upd"]
            s_scr[d] = s
        for u in units:
            att = jnp.where(keep2[u["d"]], u["att"], 0.0).astype(BF16)
            pv = _dot(att, u["v16"])
            both = u["o"] + pv
            u["o"] = jnp.where((col // GLA_DV) == 0, both[0:c_len, :], both[c_len:, :])
            if phase == "first":
                o_scr[u["d"], u["rows"], :] = u["o"]
            elif phase == "second":
                finish(u["rows"], u["o"] + o_scr[1 - u["d"], u["rows"], :])
        if phase == "only":
            for g in range(group):
                finish(units[g]["rows"], units[g]["o"] + units[group + n - 1 - g]["o"])
        return carry

    if n_it == 1:
        body(0, 0, "only")
    else:
        assert n_it % 2 == 0
        lax.fori_loop(0, n_it // 2, functools.partial(body, phase="first"), 0)
        lax.fori_loop(n_it // 2, n_it, functools.partial(body, phase="second"), 0)


def _gla_kernel(*refs, t_ctx, t_lat, n_cast):
    qc_ref, kc_ref, vc_ref, rc_ref, gc_ref, ql_ref, kl_ref, vl_ref, rl_ref, gl_ref, gw_ref, gb_ref, gn_ref = refs[:13]
    yc_ref, yl_ref = refs[13 + n_cast:15 + n_cast]
    b_scr, o_scr, s_scr = refs[-3:]

    @pl.when(pl.program_id(1) == 0)
    def _():
        _cast_blocks(refs[13:13 + n_cast], refs[15 + n_cast:15 + 2 * n_cast])

    s_scr[...] = jnp.zeros_like(s_scr)
    _gla_segment(qc_ref, kc_ref, vc_ref, rc_ref, gc_ref, yc_ref, gw_ref, gb_ref, gn_ref,
                 b_scr, o_scr, s_scr, t_ctx)
    _gla_segment(ql_ref, kl_ref, vl_ref, rl_ref, gl_ref, yl_ref, gw_ref, gb_ref, gn_ref,
                 b_scr, o_scr, s_scr, t_lat)


def _gla(z, zg, gate_w, gate_b, gla_g, bn, t_ctx, t_lat, ctx_row0, cast=()):
    pk, pv = 2 * GLA_DK, 2 * GLA_DV
    q0 = FOURIER_WIDTH // pk
    k0 = (FOURIER_WIDTH + GLA_HEADS * GLA_DK) // pk
    v0 = (FOURIER_WIDTH + 2 * GLA_HEADS * GLA_DK) // pv
    r0 = (FOURIER_WIDTH + 2 * GLA_HEADS * GLA_DK + GLA_HEADS * GLA_DV) // pv

    def seg_specs(t, row0):
        off = row0 // t
        return [pl.BlockSpec((t, pk), lambda b, p: (off + b, q0 + p)),
                pl.BlockSpec((t, pk), lambda b, p: (off + b, k0 + p)),
                pl.BlockSpec((t, pv), lambda b, p: (off + b, v0 + p)),
                pl.BlockSpec((t, pv), lambda b, p: (off + b, r0 + p)),
                pl.BlockSpec((t, 2 * GLA_GATE_RANK), lambda b, p: (off + b, 0))]

    wdt = GLA_HEADS * GLA_DV
    c_args, c_in, c_shape, c_out = _cast_riders(cast, lambda b, p: b, bn)
    return pl.pallas_call(
        functools.partial(_gla_kernel, t_ctx=t_ctx, t_lat=t_lat, n_cast=len(cast)),
        out_shape=[jax.ShapeDtypeStruct((bn * t_ctx, wdt), BF16), jax.ShapeDtypeStruct((bn * t_lat, wdt), BF16)] + c_shape,
        grid=(bn, GLA_HEADS // 2),
        in_specs=seg_specs(t_ctx, ctx_row0) + seg_specs(t_lat, 0) + [
            pl.BlockSpec((2, GLA_GATE_RANK, pk), lambda b, p: (0, 0, p)),
            pl.BlockSpec((2, 1, pk), lambda b, p: (0, 0, p)),
            pl.BlockSpec((1, GLA_DV), lambda b, p: (0, 0))] + c_in,
        out_specs=[pl.BlockSpec((t_ctx, pv), lambda b, p: (b, p)),
                   pl.BlockSpec((t_lat, pv), lambda b, p: (b, p))] + c_out,
        scratch_shapes=[pltpu.VMEM((2, t_lat, pk), F32), pltpu.VMEM((2, t_lat, pv), F32),
                        pltpu.VMEM((2, pv, pk), F32)],
        compiler_params=_params("arbitrary", "arbitrary"),
        name="gla",
    )(z, z, z, z, zg, z, z, z, z, zg, gate_w, gate_b.reshape(2, 1, GLA_HEADS * GLA_DK),
      gla_g.reshape(1, GLA_DV), *c_args)


def _short_conv_block(zb_ref, zc_ref, zx_ref, w_ref, b_ref, o_ref):
    u = zc_ref[...].astype(F32) * zx_ref[...].astype(F32)
    t = u.shape[0]
    row = lax.broadcasted_iota(jnp.int32, u.shape, 0)
    prev = jnp.where(row >= 1, pltpu.roll(u, 1, 0), 0.0)
    nxt = jnp.where(row < t - 1, pltpu.roll(u, t - 1, 0), 0.0)
    y = prev * w_ref[0:1, :] + u * w_ref[1:2, :] + nxt * w_ref[2:3, :] + b_ref[...]
    o_ref[...] = (zb_ref[...].astype(F32) * y).astype(o_ref.dtype)


def _rope_tables(t):
    rows = t // GRID_W
    row = jnp.repeat(jnp.arange(rows), GRID_W).astype(F32)
    col = jnp.tile(jnp.arange(GRID_W), rows).astype(F32)
    n = ROPE_AXIS_DIM // 2
    inv = ROPE_THETA ** (-jnp.arange(n, dtype=F32) / n)
    ar, ac = row[:, None] * inv, col[:, None] * inv
    ang = jnp.concatenate([ar, ar, ac, ac], axis=-1)
    sign = jnp.tile(jnp.concatenate([-jnp.ones((n,), F32), jnp.ones((n,), F32)]), 2)
    cos = jnp.cos(ang)
    sin = jnp.sin(ang) * sign
    return jnp.tile(cos, (1, 2)), jnp.tile(sin, (1, 2))


def _rope(x, cos, sin):
    lane = lax.broadcasted_iota(jnp.int32, x.shape, 1)
    n = ROPE_AXIS_DIM // 2
    w = x.shape[1]
    partner = jnp.where((lane & (2 * n - 1)) < n, pltpu.roll(x, w - n, 1), pltpu.roll(x, n, 1))
    return x * cos + partner * sin


ATTN_ONES_ROWS = 16
ATTN_HEADS_PER_STEP = 2


def _attn_kernel(*refs, t_lat, lam_init, tq, n_cast):
    lam_ref, q_ref, kl_ref, vl_ref, kc_ref, vc_ref, cos_ref, sin_ref, dn_ref = refs[:9]
    conv_in = refs[9:14]
    n_in = 14 + n_cast
    o_ref, conv_ref = refs[n_in:n_in + 2]
    k_scr, vt_scr, s0_scr, m0_scr, s1_scr, m1_scr = refs[-6:]

    @pl.when(pl.program_id(1) == 0)
    def _():
        _short_conv_block(*conv_in, conv_ref)
        _cast_blocks(refs[14:n_in], refs[n_in + 2:n_in + 2 + n_cast])

    hw = 2 * HEAD_DIM
    t_all = k_scr.shape[1]

    for hd in range(ATTN_HEADS_PER_STEP):
        cols = slice(hd * hw, (hd + 1) * hw)
        k_scr[hd, 0:t_lat, :] = _rope(kl_ref[:, cols].astype(F32), cos_ref[...], sin_ref[...]).astype(BF16)
        k_scr[hd, t_lat:, :] = kc_ref[:, cols].astype(BF16)
        vt_scr[hd, 0:hw, 0:t_lat] = vl_ref[:, cols].astype(F32).T.astype(BF16)
        vt_scr[hd, 0:hw, t_lat:] = vc_ref[:, cols].astype(F32).T.astype(BF16)
        vt_scr[hd, hw:, :] = jnp.ones((ATTN_ONES_ROWS, t_all), BF16)

    lv = lam_ref[...]
    lam = (jnp.exp(jnp.sum(lv[0:1] * lv[1:2], axis=-1, keepdims=True))
           - jnp.exp(jnp.sum(lv[2:3] * lv[3:4], axis=-1, keepdims=True)) + lam_init)
    lane = lax.broadcasted_iota(jnp.int32, (tq, hw), 1)

    n_blocks = t_lat // tq
    slots = ((s0_scr, m0_scr), (s1_scr, m1_scr))

    def scores(hd, i, slot):
        rows = pl.ds(pl.multiple_of(i * tq, tq), tq)
        q = (_rope(q_ref[rows, hd * hw:(hd + 1) * hw].astype(F32), cos_ref[rows, :], sin_ref[rows, :])
             * (HEAD_DIM ** -0.5 * math.log2(math.e)))
        s_scr, m_scr = slots[slot]
        for half in range(2):
            qb = jnp.where((lane // HEAD_DIM) == half, q, 0.0).astype(BF16)
            s = _dot_nt(k_scr[hd], qb)
            s_scr[half] = s
            m_scr[half] = jnp.max(s, axis=0, keepdims=True)

    def outputs(hd, i, slot):
        rows = pl.ds(pl.multiple_of(i * tq, tq), tq)
        s_scr, m_scr = slots[slot]
        outs = []
        for half in range(2):
            p = jnp.exp2(s_scr[half] - m_scr[half]).astype(BF16)
            acc = _dot(vt_scr[hd], p)
            outs.append(acc[0:hw, :] / acc[hw:hw + 1, :])
        o = (outs[0] - lam * outs[1]).T
        o_ref[rows, hd * hw:(hd + 1) * hw] = (_rms(o) * dn_ref[...] * (1.0 - lam_init)).astype(o_ref.dtype)

    assert n_blocks % 2 == 0
    scores(0, 0, 0)
    for hd in range(ATTN_HEADS_PER_STEP):
        def body(j, carry, hd=hd):
            scores(hd, 2 * j + 1, 1)
            outputs(hd, 2 * j, 0)
            scores(hd, 2 * j + 2, 0)
            outputs(hd, 2 * j + 1, 1)
            return carry

        lax.fori_loop(0, n_blocks // 2 - 1, body, 0)
        scores(hd, n_blocks - 1, 1)
        outputs(hd, n_blocks - 2, 0)
        if hd + 1 < ATTN_HEADS_PER_STEP:
            scores(hd + 1, 0, 0)
        outputs(hd, n_blocks - 1, 1)


def _odd_mixers(z, lam_vecs, dnorm, lam_init, conv_w, conv_b, bn, t_lat, t_ctx, ctx_row0, cast=()):
    hw = 2 * HEAD_DIM
    wb = ATTN_HEADS_PER_STEP * hw
    q0 = 3 * CONV_WIDTH // wb
    k0 = q0 + DIFF_HEADS // ATTN_HEADS_PER_STEP
    v0 = k0 + DIFF_HEADS // ATTN_HEADS_PER_STEP
    tq = ATTN_Q_BLOCK
    t_all = t_lat + t_ctx
    coff = ctx_row0 // t_ctx
    cos, sin = _rope_tables(t_lat)
    c_args, c_in, c_shape, c_out = _cast_riders(cast, lambda b, p: b, bn)
    outs = pl.pallas_call(
        functools.partial(_attn_kernel, t_lat=t_lat, lam_init=lam_init, tq=tq, n_cast=len(cast)),
        out_shape=[jax.ShapeDtypeStruct((bn * t_lat, DIFF_HEADS * DIFF_DV), BF16),
                   jax.ShapeDtypeStruct((bn * t_lat, CONV_WIDTH), BF16)] + c_shape,
        grid=(bn, DIFF_HEADS // ATTN_HEADS_PER_STEP),
        in_specs=[
            pl.BlockSpec((4, HEAD_DIM), lambda b, p: (0, 0)),
            pl.BlockSpec((t_lat, wb), lambda b, p: (b, q0 + p)),
            pl.BlockSpec((t_lat, wb), lambda b, p: (b, k0 + p)),
            pl.BlockSpec((t_lat, wb), lambda b, p: (b, v0 + p)),
            pl.BlockSpec((t_ctx, wb), lambda b, p: (coff + b, k0 + p)),
            pl.BlockSpec((t_ctx, wb), lambda b, p: (coff + b, v0 + p)),
            pl.BlockSpec((t_lat, hw), lambda b, p: (0, 0)),
            pl.BlockSpec((t_lat, hw), lambda b, p: (0, 0)),
            pl.BlockSpec((1, DIFF_DV), lambda b, p: (0, 0)),
            pl.BlockSpec((t_lat, CONV_WIDTH), lambda b, p: (b, 0)),
            pl.BlockSpec((t_lat, CONV_WIDTH), lambda b, p: (b, 1)),
            pl.BlockSpec((t_lat, CONV_WIDTH), lambda b, p: (b, 2)),
            pl.BlockSpec((3, CONV_WIDTH), lambda b, p: (0, 0)),
            pl.BlockSpec((1, CONV_WIDTH), lambda b, p: (0, 0)),
        ] + c_in,
        out_specs=[pl.BlockSpec((t_lat, wb), lambda b, p: (b, p)),
                   pl.BlockSpec((t_lat, CONV_WIDTH), lambda b, p: (b, 0))] + c_out,
        scratch_shapes=[pltpu.VMEM((ATTN_HEADS_PER_STEP, t_all, hw), BF16),
                        pltpu.VMEM((ATTN_HEADS_PER_STEP, hw + ATTN_ONES_ROWS, t_all), BF16),
                        pltpu.VMEM((2, t_all, tq), F32), pltpu.VMEM((2, 1, tq), F32),
                        pltpu.VMEM((2, t_all, tq), F32), pltpu.VMEM((2, 1, tq), F32)],
        compiler_params=_params("arbitrary", "arbitrary"),
        name="diff_attn",
    )(lam_vecs, z, z, z, z, z, cos, sin, dnorm.reshape(1, DIFF_DV), z, z, z, conv_w,
      conv_b.reshape(1, CONV_WIDTH), *c_args)
    return outs


def kernel(x, c, ctx, c_ctx, ada_w, ada_b, norm_ffn1, norm_mix, norm_ffn2, ffn1_w_in, ffn1_w_out, ffn2_w_in,
           ffn2_w_out, mix_w_out, even_w_in, gla_gate_w, gla_gate_b, gla_norm, odd_w_in, conv_w, conv_b,
           lambda_q1, lambda_k1, lambda_q2, lambda_k2, diff_norm, final_norm):
    assert DEPTH == 2
    bn, t_lat, d = x.shape
    t_ctx = ctx.shape[1]
    assert bn < MOD_ROWS
    ctx_row = bn

    cond = jnp.concatenate([c, c_ctx[None, :], jnp.zeros((MOD_ROWS - bn - 1, d), F32)], axis=0)
    n_mod_steps = DEPTH * MOD_TILES
    mods, w1i0, w1o0 = _modulation(cond, ada_w, ada_b,
                                   cast=((ffn1_w_in, 0, n_mod_steps), (ffn1_w_out, 0, n_mod_steps)))
    w_even_t = jnp.swapaxes(even_w_in[0], 0, 1).astype(BF16)

    tm = FFN_ROWS
    n_lat = bn * t_lat
    lat_blocks = n_lat // tm
    lat_row = lambda i: i // (t_lat // tm)
    all_row = lambda i: jnp.where(i < lat_blocks, i // (t_lat // tm), ctx_row)
    g1 = norm_ffn1.reshape(DEPTH, 1, d)
    gm = norm_mix.reshape(DEPTH, 1, d)
    g2 = norm_ffn2.reshape(DEPTH, 1, d)
    cast_gla = ((ffn2_w_in, 0, bn), (ffn2_w_out, 0, bn), (mix_w_out, 0, bn))
    cast_ffn = ((ffn1_w_in, 1, FFN_CAST_STEPS), (ffn1_w_out, 1, FFN_CAST_STEPS // 2), (odd_w_in, 0, FFN_CAST_STEPS))
    cast_attn = ((ffn2_w_in, 1, bn), (ffn2_w_out, 1, bn), (mix_w_out, 1, bn))

    proj0 = (gm, w_even_t, 2 * GLA_GATE_RANK, True)
    h, z, zg = _ffn((x.reshape(n_lat, d), ctx.reshape(bn * t_ctx, d)), mods, 0, 0, all_row, g1, w1i0, w1o0,
                    tm=tm, n_first=lat_blocks, proj=proj0)
    yf_l = _fourier(z, t_lat, bn)
    yf_c = _fourier(z, t_ctx, bn, row0=n_lat)
    yg_c, yg_l, w2i0, w2o0, wmix0 = _gla(z, zg, gla_gate_w[0], gla_gate_b[0], gla_norm[0], bn, t_ctx, t_lat, n_lat,
                                         cast=cast_gla)
    h, w1i1, w1o1, w_odd = _ffn(h, mods, 0, 6, all_row, g2, w2i0, w2o0, tm=tm, n_first=lat_blocks,
                                mix=((yf_l, yf_c), (yg_l, yg_c), wmix0), cast=cast_ffn)

    h, z = _ffn(h, mods, 1, 0, all_row, g1, w1i1, w1o1, tm=tm, proj=(gm, w_odd, 0, False))
    lam_init = 0.8 - 0.6 * math.exp(-0.3 * 1)
    lam_vecs = jnp.stack([lambda_q1[0], lambda_k1[0], lambda_q2[0], lambda_k2[0]]).astype(F32)
    y_att, y_conv, w2i1, w2o1, wmix1 = _odd_mixers(z, lam_vecs, diff_norm[0], lam_init, conv_w[0], conv_b[0], bn,
                                                   t_lat, t_ctx, n_lat, cast=cast_attn)
    h = _ffn(h, mods, 1, 6, lat_row, g2, w2i1, w2o1, tm=tm, mix=(y_conv, y_att, wmix1),
             final_gain=final_norm.reshape(1, d), n_rows=n_lat)
    return h.reshape(bn, t_lat, d)
```

```python
import functools
import math

import numpy as np
import jax
import jax.numpy as jnp
from jax import lax
from jax.experimental import pallas as pl
from jax.experimental.pallas import tpu as pltpu

D_MODEL = 1024
DEPTH = 2
GRID_W = 64
HEAD_DIM = 64
N_MOD = 9
FFN_HIDDEN = 2816
NORM_EPS = 1e-6
FOURIER_GROUPS = 4
FOURIER_WIDTH = FOURIER_GROUPS * HEAD_DIM
GLA_HEADS = 6
GLA_DK = 64
GLA_DV = 128
GLA_GATE_RANK = 16
GLA_TAU = 16.0
GLA_CHUNK = 64
GLA_GROUP = 8
FFN_ROWS = 512
FFN_TILE = 256
FFN_CAST_STEPS = 32
PROJ_TILE = 512
DFT_ROWS = 512
ATTN_Q_BLOCK = 256
CONV_WIDTH = 4 * HEAD_DIM
DIFF_HEADS = 6
DIFF_DV = 2 * HEAD_DIM
ROPE_THETA = 10000.0
ROPE_AXIS_DIM = HEAD_DIM // 2

MOD_ROWS = 16
MOD_TILES = 4
VMEM_LIMIT = 48 * 1024 * 1024

F32 = jnp.float32
BF16 = jnp.bfloat16


def _params(*sem):
    return pltpu.CompilerParams(dimension_semantics=sem, vmem_limit_bytes=VMEM_LIMIT)


def _sigmoid(x):
    return 1.0 / (1.0 + jnp.exp(-x))


def _dot(a, b):
    return jnp.dot(a, b, preferred_element_type=F32)


def _dot_nt(a, b):
    return lax.dot_general(a, b, (((1,), (1,)), ((), ())), preferred_element_type=F32)


def _dot_tn(a, b):
    return lax.dot_general(a, b, (((0,), (0,)), ((), ())), preferred_element_type=F32)


def _rms(x):
    return x * lax.rsqrt(jnp.mean(x * x, axis=-1, keepdims=True) + NORM_EPS)


def _mod_kernel(*refs, n_cast):
    cond_ref, w_ref, b_ref = refs[:3]
    o_ref = refs[3 + n_cast]
    _cast_blocks(refs[3:3 + n_cast], refs[4 + n_cast:])
    c = cond_ref[...]
    s = c * _sigmoid(c)
    w = w_ref[...]
    w_hi = w.astype(BF16)
    w_lo = (w - w_hi.astype(F32)).astype(BF16)
    s_hi = s.astype(BF16)
    s_lo = (s - s_hi.astype(F32)).astype(BF16)
    o_ref[...] = _dot(s_hi, w_hi) + _dot(s_lo, w_hi) + _dot(s_hi, w_lo) + b_ref[...]


def _modulation(cond, ada_w, ada_b, cast=()):
    n = N_MOD * D_MODEL
    tn = n // MOD_TILES
    c_args, c_in, c_shape, c_out = _cast_riders(cast, lambda l, j: l * MOD_TILES + j, DEPTH * MOD_TILES)
    outs = pl.pallas_call(
        functools.partial(_mod_kernel, n_cast=len(cast)),
        out_shape=[jax.ShapeDtypeStruct((DEPTH, MOD_ROWS, n), F32)] + c_shape,
        grid=(DEPTH, MOD_TILES),
        in_specs=[
            pl.BlockSpec((MOD_ROWS, D_MODEL), lambda l, j: (0, 0)),
            pl.BlockSpec((None, D_MODEL, tn), lambda l, j: (l, 0, j)),
            pl.BlockSpec((None, 1, tn), lambda l, j: (l, 0, j)),
        ] + c_in,
        out_specs=[pl.BlockSpec((None, MOD_ROWS, tn), lambda l, j: (l, 0, j))] + c_out,
        compiler_params=_params("arbitrary", "arbitrary"),
        name="modulation",
    )(cond, ada_w, ada_b.reshape(DEPTH, 1, n), *c_args)
    return (outs[0].reshape(DEPTH, MOD_ROWS, N_MOD, 1, D_MODEL), *outs[1:])


def _mod_spec(layer, k, row_of_block):
    return pl.BlockSpec((None, None, None, 1, D_MODEL),
                        lambda i, *_: (layer, row_of_block(i), k, 0, 0))


def _gain_spec(layer):
    return pl.BlockSpec((None, 1, D_MODEL), lambda i, *_: (layer, 0, 0))


def _resident(block_shape, index_map):
    return pl.BlockSpec(block_shape, index_map, pipeline_mode=pl.Buffered(1))


def _cast_riders(sources, step_of, n_steps):
    args, in_specs, out_shape, out_specs = [], [], [], []
    for arr, lead, nb in sources:
        _, r, c = arr.shape
        rows, stride = r // nb, n_steps // nb
        assert rows * nb == r and stride * nb == n_steps and rows % 16 == 0
        idx = lambda *g, nb=nb, stride=stride: jnp.minimum(step_of(*g) // stride, nb - 1)
        args.append(arr)
        in_specs.append(pl.BlockSpec((None, rows, c), lambda *g, idx=idx, lead=lead: (lead, idx(*g), 0)))
        out_shape.append(jax.ShapeDtypeStruct((r, c), BF16))
        out_specs.append(pl.BlockSpec((rows, c), lambda *g, idx=idx: (idx(*g), 0)))
    return args, in_specs, out_shape, out_specs


def _cast_blocks(src_refs, dst_refs):
    for src, dst in zip(src_refs, dst_refs):
        dst[...] = src[...].astype(BF16)


def _ffn_kernel(*refs, with_mix, with_proj, with_gate, with_final, paired, n_first, n_cast, proj_nt):
    it = iter(refs)
    pairs = iter(paired)

    def rows_in():
        a = next(it)
        if not next(pairs):
            return a[...]
        b = next(it)
        return jnp.where(pl.program_id(0) < n_first, a[...], b[...])

    x = rows_in()
    if with_mix:
        ma, mb = rows_in(), rows_in()
        wm_ref, gm_ref = next(it), next(it)
    sh_ref, sc_ref, g_ref, gain_ref, wi_ref, wo_ref = (next(it) for _ in range(6))
    if with_proj:
        psh_ref, psc_ref, pgain_ref, wp_ref = (next(it) for _ in range(4))
    if with_final:
        fn_ref = next(it)
    cast_src = [next(it) for _ in range(n_cast)]
    o_ref = next(it)
    if with_proj:
        z_ref = next(it)
    if with_gate:
        zg_ref = next(it)
    cast_dst = [next(it) for _ in range(n_cast)]
    a_scr = next(it)
    _cast_blocks(cast_src, cast_dst)

    if with_mix:
        ka = ma.shape[1]
        x = x + gm_ref[...] * (_dot(ma, wm_ref[0:ka, :]) + _dot(mb, wm_ref[ka:, :]))
    xn = (_rms(x) * gain_ref[...] * (1.0 + sc_ref[...]) + sh_ref[...]).astype(BF16)
    for j in range(FFN_HIDDEN // FFN_TILE):
        lo = j * FFN_TILE
        g = _dot(xn, wi_ref[:, lo:lo + FFN_TILE])
        u = _dot(xn, wi_ref[:, FFN_HIDDEN + lo:FFN_HIDDEN + lo + FFN_TILE])
        a_scr[:, lo:lo + FFN_TILE] = (g * _sigmoid(g) * u).astype(BF16)
    out = x + (0.5 * g_ref[...]) * _dot(a_scr[...], wo_ref[...])
    if with_final:
        out = _rms(out) * fn_ref[...]
    o_ref[...] = out
    if with_proj:
        xm = (_rms(out) * pgain_ref[...] * (1.0 + psc_ref[...]) + psh_ref[...]).astype(BF16)
        n_out = z_ref.shape[1]
        project = (lambda lo, hi: _dot_nt(xm, wp_ref[lo:hi, :])) if proj_nt else (lambda lo, hi: _dot(xm, wp_ref[:, lo:hi]))
        for lo in range(0, n_out, PROJ_TILE):
            z_ref[:, lo:lo + PROJ_TILE] = project(lo, lo + PROJ_TILE).astype(z_ref.dtype)
        if with_gate:
            zg_ref[...] = project(n_out, n_out + zg_ref.shape[1])


def _ffn(h, mods, layer, mod_base, row_of_block, gain, w_in, w_out, *, tm, n_first=None, n_rows=None, mix=None,
         proj=None, final_gain=None, cast=()):
    m = n_rows or (sum(a.shape[0] for a in h) if isinstance(h, tuple) else h.shape[0])
    row = lambda i: (i, 0)
    args, specs, paired = [], [], []

    def add_rows(a):
        paired.append(isinstance(a, tuple))
        if paired[-1]:
            first, second = a
            assert first.shape[0] == n_first * tm
            args.extend([first, second])
            specs.extend([pl.BlockSpec((tm, first.shape[1]), lambda i: (jnp.minimum(i, n_first - 1), 0)),
                          pl.BlockSpec((tm, second.shape[1]), lambda i: (jnp.maximum(i - n_first, 0), 0))])
        else:
            args.append(a)
            specs.append(pl.BlockSpec((tm, a.shape[1]), row))

    add_rows(h)
    if mix is not None:
        ma, mb, w_mix = mix
        add_rows(ma)
        add_rows(mb)
        args += [w_mix, mods]
        specs += [_resident(w_mix.shape, lambda i: (0, 0)), _mod_spec(layer, 5, row_of_block)]
    args += [mods, mods, mods, gain, w_in, w_out]
    specs += [
        _mod_spec(layer, mod_base, row_of_block),
        _mod_spec(layer, mod_base + 1, row_of_block),
        _mod_spec(layer, mod_base + 2, row_of_block),
        _gain_spec(layer),
        _resident((D_MODEL, 2 * FFN_HIDDEN), lambda i: (0, 0)),
        _resident((FFN_HIDDEN, D_MODEL), lambda i: (0, 0)),
    ]
    out_shape = [jax.ShapeDtypeStruct((m, D_MODEL), F32)]
    out_specs = [pl.BlockSpec((tm, D_MODEL), row)]
    n_gate, proj_nt = 0, False
    if proj is not None:
        pgain, wp, n_gate, proj_nt = proj
        n_out = wp.shape[0 if proj_nt else 1] - n_gate
        assert n_out % PROJ_TILE == 0
        args += [mods, mods, pgain, wp]
        specs += [_mod_spec(layer, 3, row_of_block), _mod_spec(layer, 4, row_of_block), _gain_spec(layer),
                  _resident(wp.shape, lambda i: (0, 0))]
        out_shape.append(jax.ShapeDtypeStruct((m, n_out), BF16))
        out_specs.append(pl.BlockSpec((tm, n_out), row))
        if n_gate:
            out_shape.append(jax.ShapeDtypeStruct((m, n_gate), F32))
            out_specs.append(pl.BlockSpec((tm, n_gate), row))
    if final_gain is not None:
        args.append(final_gain)
        specs.append(pl.BlockSpec((1, D_MODEL), lambda i: (0, 0)))
    c_args, c_in, c_shape, c_out = _cast_riders(cast, lambda i: i, FFN_CAST_STEPS)
    assert not cast or m // tm >= FFN_CAST_STEPS
    args += c_args
    specs += c_in
    out_shape += c_shape
    out_specs += c_out
    outs = pl.pallas_call(
        functools.partial(_ffn_kernel, with_mix=mix is not None, with_proj=proj is not None, with_gate=n_gate > 0,
                          with_final=final_gain is not None, paired=tuple(paired), n_first=n_first,
                          n_cast=len(cast), proj_nt=proj_nt),
        out_shape=out_shape,
        grid=(m // tm,),
        in_specs=specs,
        out_specs=out_specs,
        scratch_shapes=[pltpu.VMEM((tm, FFN_HIDDEN), BF16)],
        compiler_params=_params("arbitrary" if cast else "parallel"),
        name="ffn",
    )(*args)
    return outs if len(outs) > 1 else outs[0]


def _dft_tables(t, tq):
    k = (np.arange(tq, dtype=np.int64)[:, None] * np.arange(t, dtype=np.int64)[None, :]) % t
    ang = 2.0 * np.pi * k.astype(np.float64) / t
    pos = np.concatenate([np.cos(ang), -np.sin(ang)], axis=1).astype(np.float32)
    kc = (np.arange(HEAD_DIM)[:, None] * np.arange(HEAD_DIM)[None, :]) % HEAD_DIM
    angc = 2.0 * np.pi * kc.astype(np.float64) / HEAD_DIM
    eye = np.eye(FOURIER_GROUPS)
    chan = np.concatenate([np.kron(eye, np.cos(angc)), np.kron(eye, np.sin(angc))], axis=1).astype(np.float32)
    return pos, chan


def _dft_kernel(p_ref, z_ref, c_ref, o_ref, p_scr, ab_scr, *, scale, quarter_turns):
    i, b = pl.program_id(0), pl.program_id(1)
    t = p_ref.shape[1] // 2

    @pl.when(b == 0)
    def _():
        col = lax.broadcasted_iota(jnp.int32, (1, t), 1)
        k = (i * quarter_turns * col) & 3
        ca = jnp.where(k == 0, 1.0, jnp.where(k == 2, -1.0, 0.0))
        sa = jnp.where(k == 1, 1.0, jnp.where(k == 3, -1.0, 0.0))
        c0, n0 = p_ref[:, 0:t], p_ref[:, t:]
        p_scr[:, 0:t] = (ca * c0 + sa * n0).astype(BF16)
        p_scr[:, t:] = (ca * n0 - sa * c0).astype(BF16)

    @pl.when(i == 0)
    def _():
        ab = _dot(z_ref[...], c_ref[...].astype(BF16))
        ab_scr[b, 0:t, :] = ab[:, :FOURIER_WIDTH].astype(BF16)
        ab_scr[b, t:, :] = ab[:, FOURIER_WIDTH:].astype(BF16)

    o_ref[...] = (_dot(p_scr[...], ab_scr[b]) * scale).astype(BF16)


def _fourier(z, t, bn, row0=0):
    tq = min(t, DFT_ROWS)
    assert (4 * tq) % t == 0
    pos, chan = _dft_tables(t, tq)
    off = row0 // t
    return pl.pallas_call(
        functools.partial(_dft_kernel, scale=1.0 / math.sqrt(t * HEAD_DIM), quarter_turns=4 * tq // t),
        out_shape=jax.ShapeDtypeStruct((bn * t, FOURIER_WIDTH), BF16),
        grid=(t // tq, bn),
        in_specs=[_resident((tq, 2 * t), lambda i, b: (0, 0)),
                  pl.BlockSpec((t, FOURIER_WIDTH), lambda i, b: (off + b, 0)),
                  pl.BlockSpec((FOURIER_WIDTH, 2 * FOURIER_WIDTH), lambda i, b: (0, 0))],
        out_specs=pl.BlockSpec((tq, FOURIER_WIDTH), lambda i, b: (b * (t // tq) + i, 0)),
        scratch_shapes=[pltpu.VMEM((tq, 2 * t), BF16), pltpu.VMEM((bn, 2 * t, FOURIER_WIDTH), BF16)],
        compiler_params=_params("arbitrary", "arbitrary"),
        name="dft",
    )(jnp.asarray(pos), z, jnp.asarray(chan))


def _gla_segment(q_ref, k_ref, v_ref, r_ref, g_ref, y_ref, gw_ref, gb_ref, gn_ref,
                 b_scr, o_scr, s_scr, n_rows):
    c_len = GLA_CHUNK
    n = n_rows // c_len
    pair_k = 2 * GLA_DK
    pair_v = 2 * GLA_DV

    for d in range(2):
        zg = g_ref[:, d * GLA_GATE_RANK:(d + 1) * GLA_GATE_RANK]
        b_scr[d, 0:n_rows, :] = _dot(zg.astype(BF16), gw_ref[d].astype(BF16)) + gb_ref[d]

    lane = lax.broadcasted_iota(jnp.int32, (c_len, pair_k), 1)
    col = lax.broadcasted_iota(jnp.int32, (c_len, pair_v), 1)
    ti = lax.broadcasted_iota(jnp.int32, (c_len, c_len), 0)
    tj = lax.broadcasted_iota(jnp.int32, (c_len, c_len), 1)

    group = math.gcd(GLA_GROUP, n)
    n_it = n // group
    keep2 = [jnp.concatenate([m, m], axis=0) for m in (tj <= ti, tj >= ti)]
    ti2 = lax.broadcasted_iota(jnp.int32, (c_len, 2 * c_len), 0)
    tj2 = lax.broadcasted_iota(jnp.int32, (c_len, 2 * c_len), 1) & (c_len - 1)
    tri2 = [jnp.where(m, 1.0, 0.0).astype(BF16) for m in (tj2 <= ti2, tj2 >= ti2)]
    gn = gn_ref[...]

    def finish(rows, o):
        for hh in range(2):
            sl = slice(hh * GLA_DV, (hh + 1) * GLA_DV)
            rr = r_ref[rows, sl].astype(F32)
            y_ref[rows, sl] = (_rms(o[:, sl]) * gn * (rr * _sigmoid(rr))).astype(y_ref.dtype)

    def chunk_rows(c):
        start = c * c_len
        return pl.ds(start if isinstance(start, int) else pl.multiple_of(start, c_len), c_len)

    def body(i, carry, phase):
        units = []
        for d in range(2):
            for g in range(group):
                c = i * group + g if d == 0 else n - 1 - (i * group + g)
                rows = chunk_rows(c)
                logit = b_scr[d, rows, :]
                lg = (jnp.minimum(logit, 0.0) - jnp.log(1.0 + jnp.exp(-jnp.abs(logit)))) * (1.0 / GLA_TAU)
                hi = lg.astype(BF16)
                rest = (lg - hi.astype(F32)).astype(BF16)
                units.append(dict(d=d, rows=rows, b=_dot(tri2[d], jnp.concatenate([hi, rest], axis=0))))
        for u in units:
            d, rows, b = u["d"], u["rows"], u["b"]
            bl = b[c_len - 1:c_len, :] if d == 0 else b[0:1, :]
            k = k_ref[rows, :].astype(F32)
            v16 = v_ref[rows, :].astype(BF16)
            qd = q_ref[rows, :].astype(F32) * (GLA_DK ** -0.5) * jnp.exp(b)
            kd = (k * jnp.exp(-b)).astype(BF16)
            kdec = (k * jnp.exp(bl - b)).astype(BF16)
            q2 = jnp.concatenate([jnp.where((lane // GLA_DK) == hh, qd, 0.0) for hh in range(2)],
                                 axis=0).astype(BF16)
            att = _dot_nt(q2, kd)
            upd = _dot_tn(v16, kdec)
            u.update(q2=q2, v16=v16, att=att, upd=upd, dec=jnp.exp(bl))
        for d in range(2):
            s = s_scr[d]
            for u in units:
                if u["d"] == d:
                    u["o"] = _dot_nt(u["q2"], s.astype(BF16))
                    s = s * u["dec"] + u["upd"]
            s_scr[d] = s
        for u in units:
            att = jnp.where(keep2[u["d"]], u["att"], 0.0).astype(BF16)
            pv = _dot(att, u["v16"])
            both = u["o"] + pv
            u["o"] = jnp.where((col // GLA_DV) == 0, both[0:c_len, :], both[c_len:, :])
            if phase == "first":
                o_scr[u["d"], u["rows"], :] = u["o"]
            elif phase == "second":
                finish(u["rows"], u["o"] + o_scr[1 - u["d"], u["rows"], :])
        if phase == "only":
            for g in range(group):
                finish(units[g]["rows"], units[g]["o"] + units[group + n - 1 - g]["o"])
        return carry

    if n_it == 1:
        body(0, 0, "only")
    else:
        assert n_it % 2 == 0
        lax.fori_loop(0, n_it // 2, functools.partial(body, phase="first"), 0)
        lax.fori_loop(n_it // 2, n_it, functools.partial(body, phase="second"), 0)


def _gla_kernel(*refs, t_ctx, t_lat, n_cast):
    qc_ref, kc_ref, vc_ref, rc_ref, gc_ref, ql_ref, kl_ref, vl_ref, rl_ref, gl_ref, gw_ref, gb_ref, gn_ref = refs[:13]
    yc_ref, yl_ref = refs[13 + n_cast:15 + n_cast]
    b_scr, o_scr, s_scr = refs[-3:]

    @pl.when(pl.program_id(1) == 0)
    def _():
        _cast_blocks(refs[13:13 + n_cast], refs[15 + n_cast:15 + 2 * n_cast])

    s_scr[...] = jnp.zeros_like(s_scr)
    _gla_segment(qc_ref, kc_ref, vc_ref, rc_ref, gc_ref, yc_ref, gw_ref, gb_ref, gn_ref,
                 b_scr, o_scr, s_scr, t_ctx)
    _gla_segment(ql_ref, kl_ref, vl_ref, rl_ref, gl_ref, yl_ref, gw_ref, gb_ref, gn_ref,
                 b_scr, o_scr, s_scr, t_lat)


def _gla(z, zg, gate_w, gate_b, gla_g, bn, t_ctx, t_lat, ctx_row0, cast=()):
    pk, pv = 2 * GLA_DK, 2 * GLA_DV
    q0 = FOURIER_WIDTH // pk
    k0 = (FOURIER_WIDTH + GLA_HEADS * GLA_DK) // pk
    v0 = (FOURIER_WIDTH + 2 * GLA_HEADS * GLA_DK) // pv
    r0 = (FOURIER_WIDTH + 2 * GLA_HEADS * GLA_DK + GLA_HEADS * GLA_DV) // pv

    def seg_specs(t, row0):
        off = row0 // t
        return [pl.BlockSpec((t, pk), lambda b, p: (off + b, q0 + p)),
                pl.BlockSpec((t, pk), lambda b, p: (off + b, k0 + p)),
                pl.BlockSpec((t, pv), lambda b, p: (off + b, v0 + p)),
                pl.BlockSpec((t, pv), lambda b, p: (off + b, r0 + p)),
                pl.BlockSpec((t, 2 * GLA_GATE_RANK), lambda b, p: (off + b, 0))]

    wdt = GLA_HEADS * GLA_DV
    c_args, c_in, c_shape, c_out = _cast_riders(cast, lambda b, p: b, bn)
    return pl.pallas_call(
        functools.partial(_gla_kernel, t_ctx=t_ctx, t_lat=t_lat, n_cast=len(cast)),
        out_shape=[jax.ShapeDtypeStruct((bn * t_ctx, wdt), BF16), jax.ShapeDtypeStruct((bn * t_lat, wdt), BF16)] + c_shape,
        grid=(bn, GLA_HEADS // 2),
        in_specs=seg_specs(t_ctx, ctx_row0) + seg_specs(t_lat, 0) + [
            pl.BlockSpec((2, GLA_GATE_RANK, pk), lambda b, p: (0, 0, p)),
            pl.BlockSpec((2, 1, pk), lambda b, p: (0, 0, p)),
            pl.BlockSpec((1, GLA_DV), lambda b, p: (0, 0))] + c_in,
        out_specs=[pl.BlockSpec((t_ctx, pv), lambda b, p: (b, p)),
                   pl.BlockSpec((t_lat, pv), lambda b, p: (b, p))] + c_out,
        scratch_shapes=[pltpu.VMEM((2, t_lat, pk), F32), pltpu.VMEM((2, t_lat, pv), F32),
                        pltpu.VMEM((2, pv, pk), F32)],
        compiler_params=_params("arbitrary", "arbitrary"),
        name="gla",
    )(z, z, z, z, zg, z, z, z, z, zg, gate_w, gate_b.reshape(2, 1, GLA_HEADS * GLA_DK),
      gla_g.reshape(1, GLA_DV), *c_args)


def _short_conv_block(zb_ref, zc_ref, zx_ref, w_ref, b_ref, o_ref):
    u = zc_ref[...].astype(F32) * zx_ref[...].astype(F32)
    t = u.shape[0]
    row = lax.broadcasted_iota(jnp.int32, u.shape, 0)
    prev = jnp.where(row >= 1, pltpu.roll(u, 1, 0), 0.0)
    nxt = jnp.where(row < t - 1, pltpu.roll(u, t - 1, 0), 0.0)
    y = prev * w_ref[0:1, :] + u * w_ref[1:2, :] + nxt * w_ref[2:3, :] + b_ref[...]
    o_ref[...] = (zb_ref[...].astype(F32) * y).astype(o_ref.dtype)


def _rope_tables(t):
    rows = t // GRID_W
    row = jnp.repeat(jnp.arange(rows), GRID_W).astype(F32)
    col = jnp.tile(jnp.arange(GRID_W), rows).astype(F32)
    n = ROPE_AXIS_DIM // 2
    inv = ROPE_THETA ** (-jnp.arange(n, dtype=F32) / n)
    ar, ac = row[:, None] * inv, col[:, None] * inv
    ang = jnp.concatenate([ar, ar, ac, ac], axis=-1)
    sign = jnp.tile(jnp.concatenate([-jnp.ones((n,), F32), jnp.ones((n,), F32)]), 2)
    cos = jnp.cos(ang)
    sin = jnp.sin(ang) * sign
    return jnp.tile(cos, (1, 2)), jnp.tile(sin, (1, 2))


def _rope(x, cos, sin):
    lane = lax.broadcasted_iota(jnp.int32, x.shape, 1)
    n = ROPE_AXIS_DIM // 2
    w = x.shape[1]
    partner = jnp.where((lane & (2 * n - 1)) < n, pltpu.roll(x, w - n, 1), pltpu.roll(x, n, 1))
    return x * cos + partner * sin


ATTN_ONES_ROWS = 16
ATTN_HEADS_PER_STEP = 3


def _attn_kernel(*refs, t_lat, lam_init, tq, n_cast):
    lam_ref, q_ref, kl_ref, vl_ref, kc_ref, vc_ref, cos_ref, sin_ref, dn_ref = refs[:9]
    conv_in = refs[9:14]
    n_in = 14 + n_cast
    o_ref, conv_ref = refs[n_in:n_in + 2]
    k_scr, vt_scr, s0_scr, m0_scr, s1_scr, m1_scr = refs[-6:]

    _cast_blocks(refs[14:n_in], refs[n_in + 2:n_in + 2 + n_cast])

    @pl.when(pl.program_id(1) == 0)
    def _():
        _short_conv_block(*conv_in, conv_ref)

    hw = 2 * HEAD_DIM
    t_all = k_scr.shape[1]

    for hd in range(ATTN_HEADS_PER_STEP):
        cols = slice(hd * hw, (hd + 1) * hw)
        k_scr[hd, 0:t_lat, :] = _rope(kl_ref[:, cols].astype(F32), cos_ref[...], sin_ref[...]).astype(BF16)
        k_scr[hd, t_lat:, :] = kc_ref[:, cols].astype(BF16)
        vt_scr[hd, 0:hw, 0:t_lat] = vl_ref[:, cols].astype(F32).T.astype(BF16)
        vt_scr[hd, 0:hw, t_lat:] = vc_ref[:, cols].astype(F32).T.astype(BF16)
        vt_scr[hd, hw:, :] = jnp.ones((ATTN_ONES_ROWS, t_all), BF16)

    lv = lam_ref[...]
    lam = (jnp.exp(jnp.sum(lv[0:1] * lv[1:2], axis=-1, keepdims=True))
           - jnp.exp(jnp.sum(lv[2:3] * lv[3:4], axis=-1, keepdims=True)) + lam_init)
    lane = lax.broadcasted_iota(jnp.int32, (tq, hw), 1)

    n_blocks = t_lat // tq
    slots = ((s0_scr, m0_scr), (s1_scr, m1_scr))

    def scores(hd, i, slot):
        rows = pl.ds(pl.multiple_of(i * tq, tq), tq)
        q = (_rope(q_ref[rows, hd * hw:(hd + 1) * hw].astype(F32), cos_ref[rows, :], sin_ref[rows, :])
             * (HEAD_DIM ** -0.5 * math.log2(math.e)))
        s_scr, m_scr = slots[slot]
        for half in range(2):
            qb = jnp.where((lane // HEAD_DIM) == half, q, 0.0).astype(BF16)
            s = _dot_nt(k_scr[hd], qb)
            s_scr[half] = s
            m_scr[half] = jnp.max(s, axis=0, keepdims=True)

    def outputs(hd, i, slot):
        rows = pl.ds(pl.multiple_of(i * tq, tq), tq)
        s_scr, m_scr = slots[slot]
        outs = []
        for half in range(2):
            p = jnp.exp2(s_scr[half] - m_scr[half]).astype(BF16)
            acc = _dot(vt_scr[hd], p)
            outs.append(acc[0:hw, :] / acc[hw:hw + 1, :])
        o = (outs[0] - lam * outs[1]).T
        o_ref[rows, hd * hw:(hd + 1) * hw] = (_rms(o) * dn_ref[...] * (1.0 - lam_init)).astype(o_ref.dtype)

    assert n_blocks % 2 == 0
    scores(0, 0, 0)
    for hd in range(ATTN_HEADS_PER_STEP):
        def body(j, carry, hd=hd):
            scores(hd, 2 * j + 1, 1)
            outputs(hd, 2 * j, 0)
            scores(hd, 2 * j + 2, 0)
            outputs(hd, 2 * j + 1, 1)
            return carry

        lax.fori_loop(0, n_blocks // 2 - 1, body, 0)
        scores(hd, n_blocks - 1, 1)
        outputs(hd, n_blocks - 2, 0)
        if hd + 1 < ATTN_HEADS_PER_STEP:
            scores(hd + 1, 0, 0)
        outputs(hd, n_blocks - 1, 1)


def _odd_mixers(z, lam_vecs, dnorm, lam_init, conv_w, conv_b, bn, t_lat, t_ctx, ctx_row0, cast=()):
    hw = 2 * HEAD_DIM
    wb = ATTN_HEADS_PER_STEP * hw
    q0 = 3 * CONV_WIDTH // wb
    k0 = q0 + DIFF_HEADS // ATTN_HEADS_PER_STEP
    v0 = k0 + DIFF_HEADS // ATTN_HEADS_PER_STEP
    tq = ATTN_Q_BLOCK
    t_all = t_lat + t_ctx
    coff = ctx_row0 // t_ctx
    cos, sin = _rope_tables(t_lat)
    groups = DIFF_HEADS // ATTN_HEADS_PER_STEP
    c_args, c_in, c_shape, c_out = _cast_riders(cast, lambda b, p: b * groups + p, bn * groups)
    outs = pl.pallas_call(
        functools.partial(_attn_kernel, t_lat=t_lat, lam_init=lam_init, tq=tq, n_cast=len(cast)),
        out_shape=[jax.ShapeDtypeStruct((bn * t_lat, DIFF_HEADS * DIFF_DV), BF16),
                   jax.ShapeDtypeStruct((bn * t_lat, CONV_WIDTH), BF16)] + c_shape,
        grid=(bn, groups),
        in_specs=[
            pl.BlockSpec((4, HEAD_DIM), lambda b, p: (0, 0)),
            pl.BlockSpec((t_lat, wb), lambda b, p: (b, q0 + p)),
            pl.BlockSpec((t_lat, wb), lambda b, p: (b, k0 + p)),
            pl.BlockSpec((t_lat, wb), lambda b, p: (b, v0 + p)),
            pl.BlockSpec((t_ctx, wb), lambda b, p: (coff + b, k0 + p)),
            pl.BlockSpec((t_ctx, wb), lambda b, p: (coff + b, v0 + p)),
            pl.BlockSpec((t_lat, hw), lambda b, p: (0, 0)),
            pl.BlockSpec((t_lat, hw), lambda b, p: (0, 0)),
            pl.BlockSpec((1, DIFF_DV), lambda b, p: (0, 0)),
            pl.BlockSpec((t_lat, CONV_WIDTH), lambda b, p: (b, 0)),
            pl.BlockSpec((t_lat, CONV_WIDTH), lambda b, p: (b, 1)),
            pl.BlockSpec((t_lat, CONV_WIDTH), lambda b, p: (b, 2)),
            pl.BlockSpec((3, CONV_WIDTH), lambda b, p: (0, 0)),
            pl.BlockSpec((1, CONV_WIDTH), lambda b, p: (0, 0)),
        ] + c_in,
        out_specs=[pl.BlockSpec((t_lat, wb), lambda b, p: (b, p)),
                   pl.BlockSpec((t_lat, CONV_WIDTH), lambda b, p: (b, 0))] + c_out,
        scratch_shapes=[pltpu.VMEM((ATTN_HEADS_PER_STEP, t_all, hw), BF16),
                        pltpu.VMEM((ATTN_HEADS_PER_STEP, hw + ATTN_ONES_ROWS, t_all), BF16),
                        pltpu.VMEM((2, t_all, tq), F32), pltpu.VMEM((2, 1, tq), F32),
                        pltpu.VMEM((2, t_all, tq), F32), pltpu.VMEM((2, 1, tq), F32)],
        compiler_params=_params("arbitrary", "arbitrary"),
        name="diff_attn",
    )(lam_vecs, z, z, z, z, z, cos, sin, dnorm.reshape(1, DIFF_DV), z, z, z, conv_w,
      conv_b.reshape(1, CONV_WIDTH), *c_args)
    return outs


def kernel(x, c, ctx, c_ctx, ada_w, ada_b, norm_ffn1, norm_mix, norm_ffn2, ffn1_w_in, ffn1_w_out, ffn2_w_in,
           ffn2_w_out, mix_w_out, even_w_in, gla_gate_w, gla_gate_b, gla_norm, odd_w_in, conv_w, conv_b,
           lambda_q1, lambda_k1, lambda_q2, lambda_k2, diff_norm, final_norm):
    assert DEPTH == 2
    bn, t_lat, d = x.shape
    t_ctx = ctx.shape[1]
    assert bn < MOD_ROWS
    ctx_row = bn

    cond = jnp.concatenate([c, c_ctx[None, :], jnp.zeros((MOD_ROWS - bn - 1, d), F32)], axis=0)
    n_mod_steps = DEPTH * MOD_TILES
    mods, w1i0, w1o0 = _modulation(cond, ada_w, ada_b,
                                   cast=((ffn1_w_in, 0, n_mod_steps), (ffn1_w_out, 0, n_mod_steps)))
    w_even_t = jnp.swapaxes(even_w_in[0], 0, 1).astype(BF16)

    tm = FFN_ROWS
    n_lat = bn * t_lat
    lat_blocks = n_lat // tm
    lat_row = lambda i: i // (t_lat // tm)
    all_row = lambda i: jnp.where(i < lat_blocks, i // (t_lat // tm), ctx_row)
    g1 = norm_ffn1.reshape(DEPTH, 1, d)
    gm = norm_mix.reshape(DEPTH, 1, d)
    g2 = norm_ffn2.reshape(DEPTH, 1, d)
    cast_gla = ((ffn2_w_in, 0, bn), (ffn2_w_out, 0, bn), (mix_w_out, 0, bn))
    cast_ffn = ((ffn1_w_in, 1, FFN_CAST_STEPS), (ffn1_w_out, 1, FFN_CAST_STEPS // 2), (odd_w_in, 0, FFN_CAST_STEPS))
    n_attn = bn * (DIFF_HEADS // ATTN_HEADS_PER_STEP)
    cast_attn = ((ffn2_w_in, 1, n_attn), (ffn2_w_out, 1, n_attn), (mix_w_out, 1, n_attn))

    proj0 = (gm, w_even_t, 2 * GLA_GATE_RANK, True)
    h, z, zg = _ffn((x.reshape(n_lat, d), ctx.reshape(bn * t_ctx, d)), mods, 0, 0, all_row, g1, w1i0, w1o0,
                    tm=tm, n_first=lat_blocks, proj=proj0)
    yf_l = _fourier(z, t_lat, bn)
    yf_c = _fourier(z, t_ctx, bn, row0=n_lat)
    yg_c, yg_l, w2i0, w2o0, wmix0 = _gla(z, zg, gla_gate_w[0], gla_gate_b[0], gla_norm[0], bn, t_ctx, t_lat, n_lat,
                                         cast=cast_gla)
    h, w1i1, w1o1, w_odd = _ffn(h, mods, 0, 6, all_row, g2, w2i0, w2o0, tm=tm, n_first=lat_blocks,
                                mix=((yf_l, yf_c), (yg_l, yg_c), wmix0), cast=cast_ffn)

    h, z = _ffn(h, mods, 1, 0, all_row, g1, w1i1, w1o1, tm=tm, proj=(gm, w_odd, 0, False))
    lam_init = 0.8 - 0.6 * math.exp(-0.3 * 1)
    lam_vecs = jnp.stack([lambda_q1[0], lambda_k1[0], lambda_q2[0], lambda_k2[0]]).astype(F32)
    y_att, y_conv, w2i1, w2o1, wmix1 = _odd_mixers(z, lam_vecs, diff_norm[0], lam_init, conv_w[0], conv_b[0], bn,
                                                   t_lat, t_ctx, n_lat, cast=cast_attn)
    h = _ffn(h, mods, 1, 6, lat_row, g2, w2i1, w2o1, tm=tm, mix=(y_conv, y_att, wmix1),
             final_gain=final_norm.reshape(1, d), n_rows=n_lat)
    return h.reshape(bn, t_lat, d)
```

```python
import functools
import math

import numpy as np
import jax
import jax.numpy as jnp
from jax import lax
from jax.experimental import pallas as pl
from jax.experimental.pallas import tpu as pltpu

D_MODEL = 1024
DEPTH = 2
GRID_W = 64
HEAD_DIM = 64
N_MOD = 9
FFN_HIDDEN = 2816
NORM_EPS = 1e-6
FOURIER_GROUPS = 4
FOURIER_WIDTH = FOURIER_GROUPS * HEAD_DIM
GLA_HEADS = 6
GLA_DK = 64
GLA_DV = 128
GLA_GATE_RANK = 16
GLA_TAU = 16.0
GLA_CHUNK = 64
GLA_GROUP = 8
FFN_ROWS = 512
FFN_TILE = 256
FFN_CAST_STEPS = 32
PROJ_TILE = 512
DFT_ROWS = 512
ATTN_Q_BLOCK = 256
CONV_WIDTH = 4 * HEAD_DIM
DIFF_HEADS = 6
DIFF_DV = 2 * HEAD_DIM
ROPE_THETA = 10000.0
ROPE_AXIS_DIM = HEAD_DIM // 2

MOD_ROWS = 16
MOD_TILES = 4
VMEM_LIMIT = 48 * 1024 * 1024

F32 = jnp.float32
BF16 = jnp.bfloat16


def _params(*sem):
    return pltpu.CompilerParams(dimension_semantics=sem, vmem_limit_bytes=VMEM_LIMIT)


def _sigmoid(x):
    return 1.0 / (1.0 + jnp.exp(-x))


def _dot(a, b):
    return jnp.dot(a, b, preferred_element_type=F32)


def _dot_nt(a, b):
    return lax.dot_general(a, b, (((1,), (1,)), ((), ())), preferred_element_type=F32)


def _dot_tn(a, b):
    return lax.dot_general(a, b, (((0,), (0,)), ((), ())), preferred_element_type=F32)


def _rms(x):
    return x * lax.rsqrt(jnp.mean(x * x, axis=-1, keepdims=True) + NORM_EPS)


def _mod_kernel(*refs, n_cast):
    cond_ref, w_ref, b_ref = refs[:3]
    o_ref = refs[3 + n_cast]
    _cast_blocks(refs[3:3 + n_cast], refs[4 + n_cast:])
    c = cond_ref[...]
    s = c * _sigmoid(c)
    w = w_ref[...]
    w_hi = w.astype(BF16)
    w_lo = (w - w_hi.astype(F32)).astype(BF16)
    s_hi = s.astype(BF16)
    s_lo = (s - s_hi.astype(F32)).astype(BF16)
    o_ref[...] = _dot(s_hi, w_hi) + _dot(s_lo, w_hi) + _dot(s_hi, w_lo) + b_ref[...]


def _modulation(cond, ada_w, ada_b, cast=()):
    n = N_MOD * D_MODEL
    tn = n // MOD_TILES
    c_args, c_in, c_shape, c_out = _cast_riders(cast, lambda l, j: l * MOD_TILES + j, DEPTH * MOD_TILES)
    outs = pl.pallas_call(
        functools.partial(_mod_kernel, n_cast=len(cast)),
        out_shape=[jax.ShapeDtypeStruct((DEPTH, MOD_ROWS, n), F32)] + c_shape,
        grid=(DEPTH, MOD_TILES),
        in_specs=[
            pl.BlockSpec((MOD_ROWS, D_MODEL), lambda l, j: (0, 0)),
            pl.BlockSpec((None, D_MODEL, tn), lambda l, j: (l, 0, j)),
            pl.BlockSpec((None, 1, tn), lambda l, j: (l, 0, j)),
        ] + c_in,
        out_specs=[pl.BlockSpec((None, MOD_ROWS, tn), lambda l, j: (l, 0, j))] + c_out,
        compiler_params=_params("arbitrary", "arbitrary"),
        name="modulation",
    )(cond, ada_w, ada_b.reshape(DEPTH, 1, n), *c_args)
    return (outs[0].reshape(DEPTH, MOD_ROWS, N_MOD, 1, D_MODEL), *outs[1:])


def _mod_spec(layer, k, row_of_block):
    return pl.BlockSpec((None, None, None, 1, D_MODEL),
                        lambda i, *_: (layer, row_of_block(i), k, 0, 0))


def _gain_spec(layer):
    return pl.BlockSpec((None, 1, D_MODEL), lambda i, *_: (layer, 0, 0))


def _resident(block_shape, index_map):
    return pl.BlockSpec(block_shape, index_map, pipeline_mode=pl.Buffered(1))


def _cast_riders(sources, step_of, n_steps):
    args, in_specs, out_shape, out_specs = [], [], [], []
    for arr, lead, nb in sources:
        _, r, c = arr.shape
        rows, stride = r // nb, n_steps // nb
        assert rows * nb == r and stride * nb == n_steps and rows % 16 == 0
        idx = lambda *g, nb=nb, stride=stride: jnp.minimum(step_of(*g) // stride, nb - 1)
        args.append(arr)
        in_specs.append(pl.BlockSpec((None, rows, c), lambda *g, idx=idx, lead=lead: (lead, idx(*g), 0)))
        out_shape.append(jax.ShapeDtypeStruct((r, c), BF16))
        out_specs.append(pl.BlockSpec((rows, c), lambda *g, idx=idx: (idx(*g), 0)))
    return args, in_specs, out_shape, out_specs


def _cast_blocks(src_refs, dst_refs):
    for src, dst in zip(src_refs, dst_refs):
        dst[...] = src[...].astype(BF16)


def _ffn_kernel(*refs, with_mix, with_proj, with_gate, with_final, paired, n_first, n_cast, proj_nt):
    it = iter(refs)
    pairs = iter(paired)

    def rows_in():
        a = next(it)
        if not next(pairs):
            return a[...]
        b = next(it)
        return jnp.where(pl.program_id(0) < n_first, a[...], b[...])

    x = rows_in()
    if with_mix:
        ma, mb = rows_in(), rows_in()
        wm_ref, gm_ref = next(it), next(it)
    sh_ref, sc_ref, g_ref, gain_ref, wi_ref, wo_ref = (next(it) for _ in range(6))
    if with_proj:
        psh_ref, psc_ref, pgain_ref, wp_ref = (next(it) for _ in range(4))
    if with_final:
        fn_ref = next(it)
    cast_src = [next(it) for _ in range(n_cast)]
    o_ref = next(it)
    if with_proj:
        z_ref = next(it)
    if with_gate:
        zg_ref = next(it)
    cast_dst = [next(it) for _ in range(n_cast)]
    a_scr = next(it)
    _cast_blocks(cast_src, cast_dst)

    if with_mix:
        ka = ma.shape[1]
        x = x + gm_ref[...] * (_dot(ma, wm_ref[0:ka, :]) + _dot(mb, wm_ref[ka:, :]))
    xn = (_rms(x) * gain_ref[...] * (1.0 + sc_ref[...]) + sh_ref[...]).astype(BF16)
    for j in range(FFN_HIDDEN // FFN_TILE):
        lo = j * FFN_TILE
        g = _dot(xn, wi_ref[:, lo:lo + FFN_TILE])
        u = _dot(xn, wi_ref[:, FFN_HIDDEN + lo:FFN_HIDDEN + lo + FFN_TILE])
        a_scr[:, lo:lo + FFN_TILE] = (g * _sigmoid(g) * u).astype(BF16)
    out = x + (0.5 * g_ref[...]) * _dot(a_scr[...], wo_ref[...])
    if with_final:
        out = _rms(out) * fn_ref[...]
    o_ref[...] = out
    if with_proj:
        xm = (_rms(out) * pgain_ref[...] * (1.0 + psc_ref[...]) + psh_ref[...]).astype(BF16)
        n_out = z_ref.shape[1]
        project = (lambda lo, hi: _dot_nt(xm, wp_ref[lo:hi, :])) if proj_nt else (lambda lo, hi: _dot(xm, wp_ref[:, lo:hi]))
        for lo in range(0, n_out, PROJ_TILE):
            z_ref[:, lo:lo + PROJ_TILE] = project(lo, lo + PROJ_TILE).astype(z_ref.dtype)
        if with_gate:
            zg_ref[...] = _dot_nt(wp_ref[n_out:, :], xm)


def _ffn(h, mods, layer, mod_base, row_of_block, gain, w_in, w_out, *, tm, n_first=None, n_rows=None, mix=None,
         proj=None, final_gain=None, cast=()):
    m = n_rows or (sum(a.shape[0] for a in h) if isinstance(h, tuple) else h.shape[0])
    row = lambda i: (i, 0)
    args, specs, paired = [], [], []

    def add_rows(a):
        paired.append(isinstance(a, tuple))
        if paired[-1]:
            first, second = a
            assert first.shape[0] == n_first * tm
            args.extend([first, second])
            specs.extend([pl.BlockSpec((tm, first.shape[1]), lambda i: (jnp.minimum(i, n_first - 1), 0)),
                          pl.BlockSpec((tm, second.shape[1]), lambda i: (jnp.maximum(i - n_first, 0), 0))])
        else:
            args.append(a)
            specs.append(pl.BlockSpec((tm, a.shape[1]), row))

    add_rows(h)
    if mix is not None:
        ma, mb, w_mix = mix
        add_rows(ma)
        add_rows(mb)
        args += [w_mix, mods]
        specs += [_resident(w_mix.shape, lambda i: (0, 0)), _mod_spec(layer, 5, row_of_block)]
    args += [mods, mods, mods, gain, w_in, w_out]
    specs += [
        _mod_spec(layer, mod_base, row_of_block),
        _mod_spec(layer, mod_base + 1, row_of_block),
        _mod_spec(layer, mod_base + 2, row_of_block),
        _gain_spec(layer),
        _resident((D_MODEL, 2 * FFN_HIDDEN), lambda i: (0, 0)),
        _resident((FFN_HIDDEN, D_MODEL), lambda i: (0, 0)),
    ]
    out_shape = [jax.ShapeDtypeStruct((m, D_MODEL), F32)]
    out_specs = [pl.BlockSpec((tm, D_MODEL), row)]
    n_gate, proj_nt = 0, False
    if proj is not None:
        pgain, wp, n_gate, proj_nt = proj
        n_out = wp.shape[0 if proj_nt else 1] - n_gate
        assert n_out % PROJ_TILE == 0
        args += [mods, mods, pgain, wp]
        specs += [_mod_spec(layer, 3, row_of_block), _mod_spec(layer, 4, row_of_block), _gain_spec(layer),
                  _resident(wp.shape, lambda i: (0, 0))]
        out_shape.append(jax.ShapeDtypeStruct((m, n_out), BF16))
        out_specs.append(pl.BlockSpec((tm, n_out), row))
        if n_gate:
            assert proj_nt
            out_shape.append(jax.ShapeDtypeStruct((n_gate, m), F32))
            out_specs.append(pl.BlockSpec((n_gate, tm), lambda i: (0, i)))
    if final_gain is not None:
        args.append(final_gain)
        specs.append(pl.BlockSpec((1, D_MODEL), lambda i: (0, 0)))
    c_args, c_in, c_shape, c_out = _cast_riders(cast, lambda i: i, FFN_CAST_STEPS)
    assert not cast or m // tm >= FFN_CAST_STEPS
    args += c_args
    specs += c_in
    out_shape += c_shape
    out_specs += c_out
    outs = pl.pallas_call(
        functools.partial(_ffn_kernel, with_mix=mix is not None, with_proj=proj is not None, with_gate=n_gate > 0,
                          with_final=final_gain is not None, paired=tuple(paired), n_first=n_first,
                          n_cast=len(cast), proj_nt=proj_nt),
        out_shape=out_shape,
        grid=(m // tm,),
        in_specs=specs,
        out_specs=out_specs,
        scratch_shapes=[pltpu.VMEM((tm, FFN_HIDDEN), BF16)],
        compiler_params=_params("arbitrary" if cast else "parallel"),
        name="ffn",
    )(*args)
    return outs if len(outs) > 1 else outs[0]


def _dft_tables(t, tq):
    k = (np.arange(tq, dtype=np.int64)[:, None] * np.arange(t, dtype=np.int64)[None, :]) % t
    ang = 2.0 * np.pi * k.astype(np.float64) / t
    pos = np.concatenate([np.cos(ang), -np.sin(ang)], axis=1).astype(np.float32)
    kc = (np.arange(HEAD_DIM)[:, None] * np.arange(HEAD_DIM)[None, :]) % HEAD_DIM
    angc = 2.0 * np.pi * kc.astype(np.float64) / HEAD_DIM
    eye = np.eye(FOURIER_GROUPS)
    chan = np.concatenate([np.kron(eye, np.cos(angc)), np.kron(eye, np.sin(angc))], axis=1).astype(np.float32)
    return pos, chan


def _dft_kernel(p_ref, z_ref, c_ref, o_ref, p_scr, ab_scr, *, scale, quarter_turns):
    i, b = pl.program_id(0), pl.program_id(1)
    t = p_ref.shape[1] // 2

    @pl.when(b == 0)
    def _():
        col = lax.broadcasted_iota(jnp.int32, (1, t), 1)
        k = (i * quarter_turns * col) & 3
        ca = jnp.where(k == 0, 1.0, jnp.where(k == 2, -1.0, 0.0))
        sa = jnp.where(k == 1, 1.0, jnp.where(k == 3, -1.0, 0.0))
        c0, n0 = p_ref[:, 0:t], p_ref[:, t:]
        p_scr[:, 0:t] = (ca * c0 + sa * n0).astype(BF16)
        p_scr[:, t:] = (ca * n0 - sa * c0).astype(BF16)

    @pl.when(i == 0)
    def _():
        ab = _dot(z_ref[...], c_ref[...].astype(BF16))
        ab_scr[b, 0:t, :] = ab[:, :FOURIER_WIDTH].astype(BF16)
        ab_scr[b, t:, :] = ab[:, FOURIER_WIDTH:].astype(BF16)

    o_ref[...] = (_dot(p_scr[...], ab_scr[b]) * scale).astype(BF16)


def _fourier(z, t, bn, row0=0):
    tq = min(t, DFT_ROWS)
    assert (4 * tq) % t == 0
    pos, chan = _dft_tables(t, tq)
    off = row0 // t
    return pl.pallas_call(
        functools.partial(_dft_kernel, scale=1.0 / math.sqrt(t * HEAD_DIM), quarter_turns=4 * tq // t),
        out_shape=jax.ShapeDtypeStruct((bn * t, FOURIER_WIDTH), BF16),
        grid=(t // tq, bn),
        in_specs=[_resident((tq, 2 * t), lambda i, b: (0, 0)),
                  pl.BlockSpec((t, FOURIER_WIDTH), lambda i, b: (off + b, 0)),
                  pl.BlockSpec((FOURIER_WIDTH, 2 * FOURIER_WIDTH), lambda i, b: (0, 0))],
        out_specs=pl.BlockSpec((tq, FOURIER_WIDTH), lambda i, b: (b * (t // tq) + i, 0)),
        scratch_shapes=[pltpu.VMEM((tq, 2 * t), BF16), pltpu.VMEM((bn, 2 * t, FOURIER_WIDTH), BF16)],
        compiler_params=_params("arbitrary", "arbitrary"),
        name="dft",
    )(jnp.asarray(pos), z, jnp.asarray(chan))


def _gla_segment(q_ref, k_ref, v_ref, r_ref, g_ref, y_ref, gw_ref, gb_ref, gn_ref,
                 b_scr, o_scr, s_scr, n_rows):
    c_len = GLA_CHUNK
    n = n_rows // c_len
    pair_k = 2 * GLA_DK
    pair_v = 2 * GLA_DV

    for d in range(2):
        zg = g_ref[d * GLA_GATE_RANK:(d + 1) * GLA_GATE_RANK, :]
        b_scr[d, 0:n_rows, :] = _dot_tn(zg.astype(BF16), gw_ref[d].astype(BF16)) + gb_ref[d]

    lane = lax.broadcasted_iota(jnp.int32, (c_len, pair_k), 1)
    col = lax.broadcasted_iota(jnp.int32, (c_len, pair_v), 1)
    ti = lax.broadcasted_iota(jnp.int32, (c_len, c_len), 0)
    tj = lax.broadcasted_iota(jnp.int32, (c_len, c_len), 1)

    group = math.gcd(GLA_GROUP, n)
    n_it = n // group
    keep2 = [jnp.concatenate([m, m], axis=0) for m in (tj <= ti, tj >= ti)]
    ti2 = lax.broadcasted_iota(jnp.int32, (c_len, 2 * c_len), 0)
    tj2 = lax.broadcasted_iota(jnp.int32, (c_len, 2 * c_len), 1) & (c_len - 1)
    tri2 = [jnp.where(m, 1.0, 0.0).astype(BF16) for m in (tj2 <= ti2, tj2 >= ti2)]
    gn = gn_ref[...]

    def finish(rows, o):
        for hh in range(2):
            sl = slice(hh * GLA_DV, (hh + 1) * GLA_DV)
            rr = r_ref[rows, sl].astype(F32)
            y_ref[rows, sl] = (_rms(o[:, sl]) * gn * (rr * _sigmoid(rr))).astype(y_ref.dtype)

    def chunk_rows(c):
        start = c * c_len
        return pl.ds(start if isinstance(start, int) else pl.multiple_of(start, c_len), c_len)

    def body(i, carry, phase):
        units = []
        for d in range(2):
            for g in range(group):
                c = i * group + g if d == 0 else n - 1 - (i * group + g)
                rows = chunk_rows(c)
                logit = b_scr[d, rows, :]
                lg = (jnp.minimum(logit, 0.0) - jnp.log(1.0 + jnp.exp(-jnp.abs(logit)))) * (1.0 / GLA_TAU)
                hi = lg.astype(BF16)
                rest = (lg - hi.astype(F32)).astype(BF16)
                units.append(dict(d=d, rows=rows, b=_dot(tri2[d], jnp.concatenate([hi, rest], axis=0))))
        for u in units:
            d, rows, b = u["d"], u["rows"], u["b"]
            bl = b[c_len - 1:c_len, :] if d == 0 else b[0:1, :]
            k = k_ref[rows, :].astype(F32)
            v16 = v_ref[rows, :].astype(BF16)
            qd = q_ref[rows, :].astype(F32) * (GLA_DK ** -0.5) * jnp.exp(b)
            kd = (k * jnp.exp(-b)).astype(BF16)
            kdec = (k * jnp.exp(bl - b)).astype(BF16)
            q2 = jnp.concatenate([jnp.where((lane // GLA_DK) == hh, qd, 0.0) for hh in range(2)],
                                 axis=0).astype(BF16)
            att = _dot_nt(q2, kd)
            upd = _dot_tn(v16, kdec)
            u.update(q2=q2, v16=v16, att=att, upd=upd, dec=jnp.exp(bl))
        for d in range(2):
            s = s_scr[d]
            for u in units:
                if u["d"] == d:
                    u["o"] = _dot_nt(u["q2"], s.astype(BF16))
                    s = s * u["dec"] + u["upd"]
            s_scr[d] = s
        for u in units:
            att = jnp.where(keep2[u["d"]], u["att"], 0.0).astype(BF16)
            pv = _dot(att, u["v16"])
            both = u["o"] + pv
            u["o"] = jnp.where((col // GLA_DV) == 0, both[0:c_len, :], both[c_len:, :])
            if phase == "first":
                o_scr[u["d"], u["rows"], :] = u["o"]
            elif phase == "second":
                finish(u["rows"], u["o"] + o_scr[1 - u["d"], u["rows"], :])
        if phase == "only":
            for g in range(group):
                finish(units[g]["rows"], units[g]["o"] + units[group + n - 1 - g]["o"])
        return carry

    if n_it == 1:
        body(0, 0, "only")
    else:
        assert n_it % 2 == 0
        lax.fori_loop(0, n_it // 2, functools.partial(body, phase="first"), 0)
        lax.fori_loop(n_it // 2, n_it, functools.partial(body, phase="second"), 0)


def _gla_kernel(*refs, t_ctx, t_lat, n_cast):
    qc_ref, kc_ref, vc_ref, rc_ref, gc_ref, ql_ref, kl_ref, vl_ref, rl_ref, gl_ref, gw_ref, gb_ref, gn_ref = refs[:13]
    yc_ref, yl_ref = refs[13 + n_cast:15 + n_cast]
    b_scr, o_scr, s_scr = refs[-3:]

    @pl.when(pl.program_id(1) == 0)
    def _():
        _cast_blocks(refs[13:13 + n_cast], refs[15 + n_cast:15 + 2 * n_cast])

    s_scr[...] = jnp.zeros_like(s_scr)
    _gla_segment(qc_ref, kc_ref, vc_ref, rc_ref, gc_ref, yc_ref, gw_ref, gb_ref, gn_ref,
                 b_scr, o_scr, s_scr, t_ctx)
    _gla_segment(ql_ref, kl_ref, vl_ref, rl_ref, gl_ref, yl_ref, gw_ref, gb_ref, gn_ref,
                 b_scr, o_scr, s_scr, t_lat)


def _gla(z, zg, gate_w, gate_b, gla_g, bn, t_ctx, t_lat, ctx_row0, cast=()):
    pk, pv = 2 * GLA_DK, 2 * GLA_DV
    q0 = FOURIER_WIDTH // pk
    k0 = (FOURIER_WIDTH + GLA_HEADS * GLA_DK) // pk
    v0 = (FOURIER_WIDTH + 2 * GLA_HEADS * GLA_DK) // pv
    r0 = (FOURIER_WIDTH + 2 * GLA_HEADS * GLA_DK + GLA_HEADS * GLA_DV) // pv

    def seg_specs(t, row0):
        off = row0 // t
        return [pl.BlockSpec((t, pk), lambda b, p: (off + b, q0 + p)),
                pl.BlockSpec((t, pk), lambda b, p: (off + b, k0 + p)),
                pl.BlockSpec((t, pv), lambda b, p: (off + b, v0 + p)),
                pl.BlockSpec((t, pv), lambda b, p: (off + b, r0 + p)),
                pl.BlockSpec((2 * GLA_GATE_RANK, t), lambda b, p: (0, off + b))]

    wdt = GLA_HEADS * GLA_DV
    c_args, c_in, c_shape, c_out = _cast_riders(cast, lambda b, p: b, bn)
    return pl.pallas_call(
        functools.partial(_gla_kernel, t_ctx=t_ctx, t_lat=t_lat, n_cast=len(cast)),
        out_shape=[jax.ShapeDtypeStruct((bn * t_ctx, wdt), BF16), jax.ShapeDtypeStruct((bn * t_lat, wdt), BF16)] + c_shape,
        grid=(bn, GLA_HEADS // 2),
        in_specs=seg_specs(t_ctx, ctx_row0) + seg_specs(t_lat, 0) + [
            pl.BlockSpec((2, GLA_GATE_RANK, pk), lambda b, p: (0, 0, p)),
            pl.BlockSpec((2, 1, pk), lambda b, p: (0, 0, p)),
            pl.BlockSpec((1, GLA_DV), lambda b, p: (0, 0))] + c_in,
        out_specs=[pl.BlockSpec((t_ctx, pv), lambda b, p: (b, p)),
                   pl.BlockSpec((t_lat, pv), lambda b, p: (b, p))] + c_out,
        scratch_shapes=[pltpu.VMEM((2, t_lat, pk), F32), pltpu.VMEM((2, t_lat, pv), F32),
                        pltpu.VMEM((2, pv, pk), F32)],
        compiler_params=_params("arbitrary", "arbitrary"),
        name="gla",
    )(z, z, z, z, zg, z, z, z, z, zg, gate_w, gate_b.reshape(2, 1, GLA_HEADS * GLA_DK),
      gla_g.reshape(1, GLA_DV), *c_args)


def _short_conv_block(zb_ref, zc_ref, zx_ref, w_ref, b_ref, o_ref):
    u = zc_ref[...].astype(F32) * zx_ref[...].astype(F32)
    t = u.shape[0]
    row = lax.broadcasted_iota(jnp.int32, u.shape, 0)
    prev = jnp.where(row >= 1, pltpu.roll(u, 1, 0), 0.0)
    nxt = jnp.where(row < t - 1, pltpu.roll(u, t - 1, 0), 0.0)
    y = prev * w_ref[0:1, :] + u * w_ref[1:2, :] + nxt * w_ref[2:3, :] + b_ref[...]
    o_ref[...] = (zb_ref[...].astype(F32) * y).astype(o_ref.dtype)


def _rope_tables(t):
    rows = t // GRID_W
    row = jnp.repeat(jnp.arange(rows), GRID_W).astype(F32)
    col = jnp.tile(jnp.arange(GRID_W), rows).astype(F32)
    n = ROPE_AXIS_DIM // 2
    inv = ROPE_THETA ** (-jnp.arange(n, dtype=F32) / n)
    ar, ac = row[:, None] * inv, col[:, None] * inv
    ang = jnp.concatenate([ar, ar, ac, ac], axis=-1)
    sign = jnp.tile(jnp.concatenate([-jnp.ones((n,), F32), jnp.ones((n,), F32)]), 2)
    cos = jnp.cos(ang)
    sin = jnp.sin(ang) * sign
    return jnp.tile(cos, (1, 2)), jnp.tile(sin, (1, 2))


def _rope(x, cos, sin):
    lane = lax.broadcasted_iota(jnp.int32, x.shape, 1)
    n = ROPE_AXIS_DIM // 2
    w = x.shape[1]
    partner = jnp.where((lane & (2 * n - 1)) < n, pltpu.roll(x, w - n, 1), pltpu.roll(x, n, 1))
    return x * cos + partner * sin


ATTN_ONES_ROWS = 16
ATTN_HEADS_PER_STEP = 3


def _attn_kernel(*refs, t_lat, lam_init, tq, n_cast):
    lam_ref, q_ref, kl_ref, vl_ref, kc_ref, vc_ref, cos_ref, sin_ref, dn_ref = refs[:9]
    conv_in = refs[9:14]
    n_in = 14 + n_cast
    o_ref, conv_ref = refs[n_in:n_in + 2]
    k_scr, vt_scr, s0_scr, m0_scr, s1_scr, m1_scr = refs[-6:]

    _cast_blocks(refs[14:n_in], refs[n_in + 2:n_in + 2 + n_cast])

    @pl.when(pl.program_id(1) == 0)
    def _():
        _short_conv_block(*conv_in, conv_ref)

    hw = 2 * HEAD_DIM
    t_all = k_scr.shape[1]

    for hd in range(ATTN_HEADS_PER_STEP):
        cols = slice(hd * hw, (hd + 1) * hw)
        k_scr[hd, 0:t_lat, :] = _rope(kl_ref[:, cols].astype(F32), cos_ref[...], sin_ref[...]).astype(BF16)
        k_scr[hd, t_lat:, :] = kc_ref[:, cols].astype(BF16)
        vt_scr[hd, 0:hw, 0:t_lat] = vl_ref[:, cols].astype(F32).T.astype(BF16)
        vt_scr[hd, 0:hw, t_lat:] = vc_ref[:, cols].astype(F32).T.astype(BF16)
        vt_scr[hd, hw:, :] = jnp.ones((ATTN_ONES_ROWS, t_all), BF16)

    lv = lam_ref[...]
    lam = (jnp.exp(jnp.sum(lv[0:1] * lv[1:2], axis=-1, keepdims=True))
           - jnp.exp(jnp.sum(lv[2:3] * lv[3:4], axis=-1, keepdims=True)) + lam_init)
    lane = lax.broadcasted_iota(jnp.int32, (tq, hw), 1)

    n_blocks = t_lat // tq
    slots = ((s0_scr, m0_scr), (s1_scr, m1_scr))

    def scores(hd, i, slot):
        rows = pl.ds(pl.multiple_of(i * tq, tq), tq)
        q = (_rope(q_ref[rows, hd * hw:(hd + 1) * hw].astype(F32), cos_ref[rows, :], sin_ref[rows, :])
             * (HEAD_DIM ** -0.5 * math.log2(math.e)))
        s_scr, m_scr = slots[slot]
        for half in range(2):
            qb = jnp.where((lane // HEAD_DIM) == half, q, 0.0).astype(BF16)
            s = _dot_nt(k_scr[hd], qb)
            s_scr[half] = s
            m_scr[half] = jnp.max(s, axis=0, keepdims=True)

    def outputs(hd, i, slot):
        rows = pl.ds(pl.multiple_of(i * tq, tq), tq)
        s_scr, m_scr = slots[slot]
        outs = []
        for half in range(2):
            p = jnp.exp2(s_scr[half] - m_scr[half]).astype(BF16)
            acc = _dot(vt_scr[hd], p)
            outs.append(acc[0:hw, :] / acc[hw:hw + 1, :])
        o = (outs[0] - lam * outs[1]).T
        o_ref[rows, hd * hw:(hd + 1) * hw] = (_rms(o) * dn_ref[...] * (1.0 - lam_init)).astype(o_ref.dtype)

    assert n_blocks % 2 == 0
    scores(0, 0, 0)
    for hd in range(ATTN_HEADS_PER_STEP):
        def body(j, carry, hd=hd):
            scores(hd, 2 * j + 1, 1)
            outputs(hd, 2 * j, 0)
            scores(hd, 2 * j + 2, 0)
            outputs(hd, 2 * j + 1, 1)
            return carry

        lax.fori_loop(0, n_blocks // 2 - 1, body, 0)
        scores(hd, n_blocks - 1, 1)
        outputs(hd, n_blocks - 2, 0)
        if hd + 1 < ATTN_HEADS_PER_STEP:
            scores(hd + 1, 0, 0)
        outputs(hd, n_blocks - 1, 1)


def _odd_mixers(z, lam_vecs, dnorm, lam_init, conv_w, conv_b, bn, t_lat, t_ctx, ctx_row0, cast=()):
    hw = 2 * HEAD_DIM
    wb = ATTN_HEADS_PER_STEP * hw
    q0 = 3 * CONV_WIDTH // wb
    k0 = q0 + DIFF_HEADS // ATTN_HEADS_PER_STEP
    v0 = k0 + DIFF_HEADS // ATTN_HEADS_PER_STEP
    tq = ATTN_Q_BLOCK
    t_all = t_lat + t_ctx
    coff = ctx_row0 // t_ctx
    cos, sin = _rope_tables(t_lat)
    groups = DIFF_HEADS // ATTN_HEADS_PER_STEP
    c_args, c_in, c_shape, c_out = _cast_riders(cast, lambda b, p: b * groups + p, bn * groups)
    outs = pl.pallas_call(
        functools.partial(_attn_kernel, t_lat=t_lat, lam_init=lam_init, tq=tq, n_cast=len(cast)),
        out_shape=[jax.ShapeDtypeStruct((bn * t_lat, DIFF_HEADS * DIFF_DV), BF16),
                   jax.ShapeDtypeStruct((bn * t_lat, CONV_WIDTH), BF16)] + c_shape,
        grid=(bn, groups),
        in_specs=[
            pl.BlockSpec((4, HEAD_DIM), lambda b, p: (0, 0)),
            pl.BlockSpec((t_lat, wb), lambda b, p: (b, q0 + p)),
            pl.BlockSpec((t_lat, wb), lambda b, p: (b, k0 + p)),
            pl.BlockSpec((t_lat, wb), lambda b, p: (b, v0 + p)),
            pl.BlockSpec((t_ctx, wb), lambda b, p: (coff + b, k0 + p)),
            pl.BlockSpec((t_ctx, wb), lambda b, p: (coff + b, v0 + p)),
            pl.BlockSpec((t_lat, hw), lambda b, p: (0, 0)),
            pl.BlockSpec((t_lat, hw), lambda b, p: (0, 0)),
            pl.BlockSpec((1, DIFF_DV), lambda b, p: (0, 0)),
            pl.BlockSpec((t_lat, CONV_WIDTH), lambda b, p: (b, 0)),
            pl.BlockSpec((t_lat, CONV_WIDTH), lambda b, p: (b, 1)),
            pl.BlockSpec((t_lat, CONV_WIDTH), lambda b, p: (b, 2)),
            pl.BlockSpec((3, CONV_WIDTH), lambda b, p: (0, 0)),
            pl.BlockSpec((1, CONV_WIDTH), lambda b, p: (0, 0)),
        ] + c_in,
        out_specs=[pl.BlockSpec((t_lat, wb), lambda b, p: (b, p)),
                   pl.BlockSpec((t_lat, CONV_WIDTH), lambda b, p: (b, 0))] + c_out,
        scratch_shapes=[pltpu.VMEM((ATTN_HEADS_PER_STEP, t_all, hw), BF16),
                        pltpu.VMEM((ATTN_HEADS_PER_STEP, hw + ATTN_ONES_ROWS, t_all), BF16),
                        pltpu.VMEM((2, t_all, tq), F32), pltpu.VMEM((2, 1, tq), F32),
                        pltpu.VMEM((2, t_all, tq), F32), pltpu.VMEM((2, 1, tq), F32)],
        compiler_params=_params("arbitrary", "arbitrary"),
        name="diff_attn",
    )(lam_vecs, z, z, z, z, z, cos, sin, dnorm.reshape(1, DIFF_DV), z, z, z, conv_w,
      conv_b.reshape(1, CONV_WIDTH), *c_args)
    return outs


def kernel(x, c, ctx, c_ctx, ada_w, ada_b, norm_ffn1, norm_mix, norm_ffn2, ffn1_w_in, ffn1_w_out, ffn2_w_in,
           ffn2_w_out, mix_w_out, even_w_in, gla_gate_w, gla_gate_b, gla_norm, odd_w_in, conv_w, conv_b,
           lambda_q1, lambda_k1, lambda_q2, lambda_k2, diff_norm, final_norm):
    assert DEPTH == 2
    bn, t_lat, d = x.shape
    t_ctx = ctx.shape[1]
    assert bn < MOD_ROWS
    ctx_row = bn

    cond = jnp.concatenate([c, c_ctx[None, :], jnp.zeros((MOD_ROWS - bn - 1, d), F32)], axis=0)
    n_mod_steps = DEPTH * MOD_TILES
    mods, w1i0, w1o0 = _modulation(cond, ada_w, ada_b,
                                   cast=((ffn1_w_in, 0, n_mod_steps), (ffn1_w_out, 0, n_mod_steps)))
    w_even_t = jnp.swapaxes(even_w_in[0], 0, 1).astype(BF16)

    tm = FFN_ROWS
    n_lat = bn * t_lat
    lat_blocks = n_lat // tm
    lat_row = lambda i: i // (t_lat // tm)
    all_row = lambda i: jnp.where(i < lat_blocks, i // (t_lat // tm), ctx_row)
    g1 = norm_ffn1.reshape(DEPTH, 1, d)
    gm = norm_mix.reshape(DEPTH, 1, d)
    g2 = norm_ffn2.reshape(DEPTH, 1, d)
    cast_gla = ((ffn2_w_in, 0, bn), (ffn2_w_out, 0, bn), (mix_w_out, 0, bn))
    cast_ffn = ((ffn1_w_in, 1, FFN_CAST_STEPS), (ffn1_w_out, 1, FFN_CAST_STEPS // 2), (odd_w_in, 0, FFN_CAST_STEPS))
    n_attn = bn * (DIFF_HEADS // ATTN_HEADS_PER_STEP)
    cast_attn = ((ffn2_w_in, 1, n_attn), (ffn2_w_out, 1, n_attn), (mix_w_out, 1, n_attn))

    proj0 = (gm, w_even_t, 2 * GLA_GATE_RANK, True)
    h, z, zg = _ffn((x.reshape(n_lat, d), ctx.reshape(bn * t_ctx, d)), mods, 0, 0, all_row, g1, w1i0, w1o0,
                    tm=tm, n_first=lat_blocks, proj=proj0)
    yf_l = _fourier(z, t_lat, bn)
    yf_c = _fourier(z, t_ctx, bn, row0=n_lat)
    yg_c, yg_l, w2i0, w2o0, wmix0 = _gla(z, zg, gla_gate_w[0], gla_gate_b[0], gla_norm[0], bn, t_ctx, t_lat, n_lat,
                                         cast=cast_gla)
    h, w1i1, w1o1, w_odd = _ffn(h, mods, 0, 6, all_row, g2, w2i0, w2o0, tm=tm, n_first=lat_blocks,
                                mix=((yf_l, yf_c), (yg_l, yg_c), wmix0), cast=cast_ffn)

    h, z = _ffn(h, mods, 1, 0, all_row, g1, w1i1, w1o1, tm=tm, proj=(gm, w_odd, 0, False))
    lam_init = 0.8 - 0.6 * math.exp(-0.3 * 1)
    lam_vecs = jnp.stack([lambda_q1[0], lambda_k1[0], lambda_q2[0], lambda_k2[0]]).astype(F32)
    y_att, y_conv, w2i1, w2o1, wmix1 = _odd_mixers(z, lam_vecs, diff_norm[0], lam_init, conv_w[0], conv_b[0], bn,
                                                   t_lat, t_ctx, n_lat, cast=cast_attn)
    h = _ffn(h, mods, 1, 6, lat_row, g2, w2i1, w2o1, tm=tm, mix=(y_conv, y_att, wmix1),
             final_gain=final_norm.reshape(1, d), n_rows=n_lat)
    return h.reshape(bn, t_lat, d)
```

```python
import functools
import math

import numpy as np
import jax
import jax.numpy as jnp
from jax import lax
from jax.experimental import pallas as pl
from jax.experimental.pallas import tpu as pltpu

D_MODEL = 1024
DEPTH = 2
GRID_W = 64
HEAD_DIM = 64
N_MOD = 9
FFN_HIDDEN = 2816
NORM_EPS = 1e-6
FOURIER_GROUPS = 4
FOURIER_WIDTH = FOURIER_GROUPS * HEAD_DIM
GLA_HEADS = 6
GLA_DK = 64
GLA_DV = 128
GLA_GATE_RANK = 16
GLA_TAU = 16.0
GLA_CHUNK = 64
GLA_GROUP = 16
FFN_ROWS = 512
FFN_TILE = 256
FFN_CAST_STEPS = 32
PROJ_TILE = 512
DFT_ROWS = 512
ATTN_Q_BLOCK = 256
CONV_WIDTH = 4 * HEAD_DIM
DIFF_HEADS = 6
DIFF_DV = 2 * HEAD_DIM
ROPE_THETA = 10000.0
ROPE_AXIS_DIM = HEAD_DIM // 2

MOD_ROWS = 16
MOD_TILES = 4
VMEM_LIMIT = 48 * 1024 * 1024

F32 = jnp.float32
BF16 = jnp.bfloat16


def _params(*sem):
    return pltpu.CompilerParams(dimension_semantics=sem, vmem_limit_bytes=VMEM_LIMIT)


def _sigmoid(x):
    return 1.0 / (1.0 + jnp.exp(-x))


def _dot(a, b):
    return jnp.dot(a, b, preferred_element_type=F32)


def _dot_nt(a, b):
    return lax.dot_general(a, b, (((1,), (1,)), ((), ())), preferred_element_type=F32)


def _dot_tn(a, b):
    return lax.dot_general(a, b, (((0,), (0,)), ((), ())), preferred_element_type=F32)


def _rms(x):
    return x * lax.rsqrt(jnp.mean(x * x, axis=-1, keepdims=True) + NORM_EPS)


def _mod_kernel(*refs, n_cast):
    cond_ref, w_ref, b_ref = refs[:3]
    o_ref = refs[3 + n_cast]
    _cast_blocks(refs[3:3 + n_cast], refs[4 + n_cast:])
    c = cond_ref[...]
    s = c * _sigmoid(c)
    w = w_ref[...]
    w_hi = w.astype(BF16)
    w_lo = (w - w_hi.astype(F32)).astype(BF16)
    s_hi = s.astype(BF16)
    s_lo = (s - s_hi.astype(F32)).astype(BF16)
    o_ref[...] = _dot(s_hi, w_hi) + _dot(s_lo, w_hi) + _dot(s_hi, w_lo) + b_ref[...]


def _modulation(cond, ada_w, ada_b, cast=()):
    n = N_MOD * D_MODEL
    tn = n // MOD_TILES
    c_args, c_in, c_shape, c_out = _cast_riders(cast, lambda l, j: l * MOD_TILES + j, DEPTH * MOD_TILES)
    outs = pl.pallas_call(
        functools.partial(_mod_kernel, n_cast=len(cast)),
        out_shape=[jax.ShapeDtypeStruct((DEPTH, MOD_ROWS, n), F32)] + c_shape,
        grid=(DEPTH, MOD_TILES),
        in_specs=[
            pl.BlockSpec((MOD_ROWS, D_MODEL), lambda l, j: (0, 0)),
            pl.BlockSpec((None, D_MODEL, tn), lambda l, j: (l, 0, j)),
            pl.BlockSpec((None, 1, tn), lambda l, j: (l, 0, j)),
        ] + c_in,
        out_specs=[pl.BlockSpec((None, MOD_ROWS, tn), lambda l, j: (l, 0, j))] + c_out,
        compiler_params=_params("arbitrary", "arbitrary"),
        name="modulation",
    )(cond, ada_w, ada_b.reshape(DEPTH, 1, n), *c_args)
    return (outs[0].reshape(DEPTH, MOD_ROWS, N_MOD, 1, D_MODEL), *outs[1:])


def _mod_spec(layer, k, row_of_block):
    return pl.BlockSpec((None, None, None, 1, D_MODEL),
                        lambda i, *_: (layer, row_of_block(i), k, 0, 0))


def _gain_spec(layer):
    return pl.BlockSpec((None, 1, D_MODEL), lambda i, *_: (layer, 0, 0))


def _resident(block_shape, index_map):
    return pl.BlockSpec(block_shape, index_map, pipeline_mode=pl.Buffered(1))


def _cast_riders(sources, step_of, n_steps):
    args, in_specs, out_shape, out_specs = [], [], [], []
    for arr, lead, nb in sources:
        _, r, c = arr.shape
        rows, stride = r // nb, n_steps // nb
        assert rows * nb == r and stride * nb == n_steps and rows % 16 == 0
        idx = lambda *g, nb=nb, stride=stride: jnp.minimum(step_of(*g) // stride, nb - 1)
        args.append(arr)
        in_specs.append(pl.BlockSpec((None, rows, c), lambda *g, idx=idx, lead=lead: (lead, idx(*g), 0)))
        out_shape.append(jax.ShapeDtypeStruct((r, c), BF16))
        out_specs.append(pl.BlockSpec((rows, c), lambda *g, idx=idx: (idx(*g), 0)))
    return args, in_specs, out_shape, out_specs


def _cast_blocks(src_refs, dst_refs):
    for src, dst in zip(src_refs, dst_refs):
        dst[...] = src[...].astype(BF16)


def _ffn_kernel(*refs, with_mix, with_proj, with_gate, with_final, paired, n_first, n_cast, proj_nt):
    it = iter(refs)
    pairs = iter(paired)

    def rows_in():
        a = next(it)
        if not next(pairs):
            return a[...]
        b = next(it)
        return jnp.where(pl.program_id(0) < n_first, a[...], b[...])

    x = rows_in()
    if with_mix:
        ma, mb = rows_in(), rows_in()
        wm_ref, gm_ref = next(it), next(it)
    sh_ref, sc_ref, g_ref, gain_ref, wi_ref, wo_ref = (next(it) for _ in range(6))
    if with_proj:
        psh_ref, psc_ref, pgain_ref, wp_ref = (next(it) for _ in range(4))
    if with_final:
        fn_ref = next(it)
    cast_src = [next(it) for _ in range(n_cast)]
    o_ref = next(it)
    if with_proj:
        z_ref = next(it)
    if with_gate:
        zg_ref = next(it)
    cast_dst = [next(it) for _ in range(n_cast)]
    a_scr = next(it)
    _cast_blocks(cast_src, cast_dst)

    if with_mix:
        ka = ma.shape[1]
        x = x + gm_ref[...] * (_dot(ma, wm_ref[0:ka, :]) + _dot(mb, wm_ref[ka:, :]))
    xn = (_rms(x) * gain_ref[...] * (1.0 + sc_ref[...]) + sh_ref[...]).astype(BF16)
    for j in range(FFN_HIDDEN // FFN_TILE):
        lo = j * FFN_TILE
        g = _dot(xn, wi_ref[:, lo:lo + FFN_TILE])
        u = _dot(xn, wi_ref[:, FFN_HIDDEN + lo:FFN_HIDDEN + lo + FFN_TILE])
        a_scr[:, lo:lo + FFN_TILE] = (g * _sigmoid(g) * u).astype(BF16)
    out = x + (0.5 * g_ref[...]) * _dot(a_scr[...], wo_ref[...])
    if with_final:
        out = _rms(out) * fn_ref[...]
    o_ref[...] = out
    if with_proj:
        xm = (_rms(out) * pgain_ref[...] * (1.0 + psc_ref[...]) + psh_ref[...]).astype(BF16)
        n_out = z_ref.shape[1]
        project = (lambda lo, hi: _dot_nt(xm, wp_ref[lo:hi, :])) if proj_nt else (lambda lo, hi: _dot(xm, wp_ref[:, lo:hi]))
        for lo in range(0, n_out, PROJ_TILE):
            z_ref[:, lo:lo + PROJ_TILE] = project(lo, lo + PROJ_TILE).astype(z_ref.dtype)
        if with_gate:
            zg_ref[...] = _dot_nt(wp_ref[n_out:, :], xm)


def _ffn(h, mods, layer, mod_base, row_of_block, gain, w_in, w_out, *, tm, n_first=None, n_rows=None, mix=None,
         proj=None, final_gain=None, cast=()):
    m = n_rows or (sum(a.shape[0] for a in h) if isinstance(h, tuple) else h.shape[0])
    row = lambda i: (i, 0)
    args, specs, paired = [], [], []

    def add_rows(a):
        paired.append(isinstance(a, tuple))
        if paired[-1]:
            first, second = a
            assert first.shape[0] == n_first * tm
            args.extend([first, second])
            specs.extend([pl.BlockSpec((tm, first.shape[1]), lambda i: (jnp.minimum(i, n_first - 1), 0)),
                          pl.BlockSpec((tm, second.shape[1]), lambda i: (jnp.maximum(i - n_first, 0), 0))])
        else:
            args.append(a)
            specs.append(pl.BlockSpec((tm, a.shape[1]), row))

    add_rows(h)
    if mix is not None:
        ma, mb, w_mix = mix
        add_rows(ma)
        add_rows(mb)
        args += [w_mix, mods]
        specs += [_resident(w_mix.shape, lambda i: (0, 0)), _mod_spec(layer, 5, row_of_block)]
    args += [mods, mods, mods, gain, w_in, w_out]
    specs += [
        _mod_spec(layer, mod_base, row_of_block),
        _mod_spec(layer, mod_base + 1, row_of_block),
        _mod_spec(layer, mod_base + 2, row_of_block),
        _gain_spec(layer),
        _resident((D_MODEL, 2 * FFN_HIDDEN), lambda i: (0, 0)),
        _resident((FFN_HIDDEN, D_MODEL), lambda i: (0, 0)),
    ]
    out_shape = [jax.ShapeDtypeStruct((m, D_MODEL), F32)]
    out_specs = [pl.BlockSpec((tm, D_MODEL), row)]
    n_gate, proj_nt = 0, False
    if proj is not None:
        pgain, wp, n_gate, proj_nt = proj
        n_out = wp.shape[0 if proj_nt else 1] - n_gate
        assert n_out % PROJ_TILE == 0
        args += [mods, mods, pgain, wp]
        specs += [_mod_spec(layer, 3, row_of_block), _mod_spec(layer, 4, row_of_block), _gain_spec(layer),
                  _resident(wp.shape, lambda i: (0, 0))]
        out_shape.append(jax.ShapeDtypeStruct((m, n_out), BF16))
        out_specs.append(pl.BlockSpec((tm, n_out), row))
        if n_gate:
            assert proj_nt
            out_shape.append(jax.ShapeDtypeStruct((n_gate, m), F32))
            out_specs.append(pl.BlockSpec((n_gate, tm), lambda i: (0, i)))
    if final_gain is not None:
        args.append(final_gain)
        specs.append(pl.BlockSpec((1, D_MODEL), lambda i: (0, 0)))
    c_args, c_in, c_shape, c_out = _cast_riders(cast, lambda i: i, FFN_CAST_STEPS)
    assert not cast or m // tm >= FFN_CAST_STEPS
    args += c_args
    specs += c_in
    out_shape += c_shape
    out_specs += c_out
    outs = pl.pallas_call(
        functools.partial(_ffn_kernel, with_mix=mix is not None, with_proj=proj is not None, with_gate=n_gate > 0,
                          with_final=final_gain is not None, paired=tuple(paired), n_first=n_first,
                          n_cast=len(cast), proj_nt=proj_nt),
        out_shape=out_shape,
        grid=(m // tm,),
        in_specs=specs,
        out_specs=out_specs,
        scratch_shapes=[pltpu.VMEM((tm, FFN_HIDDEN), BF16)],
        compiler_params=_params("arbitrary" if cast else "parallel"),
        name="ffn",
    )(*args)
    return outs if len(outs) > 1 else outs[0]


def _dft_tables(t, tq):
    k = (np.arange(tq, dtype=np.int64)[:, None] * np.arange(t, dtype=np.int64)[None, :]) % t
    ang = 2.0 * np.pi * k.astype(np.float64) / t
    pos = np.concatenate([np.cos(ang), -np.sin(ang)], axis=1).astype(np.float32)
    kc = (np.arange(HEAD_DIM)[:, None] * np.arange(HEAD_DIM)[None, :]) % HEAD_DIM
    angc = 2.0 * np.pi * kc.astype(np.float64) / HEAD_DIM
    eye = np.eye(FOURIER_GROUPS)
    chan = np.concatenate([np.kron(eye, np.cos(angc)), np.kron(eye, np.sin(angc))], axis=1).astype(np.float32)
    return pos, chan


def _dft_kernel(p_ref, z_ref, c_ref, o_ref, p_scr, ab_scr, *, scale, quarter_turns):
    i, b = pl.program_id(0), pl.program_id(1)
    t = p_ref.shape[1] // 2

    @pl.when(b == 0)
    def _():
        col = lax.broadcasted_iota(jnp.int32, (1, t), 1)
        k = (i * quarter_turns * col) & 3
        ca = jnp.where(k == 0, 1.0, jnp.where(k == 2, -1.0, 0.0))
        sa = jnp.where(k == 1, 1.0, jnp.where(k == 3, -1.0, 0.0))
        c0, n0 = p_ref[:, 0:t], p_ref[:, t:]
        p_scr[:, 0:t] = (ca * c0 + sa * n0).astype(BF16)
        p_scr[:, t:] = (ca * n0 - sa * c0).astype(BF16)

    @pl.when(i == 0)
    def _():
        ab = _dot(z_ref[...], c_ref[...].astype(BF16))
        ab_scr[b, 0:t, :] = ab[:, :FOURIER_WIDTH].astype(BF16)
        ab_scr[b, t:, :] = ab[:, FOURIER_WIDTH:].astype(BF16)

    o_ref[...] = (_dot(p_scr[...], ab_scr[b]) * scale).astype(BF16)


def _fourier(z, t, bn, row0=0):
    tq = min(t, DFT_ROWS)
    assert (4 * tq) % t == 0
    pos, chan = _dft_tables(t, tq)
    off = row0 // t
    return pl.pallas_call(
        functools.partial(_dft_kernel, scale=1.0 / math.sqrt(t * HEAD_DIM), quarter_turns=4 * tq // t),
        out_shape=jax.ShapeDtypeStruct((bn * t, FOURIER_WIDTH), BF16),
        grid=(t // tq, bn),
        in_specs=[_resident((tq, 2 * t), lambda i, b: (0, 0)),
                  pl.BlockSpec((t, FOURIER_WIDTH), lambda i, b: (off + b, 0)),
                  pl.BlockSpec((FOURIER_WIDTH, 2 * FOURIER_WIDTH), lambda i, b: (0, 0))],
        out_specs=pl.BlockSpec((tq, FOURIER_WIDTH), lambda i, b: (b * (t // tq) + i, 0)),
        scratch_shapes=[pltpu.VMEM((tq, 2 * t), BF16), pltpu.VMEM((bn, 2 * t, FOURIER_WIDTH), BF16)],
        compiler_params=_params("arbitrary", "arbitrary"),
        name="dft",
    )(jnp.asarray(pos), z, jnp.asarray(chan))


def _gla_segment(q_ref, k_ref, v_ref, r_ref, g_ref, y_ref, gw_ref, gb_ref, gn_ref,
                 b_scr, o_scr, s_scr, n_rows):
    c_len = GLA_CHUNK
    n = n_rows // c_len
    pair_k = 2 * GLA_DK
    pair_v = 2 * GLA_DV

    for d in range(2):
        zg = g_ref[d * GLA_GATE_RANK:(d + 1) * GLA_GATE_RANK, :]
        b_scr[d, 0:n_rows, :] = _dot_tn(zg.astype(BF16), gw_ref[d].astype(BF16)) + gb_ref[d]

    lane = lax.broadcasted_iota(jnp.int32, (c_len, pair_k), 1)
    col = lax.broadcasted_iota(jnp.int32, (c_len, pair_v), 1)
    ti = lax.broadcasted_iota(jnp.int32, (c_len, c_len), 0)
    tj = lax.broadcasted_iota(jnp.int32, (c_len, c_len), 1)

    group = math.gcd(GLA_GROUP, n)
    n_it = n // group
    keep2 = [jnp.concatenate([m, m], axis=0) for m in (tj <= ti, tj >= ti)]
    ti2 = lax.broadcasted_iota(jnp.int32, (c_len, 2 * c_len), 0)
    tj2 = lax.broadcasted_iota(jnp.int32, (c_len, 2 * c_len), 1) & (c_len - 1)
    tri2 = [jnp.where(m, 1.0, 0.0).astype(BF16) for m in (tj2 <= ti2, tj2 >= ti2)]
    gn = gn_ref[...]

    def finish(rows, o):
        for hh in range(2):
            sl = slice(hh * GLA_DV, (hh + 1) * GLA_DV)
            rr = r_ref[rows, sl].astype(F32)
            y_ref[rows, sl] = (_rms(o[:, sl]) * gn * (rr * _sigmoid(rr))).astype(y_ref.dtype)

    def chunk_rows(c):
        start = c * c_len
        return pl.ds(start if isinstance(start, int) else pl.multiple_of(start, c_len), c_len)

    def body(i, carry, phase):
        units = []
        for d in range(2):
            for g in range(group):
                c = i * group + g if d == 0 else n - 1 - (i * group + g)
                rows = chunk_rows(c)
                logit = b_scr[d, rows, :]
                lg = (jnp.minimum(logit, 0.0) - jnp.log(1.0 + jnp.exp(-jnp.abs(logit)))) * (1.0 / GLA_TAU)
                hi = lg.astype(BF16)
                rest = (lg - hi.astype(F32)).astype(BF16)
                units.append(dict(d=d, rows=rows, b=_dot(tri2[d], jnp.concatenate([hi, rest], axis=0))))
        for u in units:
            d, rows, b = u["d"], u["rows"], u["b"]
            bl = b[c_len - 1:c_len, :] if d == 0 else b[0:1, :]
            k = k_ref[rows, :].astype(F32)
            v16 = v_ref[rows, :].astype(BF16)
            qd = q_ref[rows, :].astype(F32) * (GLA_DK ** -0.5) * jnp.exp(b)
            kd = (k * jnp.exp(-b)).astype(BF16)
            kdec = (k * jnp.exp(bl - b)).astype(BF16)
            q2 = jnp.concatenate([jnp.where((lane // GLA_DK) == hh, qd, 0.0) for hh in range(2)],
                                 axis=0).astype(BF16)
            att = _dot_nt(q2, kd)
            upd = _dot_tn(v16, kdec)
            u.update(q2=q2, v16=v16, att=att, upd=upd, dec=jnp.exp(bl))
        for d in range(2):
            s = s_scr[d]
            for u in units:
                if u["d"] == d:
                    u["o"] = _dot_nt(u["q2"], s.astype(BF16))
                    s = s * u["dec"] + u["upd"]
            s_scr[d] = s
        for u in units:
            att = jnp.where(keep2[u["d"]], u["att"], 0.0).astype(BF16)
            pv = _dot(att, u["v16"])
            both = u["o"] + pv
            u["o"] = jnp.where((col // GLA_DV) == 0, both[0:c_len, :], both[c_len:, :])
            if phase == "first":
                o_scr[u["d"], u["rows"], :] = u["o"]
            elif phase == "second":
                finish(u["rows"], u["o"] + o_scr[1 - u["d"], u["rows"], :])
        if phase == "only":
            for g in range(group):
                finish(units[g]["rows"], units[g]["o"] + units[group + n - 1 - g]["o"])
        return carry

    if n_it == 1:
        body(0, 0, "only")
    else:
        assert n_it % 2 == 0
        lax.fori_loop(0, n_it // 2, functools.partial(body, phase="first"), 0)
        lax.fori_loop(n_it // 2, n_it, functools.partial(body, phase="second"), 0)


def _gla_kernel(*refs, t_ctx, t_lat, n_cast):
    qc_ref, kc_ref, vc_ref, rc_ref, gc_ref, ql_ref, kl_ref, vl_ref, rl_ref, gl_ref, gw_ref, gb_ref, gn_ref = refs[:13]
    yc_ref, yl_ref = refs[13 + n_cast:15 + n_cast]
    b_scr, o_scr, s_scr = refs[-3:]

    @pl.when(pl.program_id(1) == 0)
    def _():
        _cast_blocks(refs[13:13 + n_cast], refs[15 + n_cast:15 + 2 * n_cast])

    s_scr[...] = jnp.zeros_like(s_scr)
    _gla_segment(qc_ref, kc_ref, vc_ref, rc_ref, gc_ref, yc_ref, gw_ref, gb_ref, gn_ref,
                 b_scr, o_scr, s_scr, t_ctx)
    _gla_segment(ql_ref, kl_ref, vl_ref, rl_ref, gl_ref, yl_ref, gw_ref, gb_ref, gn_ref,
                 b_scr, o_scr, s_scr, t_lat)


def _gla(z, zg, gate_w, gate_b, gla_g, bn, t_ctx, t_lat, ctx_row0, cast=()):
    pk, pv = 2 * GLA_DK, 2 * GLA_DV
    q0 = FOURIER_WIDTH // pk
    k0 = (FOURIER_WIDTH + GLA_HEADS * GLA_DK) // pk
    v0 = (FOURIER_WIDTH + 2 * GLA_HEADS * GLA_DK) // pv
    r0 = (FOURIER_WIDTH + 2 * GLA_HEADS * GLA_DK + GLA_HEADS * GLA_DV) // pv

    def seg_specs(t, row0):
        off = row0 // t
        return [pl.BlockSpec((t, pk), lambda b, p: (off + b, q0 + p)),
                pl.BlockSpec((t, pk), lambda b, p: (off + b, k0 + p)),
                pl.BlockSpec((t, pv), lambda b, p: (off + b, v0 + p)),
                pl.BlockSpec((t, pv), lambda b, p: (off + b, r0 + p)),
                pl.BlockSpec((2 * GLA_GATE_RANK, t), lambda b, p: (0, off + b))]

    wdt = GLA_HEADS * GLA_DV
    c_args, c_in, c_shape, c_out = _cast_riders(cast, lambda b, p: b, bn)
    return pl.pallas_call(
        functools.partial(_gla_kernel, t_ctx=t_ctx, t_lat=t_lat, n_cast=len(cast)),
        out_shape=[jax.ShapeDtypeStruct((bn * t_ctx, wdt), BF16), jax.ShapeDtypeStruct((bn * t_lat, wdt), BF16)] + c_shape,
        grid=(bn, GLA_HEADS // 2),
        in_specs=seg_specs(t_ctx, ctx_row0) + seg_specs(t_lat, 0) + [
            pl.BlockSpec((2, GLA_GATE_RANK, pk), lambda b, p: (0, 0, p)),
            pl.BlockSpec((2, 1, pk), lambda b, p: (0, 0, p)),
            pl.BlockSpec((1, GLA_DV), lambda b, p: (0, 0))] + c_in,
        out_specs=[pl.BlockSpec((t_ctx, pv), lambda b, p: (b, p)),
                   pl.BlockSpec((t_lat, pv), lambda b, p: (b, p))] + c_out,
        scratch_shapes=[pltpu.VMEM((2, t_lat, pk), F32), pltpu.VMEM((2, t_lat, pv), F32),
                        pltpu.VMEM((2, pv, pk), F32)],
        compiler_params=_params("arbitrary", "arbitrary"),
        name="gla",
    )(z, z, z, z, zg, z, z, z, z, zg, gate_w, gate_b.reshape(2, 1, GLA_HEADS * GLA_DK),
      gla_g.reshape(1, GLA_DV), *c_args)


def _short_conv_block(zb_ref, zc_ref, zx_ref, w_ref, b_ref, o_ref):
    u = zc_ref[...].astype(F32) * zx_ref[...].astype(F32)
    t = u.shape[0]
    row = lax.broadcasted_iota(jnp.int32, u.shape, 0)
    prev = jnp.where(row >= 1, pltpu.roll(u, 1, 0), 0.0)
    nxt = jnp.where(row < t - 1, pltpu.roll(u, t - 1, 0), 0.0)
    y = prev * w_ref[0:1, :] + u * w_ref[1:2, :] + nxt * w_ref[2:3, :] + b_ref[...]
    o_ref[...] = (zb_ref[...].astype(F32) * y).astype(o_ref.dtype)


def _rope_tables(t):
    rows = t // GRID_W
    row = jnp.repeat(jnp.arange(rows), GRID_W).astype(F32)
    col = jnp.tile(jnp.arange(GRID_W), rows).astype(F32)
    n = ROPE_AXIS_DIM // 2
    inv = ROPE_THETA ** (-jnp.arange(n, dtype=F32) / n)
    ar, ac = row[:, None] * inv, col[:, None] * inv
    ang = jnp.concatenate([ar, ar, ac, ac], axis=-1)
    sign = jnp.tile(jnp.concatenate([-jnp.ones((n,), F32), jnp.ones((n,), F32)]), 2)
    cos = jnp.cos(ang)
    sin = jnp.sin(ang) * sign
    return jnp.tile(cos, (1, 2)), jnp.tile(sin, (1, 2))


def _rope(x, cos, sin):
    lane = lax.broadcasted_iota(jnp.int32, x.shape, 1)
    n = ROPE_AXIS_DIM // 2
    w = x.shape[1]
    partner = jnp.where((lane & (2 * n - 1)) < n, pltpu.roll(x, w - n, 1), pltpu.roll(x, n, 1))
    return x * cos + partner * sin


ATTN_ONES_ROWS = 16
ATTN_HEADS_PER_STEP = 3


def _attn_kernel(*refs, t_lat, lam_init, tq, n_cast):
    lam_ref, q_ref, kl_ref, vl_ref, kc_ref, vc_ref, cos_ref, sin_ref, dn_ref = refs[:9]
    conv_in = refs[9:14]
    n_in = 14 + n_cast
    o_ref, conv_ref = refs[n_in:n_in + 2]
    k_scr, vt_scr, s0_scr, m0_scr, s1_scr, m1_scr = refs[-6:]

    _cast_blocks(refs[14:n_in], refs[n_in + 2:n_in + 2 + n_cast])

    @pl.when(pl.program_id(1) == 0)
    def _():
        _short_conv_block(*conv_in, conv_ref)

    hw = 2 * HEAD_DIM
    t_all = k_scr.shape[1]

    for hd in range(ATTN_HEADS_PER_STEP):
        cols = slice(hd * hw, (hd + 1) * hw)
        k_scr[hd, 0:t_lat, :] = _rope(kl_ref[:, cols].astype(F32), cos_ref[...], sin_ref[...]).astype(BF16)
        k_scr[hd, t_lat:, :] = kc_ref[:, cols].astype(BF16)
        vt_scr[hd, 0:hw, 0:t_lat] = vl_ref[:, cols].astype(F32).T.astype(BF16)
        vt_scr[hd, 0:hw, t_lat:] = vc_ref[:, cols].astype(F32).T.astype(BF16)
        vt_scr[hd, hw:, :] = jnp.ones((ATTN_ONES_ROWS, t_all), BF16)

    lv = lam_ref[...]
    lam = (jnp.exp(jnp.sum(lv[0:1] * lv[1:2], axis=-1, keepdims=True))
           - jnp.exp(jnp.sum(lv[2:3] * lv[3:4], axis=-1, keepdims=True)) + lam_init)
    lane = lax.broadcasted_iota(jnp.int32, (tq, hw), 1)

    n_blocks = t_lat // tq
    slots = ((s0_scr, m0_scr), (s1_scr, m1_scr))

    def scores(hd, i, slot):
        rows = pl.ds(pl.multiple_of(i * tq, tq), tq)
        q = (_rope(q_ref[rows, hd * hw:(hd + 1) * hw].astype(F32), cos_ref[rows, :], sin_ref[rows, :])
             * (HEAD_DIM ** -0.5 * math.log2(math.e)))
        s_scr, m_scr = slots[slot]
        for half in range(2):
            qb = jnp.where((lane // HEAD_DIM) == half, q, 0.0).astype(BF16)
            s = _dot_nt(k_scr[hd], qb)
            s_scr[half] = s
            m_scr[half] = jnp.max(s, axis=0, keepdims=True)

    def outputs(hd, i, slot):
        rows = pl.ds(pl.multiple_of(i * tq, tq), tq)
        s_scr, m_scr = slots[slot]
        outs = []
        for half in range(2):
            p = jnp.exp2(s_scr[half] - m_scr[half]).astype(BF16)
            acc = _dot(vt_scr[hd], p)
            outs.append(acc[0:hw, :] / acc[hw:hw + 1, :])
        o = (outs[0] - lam * outs[1]).T
        o_ref[rows, hd * hw:(hd + 1) * hw] = (_rms(o) * dn_ref[...] * (1.0 - lam_init)).astype(o_ref.dtype)

    assert n_blocks % 2 == 0
    scores(0, 0, 0)
    for hd in range(ATTN_HEADS_PER_STEP):
        def body(j, carry, hd=hd):
            scores(hd, 2 * j + 1, 1)
            outputs(hd, 2 * j, 0)
            scores(hd, 2 * j + 2, 0)
            outputs(hd, 2 * j + 1, 1)
            return carry

        lax.fori_loop(0, n_blocks // 2 - 1, body, 0)
        scores(hd, n_blocks - 1, 1)
        outputs(hd, n_blocks - 2, 0)
        if hd + 1 < ATTN_HEADS_PER_STEP:
            scores(hd + 1, 0, 0)
        outputs(hd, n_blocks - 1, 1)


def _odd_mixers(z, lam_vecs, dnorm, lam_init, conv_w, conv_b, bn, t_lat, t_ctx, ctx_row0, cast=()):
    hw = 2 * HEAD_DIM
    wb = ATTN_HEADS_PER_STEP * hw
    q0 = 3 * CONV_WIDTH // wb
    k0 = q0 + DIFF_HEADS // ATTN_HEADS_PER_STEP
    v0 = k0 + DIFF_HEADS // ATTN_HEADS_PER_STEP
    tq = ATTN_Q_BLOCK
    t_all = t_lat + t_ctx
    coff = ctx_row0 // t_ctx
    cos, sin = _rope_tables(t_lat)
    groups = DIFF_HEADS // ATTN_HEADS_PER_STEP
    c_args, c_in, c_shape, c_out = _cast_riders(cast, lambda b, p: b * groups + p, bn * groups)
    outs = pl.pallas_call(
        functools.partial(_attn_kernel, t_lat=t_lat, lam_init=lam_init, tq=tq, n_cast=len(cast)),
        out_shape=[jax.ShapeDtypeStruct((bn * t_lat, DIFF_HEADS * DIFF_DV), BF16),
                   jax.ShapeDtypeStruct((bn * t_lat, CONV_WIDTH), BF16)] + c_shape,
        grid=(bn, groups),
        in_specs=[
            pl.BlockSpec((4, HEAD_DIM), lambda b, p: (0, 0)),
            pl.BlockSpec((t_lat, wb), lambda b, p: (b, q0 + p)),
            pl.BlockSpec((t_lat, wb), lambda b, p: (b, k0 + p)),
            pl.BlockSpec((t_lat, wb), lambda b, p: (b, v0 + p)),
            pl.BlockSpec((t_ctx, wb), lambda b, p: (coff + b, k0 + p)),
            pl.BlockSpec((t_ctx, wb), lambda b, p: (coff + b, v0 + p)),
            pl.BlockSpec((t_lat, hw), lambda b, p: (0, 0)),
            pl.BlockSpec((t_lat, hw), lambda b, p: (0, 0)),
            pl.BlockSpec((1, DIFF_DV), lambda b, p: (0, 0)),
            pl.BlockSpec((t_lat, CONV_WIDTH), lambda b, p: (b, 0)),
            pl.BlockSpec((t_lat, CONV_WIDTH), lambda b, p: (b, 1)),
            pl.BlockSpec((t_lat, CONV_WIDTH), lambda b, p: (b, 2)),
            pl.BlockSpec((3, CONV_WIDTH), lambda b, p: (0, 0)),
            pl.BlockSpec((1, CONV_WIDTH), lambda b, p: (0, 0)),
        ] + c_in,
        out_specs=[pl.BlockSpec((t_lat, wb), lambda b, p: (b, p)),
                   pl.BlockSpec((t_lat, CONV_WIDTH), lambda b, p: (b, 0))] + c_out,
        scratch_shapes=[pltpu.VMEM((ATTN_HEADS_PER_STEP, t_all, hw), BF16),
                        pltpu.VMEM((ATTN_HEADS_PER_STEP, hw + ATTN_ONES_ROWS, t_all), BF16),
                        pltpu.VMEM((2, t_all, tq), F32), pltpu.VMEM((2, 1, tq), F32),
                        pltpu.VMEM((2, t_all, tq), F32), pltpu.VMEM((2, 1, tq), F32)],
        compiler_params=_params("arbitrary", "arbitrary"),
        name="diff_attn",
    )(lam_vecs, z, z, z, z, z, cos, sin, dnorm.reshape(1, DIFF_DV), z, z, z, conv_w,
      conv_b.reshape(1, CONV_WIDTH), *c_args)
    return outs


def kernel(x, c, ctx, c_ctx, ada_w, ada_b, norm_ffn1, norm_mix, norm_ffn2, ffn1_w_in, ffn1_w_out, ffn2_w_in,
           ffn2_w_out, mix_w_out, even_w_in, gla_gate_w, gla_gate_b, gla_norm, odd_w_in, conv_w, conv_b,
           lambda_q1, lambda_k1, lambda_q2, lambda_k2, diff_norm, final_norm):
    assert DEPTH == 2
    bn, t_lat, d = x.shape
    t_ctx = ctx.shape[1]
    assert bn < MOD_ROWS
    ctx_row = bn

    cond = jnp.concatenate([c, c_ctx[None, :], jnp.zeros((MOD_ROWS - bn - 1, d), F32)], axis=0)
    n_mod_steps = DEPTH * MOD_TILES
    mods, w1i0, w1o0 = _modulation(cond, ada_w, ada_b,
                                   cast=((ffn1_w_in, 0, n_mod_steps), (ffn1_w_out, 0, n_mod_steps)))
    w_even_t = jnp.swapaxes(even_w_in[0], 0, 1).astype(BF16)

    tm = FFN_ROWS
    n_lat = bn * t_lat
    lat_blocks = n_lat // tm
    lat_row = lambda i: i // (t_lat // tm)
    all_row = lambda i: jnp.where(i < lat_blocks, i // (t_lat // tm), ctx_row)
    g1 = norm_ffn1.reshape(DEPTH, 1, d)
    gm = norm_mix.reshape(DEPTH, 1, d)
    g2 = norm_ffn2.reshape(DEPTH, 1, d)
    cast_gla = ((ffn2_w_in, 0, bn), (ffn2_w_out, 0, bn), (mix_w_out, 0, bn))
    cast_ffn = ((ffn1_w_in, 1, FFN_CAST_STEPS), (ffn1_w_out, 1, FFN_CAST_STEPS // 2), (odd_w_in, 0, FFN_CAST_STEPS))
    n_attn = bn * (DIFF_HEADS // ATTN_HEADS_PER_STEP)
    cast_attn = ((ffn2_w_in, 1, n_attn), (ffn2_w_out, 1, n_attn), (mix_w_out, 1, n_attn))

    proj0 = (gm, w_even_t, 2 * GLA_GATE_RANK, True)
    h, z, zg = _ffn((x.reshape(n_lat, d), ctx.reshape(bn * t_ctx, d)), mods, 0, 0, all_row, g1, w1i0, w1o0,
                    tm=tm, n_first=lat_blocks, proj=proj0)
    yf_l = _fourier(z, t_lat, bn)
    yf_c = _fourier(z, t_ctx, bn, row0=n_lat)
    yg_c, yg_l, w2i0, w2o0, wmix0 = _gla(z, zg, gla_gate_w[0], gla_gate_b[0], gla_norm[0], bn, t_ctx, t_lat, n_lat,
                                         cast=cast_gla)
    h, w1i1, w1o1, w_odd = _ffn(h, mods, 0, 6, all_row, g2, w2i0, w2o0, tm=tm, n_first=lat_blocks,
                                mix=((yf_l, yf_c), (yg_l, yg_c), wmix0), cast=cast_ffn)

    h, z = _ffn(h, mods, 1, 0, all_row, g1, w1i1, w1o1, tm=tm, proj=(gm, w_odd, 0, False))
    lam_init = 0.8 - 0.6 * math.exp(-0.3 * 1)
    lam_vecs = jnp.stack([lambda_q1[0], lambda_k1[0], lambda_q2[0], lambda_k2[0]]).astype(F32)
    y_att, y_conv, w2i1, w2o1, wmix1 = _odd_mixers(z, lam_vecs, diff_norm[0], lam_init, conv_w[0], conv_b[0], bn,
                                                   t_lat, t_ctx, n_lat, cast=cast_attn)
    h = _ffn(h, mods, 1, 6, lat_row, g2, w2i1, w2o1, tm=tm, mix=(y_conv, y_att, wmix1),
             final_gain=final_norm.reshape(1, d), n_rows=n_lat)
    return h.reshape(bn, t_lat, d)
```
